```python
import jax, jax.numpy as jnp
from jax import lax
import numpy as np

D_MODEL = 1024
BATCH = 2
SEQ = 8192
DEPTH = 2

N_MIXERS = 2

HG_HEADS = 8
HG_DK = D_MODEL // HG_HEADS
HG_DV = D_MODEL // HG_HEADS
HG_KDIM = HG_HEADS * HG_DK
HG_VDIM = HG_HEADS * HG_DV
HG_CHUNK = 64
N_HGRN = (DEPTH + 1) // 2

LRU_WIDTH = D_MODEL
LRU_BLOCKS = 4
LRU_BW = LRU_WIDTH // LRU_BLOCKS
CONV_WIDTH = 4
LRU_C = 8.0
N_LRU = DEPTH // 2

D_FF = ((8 * D_MODEL // 3 + 255) // 256) * 256

EPS = 1e-6

kernel_name = "hgrn2_rglru_interleaved_trunk"


def rmsnorm(x, w):
    xf = x.astype(jnp.float32)
    y = xf * lax.rsqrt(jnp.mean(xf * xf, axis=-1, keepdims=True) + EPS) * w.astype(jnp.float32)
    return y.astype(x.dtype)


def hgrn2_mixer(h, w_in, lb, norm_w, w_out):
    B, S, _ = h.shape
    C = HG_CHUNK
    nc = S // C
    proj = h @ w_in
    q, fl, v, g = jnp.split(proj, [HG_KDIM, 2 * HG_KDIM, 2 * HG_KDIM + HG_VDIM], axis=-1)
    q = jax.nn.silu(q.astype(jnp.float32))
    fl = fl.astype(jnp.float32)
    lbv = lb.astype(jnp.float32)
    log_f = jnp.logaddexp(jnp.log(lbv), jnp.log1p(-lbv) + jax.nn.log_sigmoid(fl))
    k = (1.0 - lbv) * jax.nn.sigmoid(-fl)

    def to_chunks(t, d):
        return t.reshape(B, nc, C, HG_HEADS, d).transpose(1, 0, 3, 2, 4)

    qc = to_chunks(q, HG_DK)
    kc = to_chunks(k, HG_DK)
    vc = to_chunks(v.astype(jnp.float32), HG_DV)
    bc = jnp.cumsum(to_chunks(log_f, HG_DK), axis=3)
    causal = jnp.tril(jnp.ones((C, C), dtype=bool))[:, :, None]

    def step(state, inp):
        qb, kb, vb, bb = inp
        rel = bb[:, :, :, None, :] - bb[:, :, None, :, :]
        decay = jnp.exp(jnp.where(causal, rel, -jnp.inf))
        scores = jnp.einsum('bhtk,bhtsk,bhsk->bhts', qb, decay, kb)
        o = jnp.einsum('bhts,bhsv->bhtv', scores, vb) + jnp.einsum('bhtk,bhkv->bhtv', qb * jnp.exp(bb), state)
        b_last = bb[:, :, -1, :]
        state = jnp.exp(b_last)[..., None] * state + jnp.einsum(
            'bhsk,bhsv->bhkv', kb * jnp.exp(b_last[:, :, None, :] - bb), vb)
        return state, o

    s0 = jnp.zeros((B, HG_HEADS, HG_DK, HG_DV), jnp.float32)
    _, oc = lax.scan(step, s0, (qc, kc, vc, bc))
    o = oc.transpose(1, 0, 3, 2, 4).reshape(B, S, HG_HEADS, HG_DV)
    o = o * lax.rsqrt(jnp.mean(o * o, axis=-1, keepdims=True) + EPS) * norm_w.astype(jnp.float32)
    o = o.reshape(B, S, HG_VDIM) * jax.nn.silu(g.astype(jnp.float32))
    return o.astype(h.dtype) @ w_out


def rglru_mixer(h, w_in, conv_w, conv_b, wa, ba, wx, bx, lam, w_out):
    B, S, _ = h.shape
    proj = h @ w_in
    y, u = jnp.split(proj, [LRU_WIDTH], axis=-1)
    y = jax.nn.gelu(y.astype(jnp.float32))
    u = u.astype(jnp.float32)
    u_pad = jnp.pad(u, ((0, 0), (CONV_WIDTH - 1, 0), (0, 0)))
    cw = conv_w.astype(jnp.float32)
    uc = conv_b.astype(jnp.float32)
    for tap in range(CONV_WIDTH):
        uc = uc + u_pad[:, tap:tap + S, :] * cw[tap]
    ub = uc.reshape(B, S, LRU_BLOCKS, LRU_BW)
    r = jax.nn.sigmoid(jnp.einsum('bsnc,ncd->bsnd', ub, wa.astype(jnp.float32)).reshape(B, S, LRU_WIDTH)
                       + ba.astype(jnp.float32))
    ig = jax.nn.sigmoid(jnp.einsum('bsnc,ncd->bsnd', ub, wx.astype(jnp.float32)).reshape(B, S, LRU_WIDTH)
                        + bx.astype(jnp.float32))
    log_a = -LRU_C * r * jax.nn.softplus(-lam.astype(jnp.float32))
    a = jnp.exp(log_a)
    b_in = jnp.sqrt(-jnp.expm1(2.0 * log_a)) * (ig * uc)

    def combine(left, right):
        a1, b1 = left
        a2, b2 = right
        return a1 * a2, a2 * b1 + b2

    _, hs = lax.associative_scan(combine, (a, b_in), axis=1)
    return (hs * y).astype(h.dtype) @ w_out


def swiglu(h, w_in, w_out):
    gate, up = jnp.split(h @ w_in, [D_FF], axis=-1)
    return (jax.nn.silu(gate) * up) @ w_out


def setup_inputs(seed: int = 0) -> dict:
    key = jax.random.key(seed)
    ks = jax.random.split(key, 24)
    f32 = jnp.float32
    D = D_MODEL

    def nrm(k, shape, fan_in):
        return jax.random.normal(k, shape, f32) * (fan_in ** -0.5)

    x = jax.random.normal(ks[0], (BATCH, SEQ, D), f32)
    norm_mix = 1.0 + 0.02 * jax.random.normal(ks[1], (DEPTH, D), f32)
    norm_ffn = 1.0 + 0.02 * jax.random.normal(ks[2], (DEPTH, D), f32)
    norm_final = 1.0 + 0.02 * jax.random.normal(ks[3], (D,), f32)

    hgrn_w_in = nrm(ks[4], (N_HGRN, D, 2 * HG_KDIM + 2 * HG_VDIM), D)
    hgrn_lb = 0.1 * jax.random.normal(ks[5], (N_HGRN + 1, HG_KDIM), f32)
    hgrn_norm = 1.0 + 0.02 * jax.random.normal(ks[6], (N_HGRN, HG_DV), f32)
    hgrn_w_out = nrm(ks[7], (N_HGRN, HG_VDIM, D), HG_VDIM)

    lru_w_in = nrm(ks[8], (N_LRU, D, 2 * LRU_WIDTH), D)
    lru_conv_w = nrm(ks[9], (N_LRU, CONV_WIDTH, LRU_WIDTH), CONV_WIDTH)
    lru_conv_b = 0.01 * jax.random.normal(ks[10], (N_LRU, LRU_WIDTH), f32)
    lru_wa = nrm(ks[11], (N_LRU, LRU_BLOCKS, LRU_BW, LRU_BW), LRU_BW)
    lru_ba = 0.01 * jax.random.normal(ks[12], (N_LRU, LRU_WIDTH), f32)
    lru_wx = nrm(ks[13], (N_LRU, LRU_BLOCKS, LRU_BW, LRU_BW), LRU_BW)
    lru_bx = 0.01 * jax.random.normal(ks[14], (N_LRU, LRU_WIDTH), f32)
    a0 = jax.random.uniform(ks[15], (N_LRU, LRU_WIDTH), f32, 0.9, 0.999)
    s = a0 ** (1.0 / LRU_C)
    lru_lambda = jnp.log(s) - jnp.log1p(-s)
    lru_w_out = nrm(ks[16], (N_LRU, LRU_WIDTH, D), LRU_WIDTH)

    ffn_w_in = nrm(ks[17], (DEPTH, D, 2 * D_FF), D)
    ffn_w_out = nrm(ks[18], (DEPTH, D_FF, D), D_FF)

    return {"x": x, "norm_mix": norm_mix, "norm_ffn": norm_ffn, "norm_final": norm_final,
            "hgrn_w_in": hgrn_w_in, "hgrn_lb": hgrn_lb, "hgrn_norm": hgrn_norm, "hgrn_w_out": hgrn_w_out,
            "lru_w_in": lru_w_in, "lru_conv_w": lru_conv_w, "lru_conv_b": lru_conv_b,
            "lru_wa": lru_wa, "lru_ba": lru_ba, "lru_wx": lru_wx, "lru_bx": lru_bx,
            "lru_lambda": lru_lambda, "lru_w_out": lru_w_out,
            "ffn_w_in": ffn_w_in, "ffn_w_out": ffn_w_out}


def reference(x, norm_mix, norm_ffn, norm_final, hgrn_w_in, hgrn_lb, hgrn_norm, hgrn_w_out,
              lru_w_in, lru_conv_w, lru_conv_b, lru_wa, lru_ba, lru_wx, lru_bx, lru_lambda, lru_w_out,
              ffn_w_in, ffn_w_out):
    lb_all = jnp.cumsum(jax.nn.softmax(hgrn_lb.astype(jnp.float32), axis=0), axis=0)
    for l in range(DEPTH):
        j = l // N_MIXERS
        hn = rmsnorm(x, norm_mix[l])
        if l % N_MIXERS == 0:
            mix = hgrn2_mixer(hn, hgrn_w_in[j], lb_all[j], hgrn_norm[j], hgrn_w_out[j])
        else:
            mix = rglru_mixer(hn, lru_w_in[j], lru_conv_w[j], lru_conv_b[j], lru_wa[j], lru_ba[j],
                              lru_wx[j], lru_bx[j], lru_lambda[j], lru_w_out[j])
        x = x + mix
        x = x + swiglu(rmsnorm(x, norm_ffn[l]), ffn_w_in[l], ffn_w_out[l])
    return rmsnorm(x, norm_final)
```

```python
import functools
import math

import jax
import jax.numpy as jnp
from jax import lax
from jax.experimental import pallas as pl
from jax.experimental.pallas import tpu as pltpu

EPS = 1e-6
LRU_C = 8.0
MXU_DTYPE = jnp.bfloat16

HGRN_CHUNK = 128
HGRN_SAFE_EXP = 80.0
HGRN_BLOCK = 256
LRU_BLOCK = 256
FFN_BLOCK = 512
FFN_TILE = 256
VMEM_LIMIT_BYTES = 56 * 1024 * 1024


def _dot(a, b):
    return jnp.dot(a, b, preferred_element_type=jnp.float32)


def _dot_nt(a, b):
    return lax.dot_general(a, b, (((1,), (1,)), ((), ())), preferred_element_type=jnp.float32)


def _rmsnorm(x, w):
    return x * lax.rsqrt(jnp.mean(x * x, axis=-1, keepdims=True) + EPS) * w


def _sigmoid(x):
    return 1.0 / (1.0 + jnp.exp(-x))


def _split3(x):
    hi = x.astype(MXU_DTYPE)
    r1 = x - hi.astype(jnp.float32)
    mid = r1.astype(MXU_DTYPE)
    lo = (r1 - mid.astype(jnp.float32)).astype(MXU_DTYPE)
    return hi, mid, lo


def _select_sum(sel, parts):
    hi, mid, lo = parts
    return _dot(sel, hi) + _dot(sel, mid) + _dot(sel, lo)


def _const_spec(shape):
    return pl.BlockSpec(shape, lambda *_: (0,) * len(shape), pipeline_mode=pl.Buffered(1))


def _ffn_kernel(x_ref, nw_ref, wg_ref, wu_ref, wo_ref, fnw_ref, o_ref, *, n_tiles, final_norm):
    x = x_ref[...]
    hn = _rmsnorm(x, nw_ref[...]).astype(MXU_DTYPE)
    acc = x
    for j in range(n_tiles):
        g = _dot(hn, wg_ref[j])
        u = _dot(hn, wu_ref[j])
        a = (g * _sigmoid(g) * u).astype(MXU_DTYPE)
        acc = acc + _dot(a, wo_ref[j])
    if final_norm:
        acc = _rmsnorm(acc, fnw_ref[...])
    o_ref[...] = acc


def _ffn(x2, norm_w, w_in, w_out, final_norm_w, final_norm):
    t, d = x2.shape
    d_ff = w_out.shape[0]
    n_tiles = d_ff // FFN_TILE
    assert n_tiles * FFN_TILE == d_ff and t % FFN_BLOCK == 0
    wg = w_in[:, :d_ff].reshape(d, n_tiles, FFN_TILE).transpose(1, 0, 2).astype(MXU_DTYPE)
    wu = w_in[:, d_ff:].reshape(d, n_tiles, FFN_TILE).transpose(1, 0, 2).astype(MXU_DTYPE)
    wo = w_out.reshape(n_tiles, FFN_TILE, d).astype(MXU_DTYPE)
    row = pl.BlockSpec((FFN_BLOCK, d), lambda i: (i, 0))
    return pl.pallas_call(
        functools.partial(_ffn_kernel, n_tiles=n_tiles, final_norm=final_norm),
        grid=(t // FFN_BLOCK,),
        in_specs=[row, _const_spec((1, d)), _const_spec(wg.shape), _const_spec(wu.shape),
                  _const_spec(wo.shape), _const_spec((1, d))],
        out_specs=row,
        out_shape=jax.ShapeDtypeStruct((t, d), jnp.float32),
        compiler_params=pltpu.CompilerParams(dimension_semantics=("arbitrary",),
                                             vmem_limit_bytes=VMEM_LIMIT_BYTES),
        name="swiglu_ffn",
    )(x2, norm_w.reshape(1, d), wg, wu, wo, final_norm_w.reshape(1, d))


def _hgrn_scores_single_ref(q_c, k_c, b, heads, dk, sc_ref):
    c = q_c.shape[0]
    qd = (q_c * jnp.exp(b)).astype(MXU_DTYPE)
    kd = (k_c * jnp.exp(-b)).astype(MXU_DTYPE)
    row = lax.broadcasted_iota(jnp.int32, (c, c), 0)
    col = lax.broadcasted_iota(jnp.int32, (c, c), 1)
    causal = col <= row
    for h in range(heads):
        sl = slice(h * dk, (h + 1) * dk)
        s = _dot_nt(qd[:, sl], kd[:, sl])
        sc_ref[h] = jnp.where(causal, s, 0.0).astype(MXU_DTYPE)


def _hgrn_scores_halving(q_c, k_c, lf_parts, heads, dk, sc_ref):
    c = q_c.shape[0]
    row = lax.broadcasted_iota(jnp.int32, (c, c), 0)
    col = lax.broadcasted_iota(jnp.int32, (c, c), 1)
    acc = []
    for h in range(heads):
        sl = slice(h * dk, (h + 1) * dk)
        s = _dot_nt(q_c[:, sl].astype(MXU_DTYPE), k_c[:, sl].astype(MXU_DTYPE))
        acc.append(jnp.where(row == col, s, 0.0))
    m = c // 2
    while m >= 1:
        blk = 2 * m
        ref_of_row = (row // blk) * blk + (m - 1)
        ref_of_col = (col // blk) * blk + (m - 1)
        row_upper = (row % blk) >= m
        sel_q = (row_upper & (col > ref_of_row) & (col <= row)).astype(MXU_DTYPE)
        sel_k = ((~row_upper) & (col > row) & (col <= ref_of_row)).astype(MXU_DTYPE)
        qd = (q_c * jnp.exp(_select_sum(sel_q, lf_parts))).astype(MXU_DTYPE)
        kd = (k_c * jnp.exp(_select_sum(sel_k, lf_parts))).astype(MXU_DTYPE)
        pair = (ref_of_row == ref_of_col) & row_upper & ((col % blk) < m)
        for h in range(heads):
            sl = slice(h * dk, (h + 1) * dk)
            acc[h] = acc[h] + jnp.where(pair, _dot_nt(qd[:, sl], kd[:, sl]), 0.0)
        m //= 2
    for h in range(heads):
        sc_ref[h] = acc[h].astype(MXU_DTYPE)


def _hgrn_kernel(x_ref, nw_ref, win_ref, lbp_ref, gnw_ref, wout_ref, o_ref,
                 st_ref, proj_ref, k_ref, sc_ref, y_ref, *, heads, dk, lb_row):
    kd_total = heads * dk
    tb = x_ref.shape[0]
    c = HGRN_CHUNK
    n_chunks = tb // c

    @pl.when(pl.program_id(1) == 0)
    def _():
        st_ref[...] = jnp.zeros_like(st_ref)

    x = x_ref[...]
    hn = _rmsnorm(x, nw_ref[...]).astype(MXU_DTYPE)
    proj = _dot(hn, win_ref[...])

    lp = lbp_ref[...]
    e = jnp.exp(lp - jnp.max(lp, axis=0, keepdims=True))
    lb = jnp.sum(e[: lb_row + 1], axis=0, keepdims=True) / jnp.sum(e, axis=0, keepdims=True)
    log_lb = jnp.log(lb)
    one_m_lb = 1.0 - lb
    log_one_m_lb = jnp.log1p(-lb)

    q = proj[:, :kd_total]
    fl = proj[:, kd_total:2 * kd_total]
    g = proj[:, 3 * kd_total:]
    en = jnp.exp(-jnp.abs(fl))
    rn = 1.0 / (1.0 + en)
    log_sig = jnp.minimum(fl, 0.0) - jnp.log1p(en)
    kk = one_m_lb * jnp.where(fl >= 0.0, en * rn, rn)
    bq = log_one_m_lb + log_sig
    log_f = jnp.maximum(log_lb, bq) + jnp.log1p(jnp.exp(-jnp.abs(log_lb - bq)))

    proj_ref[:, :kd_total] = q * _sigmoid(q)
    proj_ref[:, kd_total:2 * kd_total] = log_f
    proj_ref[:, 2 * kd_total:3 * kd_total] = proj[:, 2 * kd_total:3 * kd_total]
    proj_ref[:, 3 * kd_total:] = g * _sigmoid(g)
    k_ref[...] = kk

    worst = jnp.float32(0.0)
    for ci in range(n_chunks):
        tot = jnp.sum(log_f[ci * c:(ci + 1) * c], axis=0, keepdims=True)
        worst = jnp.minimum(worst, jnp.min(tot))
    single_ref_ok = worst >= -HGRN_SAFE_EXP

    row = lax.broadcasted_iota(jnp.int32, (c, c), 0)
    col = lax.broadcasted_iota(jnp.int32, (c, c), 1)
    tril = (col <= row).astype(MXU_DTYPE)
    gnw = gnw_ref[...]

    def chunk_body(single_ref):
        for ci in range(n_chunks):
            rows = slice(ci * c, (ci + 1) * c)
            q_c = proj_ref[rows, :kd_total]
            lf_parts = _split3(proj_ref[rows, kd_total:2 * kd_total])
            k_c = k_ref[rows, :]
            b = _select_sum(tril, lf_parts)
            if single_ref:
                _hgrn_scores_single_ref(q_c, k_c, b, heads, dk, sc_ref)
            else:
                _hgrn_scores_halving(q_c, k_c, lf_parts, heads, dk, sc_ref)
            b_last = b[c - 1:c, :]
            qd = (q_c * jnp.exp(b)).astype(MXU_DTYPE)
            ke = (k_c * jnp.exp(b_last - b)).astype(MXU_DTYPE)
            dec = jnp.exp(b_last)
            for h in range(heads):
                sl = slice(h * dk, (h + 1) * dk)
                v_h = proj_ref[rows, 2 * kd_total + h * dk:2 * kd_total + (h + 1) * dk]
                v_b = v_h.astype(MXU_DTYPE)
                st = st_ref[h]
                o = _dot(sc_ref[h], v_b) + _dot_nt(qd[:, sl], st.astype(MXU_DTYPE))
                st_ref[h] = st * dec[:, sl] + _dot(v_h.T.astype(MXU_DTYPE), ke[:, sl])
                o = o * lax.rsqrt(jnp.mean(o * o, axis=-1, keepdims=True) + EPS) * gnw
                sg = proj_ref[rows, 3 * kd_total + h * dk:3 * kd_total + (h + 1) * dk]
                y_ref[rows, sl] = (o * sg).astype(MXU_DTYPE)

    @pl.when(single_ref_ok)
    def _():
        chunk_body(True)

    @pl.when(jnp.logical_not(single_ref_ok))
    def _():
        chunk_body(False)

    o_ref[...] = x + _dot(y_ref[...], wout_ref[...])


def _hgrn_mixer(x2, batch, norm_w, w_in, lb_param, gn_w, w_out, lb_row):
    t, d = x2.shape
    dv = gn_w.shape[-1]
    heads = w_out.shape[0] // dv
    kd_total = (w_in.shape[1] - 2 * heads * dv) // 2
    dk = kd_total // heads
    assert dk == dv and kd_total == heads * dv, "kernel assumes key dim == value dim"
    tb = HGRN_BLOCK
    spb = t // batch // tb
    assert spb * tb * batch == t and tb % HGRN_CHUNK == 0
    row = pl.BlockSpec((tb, d), lambda b, s: (b * spb + s, 0))
    return pl.pallas_call(
        functools.partial(_hgrn_kernel, heads=heads, dk=dk, lb_row=lb_row),
        grid=(batch, spb),
        in_specs=[row, _const_spec((1, d)), _const_spec(w_in.shape), _const_spec(lb_param.shape),
                  _const_spec((1, dv)), _const_spec(w_out.shape)],
        out_specs=row,
        out_shape=jax.ShapeDtypeStruct((t, d), jnp.float32),
        scratch_shapes=[
            pltpu.VMEM((heads, dv, dk), jnp.float32),
            pltpu.VMEM((tb, w_in.shape[1]), jnp.float32),
            pltpu.VMEM((tb, kd_total), jnp.float32),
            pltpu.VMEM((heads, HGRN_CHUNK, HGRN_CHUNK), MXU_DTYPE),
            pltpu.VMEM((tb, heads * dv), MXU_DTYPE),
        ],
        compiler_params=pltpu.CompilerParams(dimension_semantics=("arbitrary", "arbitrary"),
                                             vmem_limit_bytes=VMEM_LIMIT_BYTES),
        name="hgrn2_mixer",
    )(x2, norm_w.reshape(1, d), w_in.astype(MXU_DTYPE), lb_param, gn_w.reshape(1, dv),
      w_out.astype(MXU_DTYPE))


def _gelu_tanh(x):
    c0 = math.sqrt(2.0 / math.pi)
    return 0.5 * x * (1.0 + jnp.tanh(c0 * (x + 0.044715 * (x * x * x))))


def _lru_kernel(x_ref, nw_ref, win_ref, cw_ref, cb_ref, wa_ref, ba_ref, wx_ref, bx_ref, lam_ref,
                wout_ref, o_ref, h_ref, ext_ref, *, width, n_blocks, taps):
    tb = x_ref.shape[0]
    bw = width // n_blocks
    pad = 8

    @pl.when(pl.program_id(1) == 0)
    def _():
        h_ref[...] = jnp.zeros_like(h_ref)
        ext_ref[:pad, :] = jnp.zeros((pad, width), jnp.float32)

    x = x_ref[...]
    hn = _rmsnorm(x, nw_ref[...]).astype(MXU_DTYPE)
    proj = _dot(hn, win_ref[...])
    y = _gelu_tanh(proj[:, :width])
    u = proj[:, width:]

    ext_ref[pad:, :] = u
    cw = cw_ref[...]
    uc = cb_ref[...] + u * cw[taps - 1:taps, :]
    for j in range(1, taps):
        uc = uc + ext_ref[pad - j:pad - j + tb, :] * cw[taps - 1 - j:taps - j, :]
    ext_ref[:pad, :] = u[tb - pad:, :]

    ra, rx = [], []
    for n in range(n_blocks):
        ub = uc[:, n * bw:(n + 1) * bw].astype(MXU_DTYPE)
        ra.append(_dot(ub, wa_ref[n]))
        rx.append(_dot(ub, wx_ref[n]))
    r = _sigmoid(jnp.concatenate(ra, axis=-1) + ba_ref[...])
    ig = _sigmoid(jnp.concatenate(rx, axis=-1) + bx_ref[...])

    nl = -lam_ref[...]
    softplus = jnp.maximum(nl, 0.0) + jnp.log1p(jnp.exp(-jnp.abs(nl)))
    log_a = (-LRU_C) * r * softplus
    a = jnp.exp(log_a)
    b_in = jnp.sqrt(-jnp.tanh(log_a) * (a * a + 1.0)) * (ig * uc)

    rows = lax.broadcasted_iota(jnp.int32, (tb, width), 0)
    aa, bb = a, b_in
    d = 1
    while d < tb:
        a_sh = pltpu.roll(aa, d, axis=0)
        b_sh = pltpu.roll(bb, d, axis=0)
        live = rows >= d
        bb = jnp.where(live, aa * b_sh + bb, bb)
        aa = jnp.where(live, aa * a_sh, aa)
        d *= 2
    hs = aa * h_ref[...] + bb
    h_ref[...] = hs[tb - 1:tb, :]

    o_ref[...] = x + _dot((hs * y).astype(MXU_DTYPE), wout_ref[...])


def _lru_mixer(x2, batch, norm_w, w_in, conv_w, conv_b, wa, ba, wx, bx, lam, w_out):
    t, d = x2.shape
    width = w_out.shape[0]
    n_blocks = wa.shape[0]
    taps = conv_w.shape[0]
    tb = LRU_BLOCK
    spb = t // batch // tb
    assert spb * tb * batch == t and taps - 1 <= 8
    row = pl.BlockSpec((tb, d), lambda b, s: (b * spb + s, 0))
    vec = lambda v: v.reshape(1, width)
    return pl.pallas_call(
        functools.partial(_lru_kernel, width=width, n_blocks=n_blocks, taps=taps),
        grid=(batch, spb),
        in_specs=[row, _const_spec((1, d)), _const_spec(w_in.shape), _const_spec(conv_w.shape),
                  _const_spec((1, width)), _const_spec(wa.shape), _const_spec((1, width)),
                  _const_spec(wx.shape), _const_spec((1, width)), _const_spec((1, width)),
                  _const_spec(w_out.shape)],
        out_specs=row,
        out_shape=jax.ShapeDtypeStruct((t, d), jnp.float32),
        scratch_shapes=[
            pltpu.VMEM((1, width), jnp.float32),
            pltpu.VMEM((tb + 8, width), jnp.float32),
        ],
        compiler_params=pltpu.CompilerParams(dimension_semantics=("arbitrary", "arbitrary"),
                                             vmem_limit_bytes=VMEM_LIMIT_BYTES),
        name="rglru_mixer",
    )(x2, norm_w.reshape(1, d), w_in.astype(MXU_DTYPE), conv_w, vec(conv_b), wa.astype(MXU_DTYPE),
      vec(ba), wx.astype(MXU_DTYPE), vec(bx), vec(lam), w_out.astype(MXU_DTYPE))


def kernel(x, norm_mix, norm_ffn, norm_final, hgrn_w_in, hgrn_lb, hgrn_norm, hgrn_w_out, lru_w_in, lru_conv_w, lru_conv_b, lru_wa, lru_ba, lru_wx, lru_bx, lru_lambda, lru_w_out, ffn_w_in, ffn_w_out):
    batch, seq, d = x.shape
    depth = norm_mix.shape[0]
    n_mixers = 2
    h = x.reshape(batch * seq, d)
    for l in range(depth):
        j = l // n_mixers
        if l % n_mixers == 0:
            h = _hgrn_mixer(h, batch, norm_mix[l], hgrn_w_in[j], hgrn_lb, hgrn_norm[j],
                            hgrn_w_out[j], lb_row=j)
        else:
            h = _lru_mixer(h, batch, norm_mix[l], lru_w_in[j], lru_conv_w[j], lru_conv_b[j],
                           lru_wa[j], lru_ba[j], lru_wx[j], lru_bx[j], lru_lambda[j], lru_w_out[j])
        h = _ffn(h, norm_ffn[l], ffn_w_in[l], ffn_w_out[l], norm_final, final_norm=(l == depth - 1))
    return h.reshape(batch, seq, d)
```

```python
import functools
import math

import jax
import jax.numpy as jnp
from jax import lax
from jax.experimental import pallas as pl
from jax.experimental.pallas import tpu as pltpu

EPS = 1e-6
LRU_C = 8.0
MXU_DTYPE = jnp.bfloat16
SUBLANES = 8
LANES = 128

HGRN_CHUNK = 128
HGRN_SAFE_EXP = 80.0
HGRN_LOG_FLOOR = -1e4
HGRN_BLOCK = 256
LRU_BLOCK = 256
FFN_BLOCK = 512
FFN_TILE = 256
VMEM_LIMIT_BYTES = 56 * 1024 * 1024


def _dot(a, b):
    return jnp.dot(a, b, preferred_element_type=jnp.float32)


def _dot_nt(a, b):
    return lax.dot_general(a, b, (((1,), (1,)), ((), ())), preferred_element_type=jnp.float32)


def _rmsnorm(x, w):
    return x * lax.rsqrt(jnp.mean(x * x, axis=-1, keepdims=True) + EPS) * w


def _sigmoid(x):
    return 1.0 / (1.0 + jnp.exp(-x))


def _split(x, terms):
    parts = []
    for i in range(terms):
        p = x.astype(MXU_DTYPE)
        parts.append(p)
        if i + 1 < terms:
            x = x - p.astype(jnp.float32)
    return parts


def _select_sum(sel, parts):
    out = _dot(sel, parts[0])
    for p in parts[1:]:
        out = out + _dot(sel, p)
    return out


def _const_spec(shape):
    return pl.BlockSpec(shape, lambda *_: (0,) * len(shape), pipeline_mode=pl.Buffered(1))


def _ffn_kernel(x_ref, nw_ref, wi_ref, wo_ref, fnw_ref, o_ref, *, d_ff, final_norm):
    x = x_ref[...]
    hn = _rmsnorm(x, nw_ref[...]).astype(MXU_DTYPE)
    acc = x
    for j in range(d_ff // FFN_TILE):
        lo, hi = j * FFN_TILE, (j + 1) * FFN_TILE
        g = _dot(hn, wi_ref[:, lo:hi])
        u = _dot(hn, wi_ref[:, d_ff + lo:d_ff + hi])
        a = (g * _sigmoid(g) * u).astype(MXU_DTYPE)
        acc = acc + _dot(a, wo_ref[lo:hi, :])
    if final_norm:
        acc = _rmsnorm(acc, fnw_ref[...])
    o_ref[...] = acc


def _ffn(x2, norm_w, w_in, w_out, final_norm_w, final_norm):
    t, d = x2.shape
    d_ff = w_out.shape[0]
    assert d_ff % FFN_TILE == 0 and t % FFN_BLOCK == 0
    row = pl.BlockSpec((FFN_BLOCK, d), lambda i: (i, 0))
    return pl.pallas_call(
        functools.partial(_ffn_kernel, d_ff=d_ff, final_norm=final_norm),
        grid=(t // FFN_BLOCK,),
        in_specs=[row, _const_spec((1, d)), _const_spec(w_in.shape), _const_spec(w_out.shape),
                  _const_spec((1, d))],
        out_specs=row,
        out_shape=jax.ShapeDtypeStruct((t, d), jnp.float32),
        compiler_params=pltpu.CompilerParams(dimension_semantics=("arbitrary",),
                                             vmem_limit_bytes=VMEM_LIMIT_BYTES),
        name="swiglu_ffn",
    )(x2, norm_w.reshape(1, d), w_in.astype(MXU_DTYPE), w_out.astype(MXU_DTYPE),
      final_norm_w.reshape(1, d))


def _hgrn_scores_single_ref(qd, kd, heads, dk, sc_ref):
    c = qd.shape[0]
    row = lax.broadcasted_iota(jnp.int32, (c, c), 0)
    col = lax.broadcasted_iota(jnp.int32, (c, c), 1)
    causal = col <= row
    for h in range(heads):
        sl = slice(h * dk, (h + 1) * dk)
        s = _dot_nt(qd[:, sl], kd[:, sl])
        sc_ref[h] = jnp.where(causal, s, 0.0).astype(MXU_DTYPE)


def _hgrn_scores_halving(q_c, k_c, lf_parts, heads, dk, sc_ref):
    c = q_c.shape[0]
    row = lax.broadcasted_iota(jnp.int32, (c, c), 0)
    col = lax.broadcasted_iota(jnp.int32, (c, c), 1)
    acc = []
    for h in range(heads):
        sl = slice(h * dk, (h + 1) * dk)
        s = _dot_nt(q_c[:, sl].astype(MXU_DTYPE), k_c[:, sl].astype(MXU_DTYPE))
        acc.append(jnp.where(row == col, s, 0.0))
    m = c // 2
    while m >= 1:
        blk = 2 * m
        ref_of_row = (row // blk) * blk + (m - 1)
        ref_of_col = (col // blk) * blk + (m - 1)
        row_upper = (row % blk) >= m
        sel_q = (row_upper & (col > ref_of_row) & (col <= row)).astype(MXU_DTYPE)
        sel_k = ((~row_upper) & (col > row) & (col <= ref_of_row)).astype(MXU_DTYPE)
        qd = (q_c * jnp.exp(_select_sum(sel_q, lf_parts))).astype(MXU_DTYPE)
        kd = (k_c * jnp.exp(_select_sum(sel_k, lf_parts))).astype(MXU_DTYPE)
        pair = (ref_of_row == ref_of_col) & row_upper & ((col % blk) < m)
        for h in range(heads):
            sl = slice(h * dk, (h + 1) * dk)
            acc[h] = acc[h] + jnp.where(pair, _dot_nt(qd[:, sl], kd[:, sl]), 0.0)
        m //= 2
    for h in range(heads):
        sc_ref[h] = acc[h].astype(MXU_DTYPE)


def _hgrn_kernel(x_ref, nw_ref, win_ref, lbp_ref, gnw_ref, wout_ref, o_ref, worst_ref,
                 st_ref, proj_ref, k_ref, sc_ref, y_ref, *, heads, dk, lb_row, single_ref):
    kd_total = heads * dk
    tb = x_ref.shape[0]
    c = HGRN_CHUNK
    n_chunks = tb // c
    first = (pl.program_id(0) == 0) & (pl.program_id(1) == 0)

    @pl.when(pl.program_id(1) == 0)
    def _():
        st_ref[...] = jnp.zeros_like(st_ref)

    @pl.when(first)
    def _():
        worst_ref[...] = jnp.zeros_like(worst_ref)

    x = x_ref[...]
    hn = _rmsnorm(x, nw_ref[...]).astype(MXU_DTYPE)
    proj = _dot(hn, win_ref[...])

    lp = lbp_ref[...]
    e = jnp.exp(lp - jnp.max(lp, axis=0, keepdims=True))
    lb = jnp.sum(e[: lb_row + 1], axis=0, keepdims=True) / jnp.sum(e, axis=0, keepdims=True)
    one_m_lb = 1.0 - lb

    q = proj[:, :kd_total]
    fl = proj[:, kd_total:2 * kd_total]
    g = proj[:, 3 * kd_total:]
    en = jnp.exp(-jnp.abs(fl))
    rn = 1.0 / (1.0 + en)
    small = en * rn
    pos = fl >= 0.0
    kk = one_m_lb * jnp.where(pos, small, rn)
    f = lb + one_m_lb * jnp.where(pos, rn, small)
    log_f = jnp.maximum(jnp.log(f), HGRN_LOG_FLOOR)

    proj_ref[:, :kd_total] = q * _sigmoid(q)
    proj_ref[:, kd_total:2 * kd_total] = log_f
    proj_ref[:, 2 * kd_total:3 * kd_total] = proj[:, 2 * kd_total:3 * kd_total]
    proj_ref[:, 3 * kd_total:] = g * _sigmoid(g)
    k_ref[...] = kk

    row = lax.broadcasted_iota(jnp.int32, (c, c), 0)
    col = lax.broadcasted_iota(jnp.int32, (c, c), 1)
    tril = (col <= row).astype(MXU_DTYPE)
    gnw = gnw_ref[...]

    for ci in range(n_chunks):
        rows = slice(ci * c, (ci + 1) * c)
        q_c = proj_ref[rows, :kd_total]
        lf_parts = _split(proj_ref[rows, kd_total:2 * kd_total], 2 if single_ref else 3)
        k_c = k_ref[rows, :]
        b = _select_sum(tril, lf_parts)
        b_last = b[c - 1:c, :]
        dec = jnp.exp(b_last)
        qd = (q_c * jnp.exp(b)).astype(MXU_DTYPE)
        if single_ref:
            kd_f = k_c * jnp.exp(-b)
            _hgrn_scores_single_ref(qd, kd_f.astype(MXU_DTYPE), heads, dk, sc_ref)
            ke = (kd_f * dec).astype(MXU_DTYPE)
            folded = b_last
            width = kd_total
            while width > worst_ref.shape[1]:
                width //= 2
                folded = jnp.minimum(folded[:, :width], folded[:, width:])
            worst_ref[...] = jnp.minimum(worst_ref[...], folded)
        else:
            _hgrn_scores_halving(q_c, k_c, lf_parts, heads, dk, sc_ref)
            ke = (k_c * jnp.exp(b_last - b)).astype(MXU_DTYPE)
        for h in range(heads):
            sl = slice(h * dk, (h + 1) * dk)
            v_h = proj_ref[rows, 2 * kd_total + h * dk:2 * kd_total + (h + 1) * dk]
            st = st_ref[h]
            o = _dot(sc_ref[h], v_h.astype(MXU_DTYPE)) + _dot_nt(qd[:, sl], st.astype(MXU_DTYPE))
            st_ref[h] = st * dec[:, sl] + _dot(v_h.T.astype(MXU_DTYPE), ke[:, sl])
            o = o * lax.rsqrt(jnp.mean(o * o, axis=-1, keepdims=True) + EPS) * gnw
            sg = proj_ref[rows, 3 * kd_total + h * dk:3 * kd_total + (h + 1) * dk]
            y_ref[rows, sl] = (o * sg).astype(MXU_DTYPE)

    o_ref[...] = x + _dot(y_ref[...], wout_ref[...])


def _hgrn_mixer(x2, batch, norm_w, w_in, lb_param, gn_w, w_out, lb_row, single_ref):
    t, d = x2.shape
    dv = gn_w.shape[-1]
    heads = w_out.shape[0] // dv
    kd_total = (w_in.shape[1] - 2 * heads * dv) // 2
    dk = kd_total // heads
    assert dk == dv and kd_total == heads * dv, "kernel assumes key dim == value dim"
    tb = HGRN_BLOCK
    spb = t // batch // tb
    assert spb * tb * batch == t and tb % HGRN_CHUNK == 0
    row = pl.BlockSpec((tb, d), lambda b, s: (b * spb + s, 0))
    return pl.pallas_call(
        functools.partial(_hgrn_kernel, heads=heads, dk=dk, lb_row=lb_row, single_ref=single_ref),
        grid=(batch, spb),
        in_specs=[row, _const_spec((1, d)), _const_spec(w_in.shape), _const_spec(lb_param.shape),
                  _const_spec((1, dv)), _const_spec(w_out.shape)],
        out_specs=[row, pl.BlockSpec((1, dk), lambda b, s: (0, 0))],
        out_shape=[jax.ShapeDtypeStruct((t, d), jnp.float32),
                   jax.ShapeDtypeStruct((1, dk), jnp.float32)],
        scratch_shapes=[
            pltpu.VMEM((heads, dv, dk), jnp.float32),
            pltpu.VMEM((tb, w_in.shape[1]), jnp.float32),
            pltpu.VMEM((tb, kd_total), jnp.float32),
            pltpu.VMEM((heads, HGRN_CHUNK, HGRN_CHUNK), MXU_DTYPE),
            pltpu.VMEM((tb, heads * dv), MXU_DTYPE),
        ],
        compiler_params=pltpu.CompilerParams(dimension_semantics=("arbitrary", "arbitrary"),
                                             vmem_limit_bytes=VMEM_LIMIT_BYTES),
        name="hgrn2_mixer" if single_ref else "hgrn2_mixer_any_decay",
    )(x2, norm_w.reshape(1, d), w_in.astype(MXU_DTYPE), lb_param, gn_w.reshape(1, dv),
      w_out.astype(MXU_DTYPE))


def _gelu_tanh(x):
    c0 = math.sqrt(2.0 / math.pi)
    return 0.5 * x * (1.0 + jnp.tanh(c0 * (x + 0.044715 * (x * x * x))))


def _scan_rows(a, b, n):
    rows = lax.broadcasted_iota(jnp.int32, a.shape, 0)
    d = 1
    while d < n:
        live = rows >= d
        a_sh = jnp.where(live, pltpu.roll(a, d, axis=0), 1.0)
        b_sh = jnp.where(live, pltpu.roll(b, d, axis=0), 0.0)
        b = a * b_sh + b
        a = a * a_sh
        d *= 2
    return a, b


def _lru_kernel(x_ref, nw_ref, win_ref, cw_ref, cb_ref, wa_ref, ba_ref, wx_ref, bx_ref, lam_ref,
                wout_ref, o_ref, h_ref, ext_ref, a_ref, b_ref, c_ref, hs_ref, *, width, n_blocks,
                taps):
    tb = x_ref.shape[0]
    bw = width // n_blocks
    pad = SUBLANES
    n_groups = tb // SUBLANES

    @pl.when(pl.program_id(1) == 0)
    def _():
        h_ref[...] = jnp.zeros_like(h_ref)
        ext_ref[:pad, :] = jnp.zeros((pad, width), jnp.float32)

    x = x_ref[...]
    hn = _rmsnorm(x, nw_ref[...]).astype(MXU_DTYPE)
    proj = _dot(hn, win_ref[...])
    y = _gelu_tanh(proj[:, :width])
    u = proj[:, width:]

    ext_ref[pad:, :] = u
    cw = cw_ref[...]
    uc = cb_ref[...] + u * cw[taps - 1:taps, :]
    for j in range(1, taps):
        uc = uc + ext_ref[pad - j:pad - j + tb, :] * cw[taps - 1 - j:taps - j, :]
    ext_ref[:pad, :] = u[tb - pad:, :]

    ra, rx = [], []
    for n in range(n_blocks):
        ub = uc[:, n * bw:(n + 1) * bw].astype(MXU_DTYPE)
        ra.append(_dot(ub, wa_ref[n]))
        rx.append(_dot(ub, wx_ref[n]))
    r = _sigmoid(jnp.concatenate(ra, axis=-1) + ba_ref[...])
    ig = _sigmoid(jnp.concatenate(rx, axis=-1) + bx_ref[...])

    nl = -lam_ref[...]
    softplus = jnp.maximum(nl, 0.0) + jnp.log1p(jnp.exp(-jnp.abs(nl)))
    log_a = (-LRU_C) * r * softplus
    a = jnp.exp(log_a)
    b_in = jnp.sqrt(-jnp.tanh(log_a) * (a * a + 1.0)) * (ig * uc)

    a3, b3 = _scan_groups(a.reshape(n_groups, SUBLANES, width), b_in.reshape(n_groups, SUBLANES, width))
    a2 = a3.reshape(tb, width)
    b2 = b3.reshape(tb, width)
    last = pl.ds(SUBLANES - 1, n_groups, stride=SUBLANES)
    for j in range(width // LANES):
        cols = slice(j * LANES, (j + 1) * LANES)
        a_ref[j] = a2[:, cols]
        b_ref[j] = b2[:, cols]
        h_prev = h_ref[:, cols]
        a_inc, b_inc = _scan_rows(a_ref[j, last, :], b_ref[j, last, :], n_groups)
        h_end = a_inc * h_prev + b_inc
        c_ref[j, SUBLANES - 1:SUBLANES, :] = h_prev
        c_ref[j, SUBLANES:, :] = h_end
        h_ref[:, cols] = h_end[n_groups - 1:, :]
        for gi in range(n_groups):
            rows = slice(gi * SUBLANES, (gi + 1) * SUBLANES)
            hs_ref[rows, cols] = (a_ref[j, rows, :] * c_ref[j, SUBLANES - 1 + gi:SUBLANES + gi, :]
                                  + b_ref[j, rows, :])
    hy = (hs_ref[...] * y).astype(MXU_DTYPE)

    o_ref[...] = x + _dot(hy, wout_ref[...])


def _scan_groups(a, b):
    sub = lax.broadcasted_iota(jnp.int32, a.shape, 1)
    d = 1
    while d < SUBLANES:
        live = sub >= d
        a_sh = jnp.where(live, pltpu.roll(a, d, axis=1), 1.0)
        b_sh = jnp.where(live, pltpu.roll(b, d, axis=1), 0.0)
        b = a * b_sh + b
        a = a * a_sh
        d *= 2
    return a, b


def _lru_mixer(x2, batch, norm_w, w_in, conv_w, conv_b, wa, ba, wx, bx, lam, w_out):
    t, d = x2.shape
    width = w_out.shape[0]
    n_blocks = wa.shape[0]
    taps = conv_w.shape[0]
    tb = LRU_BLOCK
    spb = t // batch // tb
    assert spb * tb * batch == t and taps - 1 <= SUBLANES and tb % SUBLANES == 0
    row = pl.BlockSpec((tb, d), lambda b, s: (b * spb + s, 0))
    vec = lambda v: v.reshape(1, width)
    return pl.pallas_call(
        functools.partial(_lru_kernel, width=width, n_blocks=n_blocks, taps=taps),
        grid=(batch, spb),
        in_specs=[row, _const_spec((1, d)), _const_spec(w_in.shape), _const_spec(conv_w.shape),
                  _const_spec((1, width)), _const_spec(wa.shape), _const_spec((1, width)),
                  _const_spec(wx.shape), _const_spec((1, width)), _const_spec((1, width)),
                  _const_spec(w_out.shape)],
        out_specs=row,
        out_shape=jax.ShapeDtypeStruct((t, d), jnp.float32),
        scratch_shapes=[
            pltpu.VMEM((1, width), jnp.float32),
            pltpu.VMEM((tb + SUBLANES, width), jnp.float32),
            pltpu.VMEM((width // LANES, tb, LANES), jnp.float32),
            pltpu.VMEM((width // LANES, tb, LANES), jnp.float32),
            pltpu.VMEM((width // LANES, tb // SUBLANES + SUBLANES, LANES), jnp.float32),
            pltpu.VMEM((tb, width), jnp.float32),
        ],
        compiler_params=pltpu.CompilerParams(dimension_semantics=("arbitrary", "arbitrary"),
                                             vmem_limit_bytes=VMEM_LIMIT_BYTES),
        name="rglru_mixer",
    )(x2, norm_w.reshape(1, d), w_in.astype(MXU_DTYPE), conv_w, vec(conv_b), wa.astype(MXU_DTYPE),
      vec(ba), wx.astype(MXU_DTYPE), vec(bx), vec(lam), w_out.astype(MXU_DTYPE))


def kernel(x, norm_mix, norm_ffn, norm_final, hgrn_w_in, hgrn_lb, hgrn_norm, hgrn_w_out, lru_w_in, lru_conv_w, lru_conv_b, lru_wa, lru_ba, lru_wx, lru_bx, lru_lambda, lru_w_out, ffn_w_in, ffn_w_out):
    batch, seq, d = x.shape
    depth = norm_mix.shape[0]
    n_mixers = 2

    def ffn(l, h):
        return _ffn(h, norm_ffn[l], ffn_w_in[l], ffn_w_out[l], norm_final, final_norm=(l == depth - 1))

    def layers_from(l, h):
        if l == depth:
            return h
        j = l // n_mixers
        if l % n_mixers == 1:
            h = _lru_mixer(h, batch, norm_mix[l], lru_w_in[j], lru_conv_w[j], lru_conv_b[j],
                           lru_wa[j], lru_ba[j], lru_wx[j], lru_bx[j], lru_lambda[j], lru_w_out[j])
            return layers_from(l + 1, ffn(l, h))
        hgrn = functools.partial(_hgrn_mixer, h, batch, norm_mix[l], hgrn_w_in[j], hgrn_lb,
                                 hgrn_norm[j], hgrn_w_out[j], j)
        mixed, worst = hgrn(single_ref=True)
        return lax.cond(jnp.min(worst) >= -HGRN_SAFE_EXP,
                        lambda: layers_from(l + 1, ffn(l, mixed)),
                        lambda: layers_from(l + 1, ffn(l, hgrn(single_ref=False)[0])))

    return layers_from(0, x.reshape(batch * seq, d)).reshape(batch, seq, d)
```

```python
import functools
import math

import jax
import jax.numpy as jnp
from jax import lax
from jax.experimental import pallas as pl
from jax.experimental.pallas import tpu as pltpu

EPS = 1e-6
LRU_C = 8.0
LOG2E = math.log2(math.e)
MXU_DTYPE = jnp.bfloat16
SUBLANES = 8
LANES = 128

HGRN_CHUNK = 128
HGRN_SAFE_EXP = 80.0
HGRN_LOG_FLOOR = -1e4
HGRN_BLOCK = 512
LRU_BLOCK = 512
FFN_BLOCK = 512
FFN_TILE = 256
VMEM_LIMIT_BYTES = 56 * 1024 * 1024


def _dot(a, b):
    return jnp.dot(a, b, preferred_element_type=jnp.float32)


def _dot_nt(a, b):
    return lax.dot_general(a, b, (((1,), (1,)), ((), ())), preferred_element_type=jnp.float32)


def _rmsnorm(x, w):
    return x * lax.rsqrt(jnp.mean(x * x, axis=-1, keepdims=True) + EPS) * w


def _sigmoid(x):
    return 1.0 / (1.0 + jnp.exp2(x * (-LOG2E)))


def _split(x, terms):
    parts = []
    for i in range(terms):
        p = x.astype(MXU_DTYPE)
        parts.append(p)
        if i + 1 < terms:
            x = x - p.astype(jnp.float32)
    return parts


def _select_sum(sel, parts):
    out = _dot(sel, parts[0])
    for p in parts[1:]:
        out = out + _dot(sel, p)
    return out


def _const_spec(shape):
    return pl.BlockSpec(shape, lambda *_: (0,) * len(shape), pipeline_mode=pl.Buffered(1))


def _ffn_kernel(x_ref, nw_ref, wi_ref, wo_ref, fnw_ref, o_ref, *, d_ff, final_norm):
    x = x_ref[...]
    hn = _rmsnorm(x, nw_ref[...]).astype(MXU_DTYPE)
    acc = x
    for j in range(d_ff // FFN_TILE):
        lo, hi = j * FFN_TILE, (j + 1) * FFN_TILE
        g = _dot(hn, wi_ref[:, lo:hi])
        u = _dot(hn, wi_ref[:, d_ff + lo:d_ff + hi])
        a = (g * _sigmoid(g) * u).astype(MXU_DTYPE)
        acc = acc + _dot(a, wo_ref[lo:hi, :])
    if final_norm:
        acc = _rmsnorm(acc, fnw_ref[...])
    o_ref[...] = acc


def _ffn(x2, norm_w, w_in, w_out, final_norm_w, final_norm):
    t, d = x2.shape
    d_ff = w_out.shape[0]
    assert d_ff % FFN_TILE == 0 and t % FFN_BLOCK == 0
    row = pl.BlockSpec((FFN_BLOCK, d), lambda i: (i, 0))
    return pl.pallas_call(
        functools.partial(_ffn_kernel, d_ff=d_ff, final_norm=final_norm),
        grid=(t // FFN_BLOCK,),
        in_specs=[row, _const_spec((1, d)), _const_spec(w_in.shape), _const_spec(w_out.shape),
                  _const_spec((1, d))],
        out_specs=row,
        out_shape=jax.ShapeDtypeStruct((t, d), jnp.float32),
        compiler_params=pltpu.CompilerParams(dimension_semantics=("arbitrary",),
                                             vmem_limit_bytes=VMEM_LIMIT_BYTES),
        name="swiglu_ffn",
    )(x2, norm_w.reshape(1, d), w_in.astype(MXU_DTYPE), w_out.astype(MXU_DTYPE),
      final_norm_w.reshape(1, d))


def _hgrn_scores_single_ref(qd, kd, heads, dk, sc_ref):
    c = qd.shape[0]
    row = lax.broadcasted_iota(jnp.int32, (c, c), 0)
    col = lax.broadcasted_iota(jnp.int32, (c, c), 1)
    causal = col <= row
    for h in range(heads):
        sl = slice(h * dk, (h + 1) * dk)
        s = _dot_nt(qd[:, sl], kd[:, sl])
        sc_ref[h] = jnp.where(causal, s, 0.0).astype(MXU_DTYPE)


def _hgrn_scores_halving(q_c, k_c, lf_parts, heads, dk, sc_ref):
    c = q_c.shape[0]
    row = lax.broadcasted_iota(jnp.int32, (c, c), 0)
    col = lax.broadcasted_iota(jnp.int32, (c, c), 1)
    acc = []
    for h in range(heads):
        sl = slice(h * dk, (h + 1) * dk)
        s = _dot_nt(q_c[:, sl].astype(MXU_DTYPE), k_c[:, sl].astype(MXU_DTYPE))
        acc.append(jnp.where(row == col, s, 0.0))
    m = c // 2
    while m >= 1:
        blk = 2 * m
        ref_of_row = (row // blk) * blk + (m - 1)
        ref_of_col = (col // blk) * blk + (m - 1)
        row_upper = (row % blk) >= m
        sel_q = (row_upper & (col > ref_of_row) & (col <= row)).astype(MXU_DTYPE)
        sel_k = ((~row_upper) & (col > row) & (col <= ref_of_row)).astype(MXU_DTYPE)
        qd = (q_c * jnp.exp(_select_sum(sel_q, lf_parts))).astype(MXU_DTYPE)
        kd = (k_c * jnp.exp(_select_sum(sel_k, lf_parts))).astype(MXU_DTYPE)
        pair = (ref_of_row == ref_of_col) & row_upper & ((col % blk) < m)
        for h in range(heads):
            sl = slice(h * dk, (h + 1) * dk)
            acc[h] = acc[h] + jnp.where(pair, _dot_nt(qd[:, sl], kd[:, sl]), 0.0)
        m //= 2
    for h in range(heads):
        sc_ref[h] = acc[h].astype(MXU_DTYPE)


def _hgrn_kernel(x_ref, nw_ref, win_ref, lbp_ref, gnw_ref, wout_ref, o_ref, worst_ref,
                 st_ref, proj_ref, k_ref, sc_ref, y_ref, *, heads, dk, lb_row, single_ref):
    kd_total = heads * dk
    tb = x_ref.shape[0]
    c = HGRN_CHUNK
    n_chunks = tb // c
    first = (pl.program_id(0) == 0) & (pl.program_id(1) == 0)

    @pl.when(pl.program_id(1) == 0)
    def _():
        st_ref[...] = jnp.zeros_like(st_ref)

    @pl.when(first)
    def _():
        worst_ref[...] = jnp.zeros_like(worst_ref)

    x = x_ref[...]
    hn = _rmsnorm(x, nw_ref[...]).astype(MXU_DTYPE)
    proj = _dot(hn, win_ref[...])

    lp = lbp_ref[...]
    e = jnp.exp(lp - jnp.max(lp, axis=0, keepdims=True))
    lb = jnp.sum(e[: lb_row + 1], axis=0, keepdims=True) / jnp.sum(e, axis=0, keepdims=True)
    one_m_lb = 1.0 - lb

    q = proj[:, :kd_total]
    fl = proj[:, kd_total:2 * kd_total]
    g = proj[:, 3 * kd_total:]
    en = jnp.exp(-jnp.abs(fl))
    rn = 1.0 / (1.0 + en)
    small = en * rn
    pos = fl >= 0.0
    kk = one_m_lb * jnp.where(pos, small, rn)
    f = lb + one_m_lb * jnp.where(pos, rn, small)
    log_f = jnp.maximum(jnp.log(f), HGRN_LOG_FLOOR)

    proj_ref[:, :kd_total] = q * _sigmoid(q)
    proj_ref[:, kd_total:2 * kd_total] = log_f
    proj_ref[:, 2 * kd_total:3 * kd_total] = proj[:, 2 * kd_total:3 * kd_total]
    proj_ref[:, 3 * kd_total:] = g * _sigmoid(g)
    k_ref[...] = kk

    row = lax.broadcasted_iota(jnp.int32, (c, c), 0)
    col = lax.broadcasted_iota(jnp.int32, (c, c), 1)
    tril = (col <= row).astype(MXU_DTYPE)
    gnw = gnw_ref[...]

    for ci in range(n_chunks):
        rows = slice(ci * c, (ci + 1) * c)
        q_c = proj_ref[rows, :kd_total]
        lf_parts = _split(proj_ref[rows, kd_total:2 * kd_total], 2 if single_ref else 3)
        k_c = k_ref[rows, :]
        b = _select_sum(tril, lf_parts)
        b_last = b[c - 1:c, :]
        dec = jnp.exp(b_last)
        qd = (q_c * jnp.exp(b)).astype(MXU_DTYPE)
        if single_ref:
            kd_f = k_c * jnp.exp(-b)
            _hgrn_scores_single_ref(qd, kd_f.astype(MXU_DTYPE), heads, dk, sc_ref)
            ke = (kd_f * dec).astype(MXU_DTYPE)
            folded = b_last
            width = kd_total
            while width > worst_ref.shape[1]:
                width //= 2
                folded = jnp.minimum(folded[:, :width], folded[:, width:])
            worst_ref[...] = jnp.minimum(worst_ref[...], folded)
        else:
            _hgrn_scores_halving(q_c, k_c, lf_parts, heads, dk, sc_ref)
            ke = (k_c * jnp.exp(b_last - b)).astype(MXU_DTYPE)
        for h in range(heads):
            sl = slice(h * dk, (h + 1) * dk)
            v_h = proj_ref[rows, 2 * kd_total + h * dk:2 * kd_total + (h + 1) * dk]
            st = st_ref[h]
            o = _dot(sc_ref[h], v_h.astype(MXU_DTYPE)) + _dot_nt(qd[:, sl], st.astype(MXU_DTYPE))
            st_ref[h] = st * dec[:, sl] + _dot(v_h.T.astype(MXU_DTYPE), ke[:, sl])
            o = o * lax.rsqrt(jnp.mean(o * o, axis=-1, keepdims=True) + EPS) * gnw
            sg = proj_ref[rows, 3 * kd_total + h * dk:3 * kd_total + (h + 1) * dk]
            y_ref[rows, sl] = (o * sg).astype(MXU_DTYPE)

    o_ref[...] = x + _dot(y_ref[...], wout_ref[...])


def _hgrn_mixer(x2, batch, norm_w, w_in, lb_param, gn_w, w_out, lb_row, single_ref):
    t, d = x2.shape
    dv = gn_w.shape[-1]
    heads = w_out.shape[0] // dv
    kd_total = (w_in.shape[1] - 2 * heads * dv) // 2
    dk = kd_total // heads
    assert dk == dv and kd_total == heads * dv, "kernel assumes key dim == value dim"
    tb = HGRN_BLOCK
    spb = t // batch // tb
    assert spb * tb * batch == t and tb % HGRN_CHUNK == 0
    row = pl.BlockSpec((tb, d), lambda b, s: (b * spb + s, 0))
    return pl.pallas_call(
        functools.partial(_hgrn_kernel, heads=heads, dk=dk, lb_row=lb_row, single_ref=single_ref),
        grid=(batch, spb),
        in_specs=[row, _const_spec((1, d)), _const_spec(w_in.shape), _const_spec(lb_param.shape),
                  _const_spec((1, dv)), _const_spec(w_out.shape)],
        out_specs=[row, pl.BlockSpec((1, dk), lambda b, s: (0, 0))],
        out_shape=[jax.ShapeDtypeStruct((t, d), jnp.float32),
                   jax.ShapeDtypeStruct((1, dk), jnp.float32)],
        scratch_shapes=[
            pltpu.VMEM((heads, dv, dk), jnp.float32),
            pltpu.VMEM((tb, w_in.shape[1]), jnp.float32),
            pltpu.VMEM((tb, kd_total), jnp.float32),
            pltpu.VMEM((heads, HGRN_CHUNK, HGRN_CHUNK), MXU_DTYPE),
            pltpu.VMEM((tb, heads * dv), MXU_DTYPE),
        ],
        compiler_params=pltpu.CompilerParams(dimension_semantics=("arbitrary", "arbitrary"),
                                             vmem_limit_bytes=VMEM_LIMIT_BYTES),
        name="hgrn2_mixer" if single_ref else "hgrn2_mixer_any_decay",
    )(x2, norm_w.reshape(1, d), w_in.astype(MXU_DTYPE), lb_param, gn_w.reshape(1, dv),
      w_out.astype(MXU_DTYPE))


def _gelu_tanh(x):
    c = -2.0 * math.sqrt(2.0 / math.pi) * LOG2E
    return x * (1.0 / (1.0 + jnp.exp2(x * (c + (c * 0.044715) * (x * x)))))


def _scan_rows(a, b, n):
    rows = lax.broadcasted_iota(jnp.int32, a.shape, 0)
    d = 1
    while d < n:
        live = rows >= d
        a_sh = jnp.where(live, pltpu.roll(a, d, axis=0), 1.0)
        b_sh = jnp.where(live, pltpu.roll(b, d, axis=0), 0.0)
        b = a * b_sh + b
        a = a * a_sh
        d *= 2
    return a, b


def _lru_kernel(x_ref, nw_ref, win_ref, cw_ref, cb_ref, wa_ref, ba_ref, wx_ref, bx_ref, lam_ref,
                wout_ref, o_ref, h_ref, ext_ref, a_ref, b_ref, hs_ref, *, width, n_blocks, taps):
    tb = x_ref.shape[0]
    bw = width // n_blocks
    pad = SUBLANES
    n_groups = tb // SUBLANES

    @pl.when(pl.program_id(1) == 0)
    def _():
        h_ref[...] = jnp.zeros_like(h_ref)
        ext_ref[:pad, :] = jnp.zeros((pad, width), jnp.float32)

    x = x_ref[...]
    hn = _rmsnorm(x, nw_ref[...]).astype(MXU_DTYPE)
    proj = _dot(hn, win_ref[...])
    y = _gelu_tanh(proj[:, :width])
    u = proj[:, width:]

    ext_ref[pad:, :] = u
    cw = cw_ref[...]
    uc = cb_ref[...] + u * cw[taps - 1:taps, :]
    for j in range(1, taps):
        uc = uc + ext_ref[pad - j:pad - j + tb, :] * cw[taps - 1 - j:taps - j, :]
    ext_ref[:pad, :] = u[tb - pad:, :]

    ra, rx = [], []
    for n in range(n_blocks):
        ub = uc[:, n * bw:(n + 1) * bw].astype(MXU_DTYPE)
        ra.append(_dot(ub, wa_ref[n]))
        rx.append(_dot(ub, wx_ref[n]))
    r = _sigmoid(jnp.concatenate(ra, axis=-1) + ba_ref[...])
    ig = _sigmoid(jnp.concatenate(rx, axis=-1) + bx_ref[...])

    nl = -lam_ref[...]
    softplus = jnp.maximum(nl, 0.0) + jnp.log1p(jnp.exp(-jnp.abs(nl)))
    rate = (-LRU_C) * softplus
    log_a = rate * r
    a = jnp.exp2((rate * LOG2E) * r)
    b_in = jnp.sqrt(-jnp.tanh(log_a) * (a * a + 1.0)) * (ig * uc)

    n_tiles = width // LANES
    for j in range(n_tiles):
        cols = slice(j * LANES, (j + 1) * LANES)
        a_ref[j] = a[:, cols]
        b_ref[j] = b_in[:, cols]
    group_row = lax.broadcasted_iota(jnp.int32, (n_groups, LANES), 0)
    for j in range(n_tiles):
        cols = slice(j * LANES, (j + 1) * LANES)
        step = lambda r: pl.ds(r, n_groups, stride=SUBLANES)
        a_cum = [a_ref[j, step(0), :]]
        h_loc = [b_ref[j, step(0), :]]
        for r in range(1, SUBLANES):
            a_r = a_ref[j, step(r), :]
            h_loc.append(a_r * h_loc[-1] + b_ref[j, step(r), :])
            a_cum.append(a_r * a_cum[-1])
        a_inc, b_inc = _scan_rows(a_cum[-1], h_loc[-1], n_groups)
        h_prev = h_ref[:, cols]
        h_end = a_inc * h_prev + b_inc
        h_in = jnp.where(group_row >= 1, pltpu.roll(h_end, 1, axis=0), h_prev)
        h_ref[:, cols] = h_end[n_groups - 1:, :]
        for r in range(SUBLANES):
            hs_ref[j, step(r), :] = h_loc[r] + a_cum[r] * h_in
    hs = jnp.concatenate([hs_ref[j] for j in range(n_tiles)], axis=-1)
    hy = (hs * y).astype(MXU_DTYPE)

    o_ref[...] = x + _dot(hy, wout_ref[...])


def _lru_mixer(x2, batch, norm_w, w_in, conv_w, conv_b, wa, ba, wx, bx, lam, w_out):
    t, d = x2.shape
    width = w_out.shape[0]
    n_blocks = wa.shape[0]
    taps = conv_w.shape[0]
    tb = LRU_BLOCK
    spb = t // batch // tb
    assert spb * tb * batch == t and taps - 1 <= SUBLANES and tb % SUBLANES == 0
    row = pl.BlockSpec((tb, d), lambda b, s: (b * spb + s, 0))
    vec = lambda v: v.reshape(1, width)
    return pl.pallas_call(
        functools.partial(_lru_kernel, width=width, n_blocks=n_blocks, taps=taps),
        grid=(batch, spb),
        in_specs=[row, _const_spec((1, d)), _const_spec(w_in.shape), _const_spec(conv_w.shape),
                  _const_spec((1, width)), _const_spec(wa.shape), _const_spec((1, width)),
                  _const_spec(wx.shape), _const_spec((1, width)), _const_spec((1, width)),
                  _const_spec(w_out.shape)],
        out_specs=row,
        out_shape=jax.ShapeDtypeStruct((t, d), jnp.float32),
        scratch_shapes=[
            pltpu.VMEM((1, width), jnp.float32),
            pltpu.VMEM((tb + SUBLANES, width), jnp.float32),
            pltpu.VMEM((width // LANES, tb, LANES), jnp.float32),
            pltpu.VMEM((width // LANES, tb, LANES), jnp.float32),
            pltpu.VMEM((width // LANES, tb, LANES), jnp.float32),
        ],
        compiler_params=pltpu.CompilerParams(dimension_semantics=("arbitrary", "arbitrary"),
                                             vmem_limit_bytes=VMEM_LIMIT_BYTES),
        name="rglru_mixer",
    )(x2, norm_w.reshape(1, d), w_in.astype(MXU_DTYPE), conv_w, vec(conv_b), wa.astype(MXU_DTYPE),
      vec(ba), wx.astype(MXU_DTYPE), vec(bx), vec(lam), w_out.astype(MXU_DTYPE))


def kernel(x, norm_mix, norm_ffn, norm_final, hgrn_w_in, hgrn_lb, hgrn_norm, hgrn_w_out, lru_w_in, lru_conv_w, lru_conv_b, lru_wa, lru_ba, lru_wx, lru_bx, lru_lambda, lru_w_out, ffn_w_in, ffn_w_out):
    batch, seq, d = x.shape
    depth = norm_mix.shape[0]
    n_mixers = 2

    def ffn(l, h):
        return _ffn(h, norm_ffn[l], ffn_w_in[l], ffn_w_out[l], norm_final, final_norm=(l == depth - 1))

    def layers_from(l, h):
        if l == depth:
            return h
        j = l // n_mixers
        if l % n_mixers == 1:
            h = _lru_mixer(h, batch, norm_mix[l], lru_w_in[j], lru_conv_w[j], lru_conv_b[j],
                           lru_wa[j], lru_ba[j], lru_wx[j], lru_bx[j], lru_lambda[j], lru_w_out[j])
            return layers_from(l + 1, ffn(l, h))
        hgrn = functools.partial(_hgrn_mixer, h, batch, norm_mix[l], hgrn_w_in[j], hgrn_lb,
                                 hgrn_norm[j], hgrn_w_out[j], j)
        mixed, worst = hgrn(single_ref=True)
        return lax.cond(jnp.min(worst) >= -HGRN_SAFE_EXP,
                        lambda: layers_from(l + 1, ffn(l, mixed)),
                        lambda: layers_from(l + 1, ffn(l, hgrn(single_ref=False)[0])))

    return layers_from(0, x.reshape(batch * seq, d)).reshape(batch, seq, d)
```

```python
import functools
import math

import jax
import jax.numpy as jnp
from jax import lax
from jax.experimental import pallas as pl
from jax.experimental.pallas import tpu as pltpu

EPS = 1e-6
LRU_C = 8.0
LOG2E = math.log2(math.e)
MXU_DTYPE = jnp.bfloat16
SUBLANES = 8
LANES = 128

HGRN_CHUNK = 128
HGRN_SAFE_EXP = 80.0
HGRN_LOG_FLOOR = -1e4
HGRN_BLOCK = 512
LRU_BLOCK = 512
FFN_BLOCK = 512
FFN_TILE = 256
VMEM_LIMIT_BYTES = 56 * 1024 * 1024
BF16_ROWS = 16
WEIGHT_CHUNK_BYTES = 2 * 1024 * 1024


def _dot(a, b):
    return jnp.dot(a, b, preferred_element_type=jnp.float32)


def _dot_nt(a, b):
    return lax.dot_general(a, b, (((1,), (1,)), ((), ())), preferred_element_type=jnp.float32)


def _rmsnorm(x, w):
    return x * lax.rsqrt(jnp.mean(x * x, axis=-1, keepdims=True) + EPS) * w


def _sigmoid(x):
    return 1.0 / (1.0 + jnp.exp2(x * (-LOG2E)))


def _split(x, terms):
    parts = []
    for i in range(terms):
        p = x.astype(MXU_DTYPE)
        parts.append(p)
        if i + 1 < terms:
            x = x - p.astype(jnp.float32)
    return parts


def _select_sum(sel, parts):
    out = _dot(sel, parts[0])
    for p in parts[1:]:
        out = out + _dot(sel, p)
    return out


def _const_spec(shape):
    return pl.BlockSpec(shape, lambda *_: (0,) * len(shape), pipeline_mode=pl.Buffered(1))


_HBM = pl.BlockSpec(memory_space=pl.ANY)


def _chunk_rows(rows, cols):
    r = max(BF16_ROWS, min(rows, WEIGHT_CHUNK_BYTES // (cols * 4)))
    while rows % r or r % BF16_ROWS:
        r -= 1
    return r


def _weight_scratch(rows, cols):
    return [pltpu.VMEM((rows, cols), MXU_DTYPE),
            pltpu.VMEM((2, _chunk_rows(rows, cols), cols), jnp.float32),
            pltpu.SemaphoreType.DMA((2,))]


def _load_weight(w_hbm, w_ref, stage_ref, sem_ref):
    rows = stage_ref.shape[1]
    n = w_hbm.shape[0] // rows

    def chunk_copy(k):
        return pltpu.make_async_copy(w_hbm.at[pl.ds(k * rows, rows)], stage_ref.at[k % 2], sem_ref.at[k % 2])

    chunk_copy(0).start()
    for k in range(n):
        if k + 1 < n:
            chunk_copy(k + 1).start()
        chunk_copy(k).wait()
        w_ref[k * rows:(k + 1) * rows, :] = stage_ref[k % 2].astype(MXU_DTYPE)


def _ffn_kernel(x_ref, nw_ref, wi_hbm, wo_hbm, fnw_ref, o_ref,
                wi_ref, wi_stage, wi_sem, wo_ref, wo_stage, wo_sem, *, layer, d_ff, final_norm):
    @pl.when(pl.program_id(0) == 0)
    def _():
        _load_weight(wi_hbm.at[layer], wi_ref, wi_stage, wi_sem)
        _load_weight(wo_hbm.at[layer], wo_ref, wo_stage, wo_sem)

    x = x_ref[...]
    hn = _rmsnorm(x, nw_ref[...]).astype(MXU_DTYPE)
    acc = x
    for j in range(d_ff // FFN_TILE):
        lo, hi = j * FFN_TILE, (j + 1) * FFN_TILE
        g = _dot(hn, wi_ref[:, lo:hi])
        u = _dot(hn, wi_ref[:, d_ff + lo:d_ff + hi])
        a = (g * _sigmoid(g) * u).astype(MXU_DTYPE)
        acc = acc + _dot(a, wo_ref[lo:hi, :])
    if final_norm:
        acc = _rmsnorm(acc, fnw_ref[...])
    o_ref[...] = acc


def _ffn(x2, norm_w, w_in_all, w_out_all, layer, final_norm_w, final_norm):
    t, d = x2.shape
    d_ff = w_out_all.shape[1]
    assert d_ff % FFN_TILE == 0 and t % FFN_BLOCK == 0
    row = pl.BlockSpec((FFN_BLOCK, d), lambda i: (i, 0))
    return pl.pallas_call(
        functools.partial(_ffn_kernel, layer=layer, d_ff=d_ff, final_norm=final_norm),
        grid=(t // FFN_BLOCK,),
        in_specs=[row, _const_spec((1, d)), _HBM, _HBM, _const_spec((1, d))],
        out_specs=row,
        out_shape=jax.ShapeDtypeStruct((t, d), jnp.float32),
        scratch_shapes=_weight_scratch(d, 2 * d_ff) + _weight_scratch(d_ff, d),
        compiler_params=pltpu.CompilerParams(dimension_semantics=("arbitrary",),
                                             vmem_limit_bytes=VMEM_LIMIT_BYTES),
        name="swiglu_ffn",
    )(x2, norm_w.reshape(1, d), w_in_all, w_out_all, final_norm_w.reshape(1, d))


def _hgrn_scores_single_ref(qd, kd, heads, dk, sc_ref):
    c = qd.shape[0]
    row = lax.broadcasted_iota(jnp.int32, (c, c), 0)
    col = lax.broadcasted_iota(jnp.int32, (c, c), 1)
    causal = col <= row
    for h in range(heads):
        sl = slice(h * dk, (h + 1) * dk)
        s = _dot_nt(qd[:, sl], kd[:, sl])
        sc_ref[h] = jnp.where(causal, s, 0.0).astype(MXU_DTYPE)


def _hgrn_scores_halving(q_c, k_c, lf_parts, heads, dk, sc_ref):
    c = q_c.shape[0]
    row = lax.broadcasted_iota(jnp.int32, (c, c), 0)
    col = lax.broadcasted_iota(jnp.int32, (c, c), 1)
    acc = []
    for h in range(heads):
        sl = slice(h * dk, (h + 1) * dk)
        s = _dot_nt(q_c[:, sl].astype(MXU_DTYPE), k_c[:, sl].astype(MXU_DTYPE))
        acc.append(jnp.where(row == col, s, 0.0))
    m = c // 2
    while m >= 1:
        blk = 2 * m
        ref_of_row = (row // blk) * blk + (m - 1)
        ref_of_col = (col // blk) * blk + (m - 1)
        row_upper = (row % blk) >= m
        sel_q = (row_upper & (col > ref_of_row) & (col <= row)).astype(MXU_DTYPE)
        sel_k = ((~row_upper) & (col > row) & (col <= ref_of_row)).astype(MXU_DTYPE)
        qd = (q_c * jnp.exp(_select_sum(sel_q, lf_parts))).astype(MXU_DTYPE)
        kd = (k_c * jnp.exp(_select_sum(sel_k, lf_parts))).astype(MXU_DTYPE)
        pair = (ref_of_row == ref_of_col) & row_upper & ((col % blk) < m)
        for h in range(heads):
            sl = slice(h * dk, (h + 1) * dk)
            acc[h] = acc[h] + jnp.where(pair, _dot_nt(qd[:, sl], kd[:, sl]), 0.0)
        m //= 2
    for h in range(heads):
        sc_ref[h] = acc[h].astype(MXU_DTYPE)


def _hgrn_kernel(x_ref, nw_ref, win_hbm, lbp_ref, gnw_ref, wout_hbm, o_ref, worst_ref,
                 st_ref, proj_ref, k_ref, sc_ref, y_ref,
                 win_ref, win_stage, win_sem, wout_ref, wout_stage, wout_sem,
                 *, layer, heads, dk, single_ref):
    kd_total = heads * dk
    tb = x_ref.shape[0]
    c = HGRN_CHUNK
    n_chunks = tb // c
    first = (pl.program_id(0) == 0) & (pl.program_id(1) == 0)

    @pl.when(pl.program_id(1) == 0)
    def _():
        st_ref[...] = jnp.zeros_like(st_ref)

    @pl.when(first)
    def _():
        worst_ref[...] = jnp.zeros_like(worst_ref)
        _load_weight(win_hbm.at[layer], win_ref, win_stage, win_sem)
        _load_weight(wout_hbm.at[layer], wout_ref, wout_stage, wout_sem)

    x = x_ref[...]
    hn = _rmsnorm(x, nw_ref[...]).astype(MXU_DTYPE)
    proj = _dot(hn, win_ref[...])

    lp = lbp_ref[...]
    e = jnp.exp(lp - jnp.max(lp, axis=0, keepdims=True))
    lb = jnp.sum(e[: layer + 1], axis=0, keepdims=True) / jnp.sum(e, axis=0, keepdims=True)
    one_m_lb = 1.0 - lb

    q = proj[:, :kd_total]
    fl = proj[:, kd_total:2 * kd_total]
    g = proj[:, 3 * kd_total:]
    en = jnp.exp(-jnp.abs(fl))
    rn = 1.0 / (1.0 + en)
    small = en * rn
    pos = fl >= 0.0
    kk = one_m_lb * jnp.where(pos, small, rn)
    f = lb + one_m_lb * jnp.where(pos, rn, small)
    log_f = jnp.maximum(jnp.log(f), HGRN_LOG_FLOOR)

    proj_ref[:, :kd_total] = q * _sigmoid(q)
    proj_ref[:, kd_total:2 * kd_total] = log_f
    proj_ref[:, 2 * kd_total:3 * kd_total] = proj[:, 2 * kd_total:3 * kd_total]
    proj_ref[:, 3 * kd_total:] = g * _sigmoid(g)
    k_ref[...] = kk

    row = lax.broadcasted_iota(jnp.int32, (c, c), 0)
    col = lax.broadcasted_iota(jnp.int32, (c, c), 1)
    tril = (col <= row).astype(MXU_DTYPE)
    gnw = gnw_ref[...]

    for ci in range(n_chunks):
        rows = slice(ci * c, (ci + 1) * c)
        q_c = proj_ref[rows, :kd_total]
        lf_parts = _split(proj_ref[rows, kd_total:2 * kd_total], 2 if single_ref else 3)
        k_c = k_ref[rows, :]
        b = _select_sum(tril, lf_parts)
        b_last = b[c - 1:c, :]
        dec = jnp.exp(b_last)
        qd = (q_c * jnp.exp(b)).astype(MXU_DTYPE)
        if single_ref:
            kd_f = k_c * jnp.exp(-b)
            _hgrn_scores_single_ref(qd, kd_f.astype(MXU_DTYPE), heads, dk, sc_ref)
            ke = (kd_f * dec).astype(MXU_DTYPE)
            folded = b_last
            width = kd_total
            while width > worst_ref.shape[1]:
                width //= 2
                folded = jnp.minimum(folded[:, :width], folded[:, width:])
            worst_ref[...] = jnp.minimum(worst_ref[...], folded)
        else:
            _hgrn_scores_halving(q_c, k_c, lf_parts, heads, dk, sc_ref)
            ke = (k_c * jnp.exp(b_last - b)).astype(MXU_DTYPE)
        for h in range(heads):
            sl = slice(h * dk, (h + 1) * dk)
            v_h = proj_ref[rows, 2 * kd_total + h * dk:2 * kd_total + (h + 1) * dk]
            st = st_ref[h]
            o = _dot(sc_ref[h], v_h.astype(MXU_DTYPE)) + _dot_nt(qd[:, sl], st.astype(MXU_DTYPE))
            st_ref[h] = st * dec[:, sl] + _dot(v_h.T.astype(MXU_DTYPE), ke[:, sl])
            o = o * lax.rsqrt(jnp.mean(o * o, axis=-1, keepdims=True) + EPS) * gnw
            sg = proj_ref[rows, 3 * kd_total + h * dk:3 * kd_total + (h + 1) * dk]
            y_ref[rows, sl] = (o * sg).astype(MXU_DTYPE)

    o_ref[...] = x + _dot(y_ref[...], wout_ref[...])


def _hgrn_mixer(x2, batch, norm_w, w_in_all, lb_param, gn_w, w_out_all, layer, single_ref):
    t, d = x2.shape
    dv = gn_w.shape[-1]
    w_in_shape, w_out_shape = w_in_all.shape[1:], w_out_all.shape[1:]
    heads = w_out_shape[0] // dv
    kd_total = (w_in_shape[1] - 2 * heads * dv) // 2
    dk = kd_total // heads
    assert dk == dv and kd_total == heads * dv, "kernel assumes key dim == value dim"
    tb = HGRN_BLOCK
    spb = t // batch // tb
    assert spb * tb * batch == t and tb % HGRN_CHUNK == 0
    row = pl.BlockSpec((tb, d), lambda b, s: (b * spb + s, 0))
    return pl.pallas_call(
        functools.partial(_hgrn_kernel, layer=layer, heads=heads, dk=dk, single_ref=single_ref),
        grid=(batch, spb),
        in_specs=[row, _const_spec((1, d)), _HBM, _const_spec(lb_param.shape),
                  _const_spec((1, dv)), _HBM],
        out_specs=[row, pl.BlockSpec((1, dk), lambda b, s: (0, 0))],
        out_shape=[jax.ShapeDtypeStruct((t, d), jnp.float32),
                   jax.ShapeDtypeStruct((1, dk), jnp.float32)],
        scratch_shapes=[
            pltpu.VMEM((heads, dv, dk), jnp.float32),
            pltpu.VMEM((tb, w_in_shape[1]), jnp.float32),
            pltpu.VMEM((tb, kd_total), jnp.float32),
            pltpu.VMEM((heads, HGRN_CHUNK, HGRN_CHUNK), MXU_DTYPE),
            pltpu.VMEM((tb, heads * dv), MXU_DTYPE),
        ] + _weight_scratch(*w_in_shape) + _weight_scratch(*w_out_shape),
        compiler_params=pltpu.CompilerParams(dimension_semantics=("arbitrary", "arbitrary"),
                                             vmem_limit_bytes=VMEM_LIMIT_BYTES),
        name="hgrn2_mixer" if single_ref else "hgrn2_mixer_any_decay",
    )(x2, norm_w.reshape(1, d), w_in_all, lb_param, gn_w.reshape(1, dv), w_out_all)


def _gelu_tanh(x):
    c = -2.0 * math.sqrt(2.0 / math.pi) * LOG2E
    return x * (1.0 / (1.0 + jnp.exp2(x * (c + (c * 0.044715) * (x * x)))))


def _scan_rows(a, b, n):
    rows = lax.broadcasted_iota(jnp.int32, a.shape, 0)
    d = 1
    while d < n:
        live = rows >= d
        a_sh = jnp.where(live, pltpu.roll(a, d, axis=0), 1.0)
        b_sh = jnp.where(live, pltpu.roll(b, d, axis=0), 0.0)
        b = a * b_sh + b
        a = a * a_sh
        d *= 2
    return a, b


def _lru_kernel(x_ref, nw_ref, win_hbm, cw_ref, cb_ref, wa_hbm, ba_ref, wx_hbm, bx_ref, lam_ref,
                wout_hbm, o_ref, h_ref, ext_ref, a_ref, b_ref, hs_ref,
                win_ref, win_stage, win_sem, wa_ref, wa_stage, wa_sem, wx_ref, wx_stage, wx_sem,
                wout_ref, wout_stage, wout_sem, *, layer, width, n_blocks, taps):
    tb = x_ref.shape[0]
    bw = width // n_blocks
    pad = SUBLANES
    n_groups = tb // SUBLANES

    @pl.when((pl.program_id(0) == 0) & (pl.program_id(1) == 0))
    def _():
        _load_weight(win_hbm.at[layer], win_ref, win_stage, win_sem)
        _load_weight(wa_hbm.at[layer], wa_ref, wa_stage, wa_sem)
        _load_weight(wx_hbm.at[layer], wx_ref, wx_stage, wx_sem)
        _load_weight(wout_hbm.at[layer], wout_ref, wout_stage, wout_sem)

    @pl.when(pl.program_id(1) == 0)
    def _():
        h_ref[...] = jnp.zeros_like(h_ref)
        ext_ref[:pad, :] = jnp.zeros((pad, width), jnp.float32)

    x = x_ref[...]
    hn = _rmsnorm(x, nw_ref[...]).astype(MXU_DTYPE)
    proj = _dot(hn, win_ref[...])
    y = _gelu_tanh(proj[:, :width])
    u = proj[:, width:]

    ext_ref[pad:, :] = u
    cw = cw_ref[...]
    uc = cb_ref[...] + u * cw[taps - 1:taps, :]
    for j in range(1, taps):
        uc = uc + ext_ref[pad - j:pad - j + tb, :] * cw[taps - 1 - j:taps - j, :]
    ext_ref[:pad, :] = u[tb - pad:, :]

    ra, rx = [], []
    for n in range(n_blocks):
        ub = uc[:, n * bw:(n + 1) * bw].astype(MXU_DTYPE)
        ra.append(_dot(ub, wa_ref[n * bw:(n + 1) * bw, :]))
        rx.append(_dot(ub, wx_ref[n * bw:(n + 1) * bw, :]))
    r = _sigmoid(jnp.concatenate(ra, axis=-1) + ba_ref[...])
    ig = _sigmoid(jnp.concatenate(rx, axis=-1) + bx_ref[...])

    nl = -lam_ref[...]
    softplus = jnp.maximum(nl, 0.0) + jnp.log1p(jnp.exp(-jnp.abs(nl)))
    rate = (-LRU_C) * softplus
    log_a = rate * r
    a = jnp.exp2((rate * LOG2E) * r)
    b_in = jnp.sqrt(-jnp.tanh(log_a) * (a * a + 1.0)) * (ig * uc)

    n_tiles = width // LANES
    for j in range(n_tiles):
        cols = slice(j * LANES, (j + 1) * LANES)
        a_ref[j] = a[:, cols]
        b_ref[j] = b_in[:, cols]
    group_row = lax.broadcasted_iota(jnp.int32, (n_groups, LANES), 0)
    for j in range(n_tiles):
        cols = slice(j * LANES, (j + 1) * LANES)
        step = lambda r: pl.ds(r, n_groups, stride=SUBLANES)
        a_cum = [a_ref[j, step(0), :]]
        h_loc = [b_ref[j, step(0), :]]
        for r in range(1, SUBLANES):
            a_r = a_ref[j, step(r), :]
            h_loc.append(a_r * h_loc[-1] + b_ref[j, step(r), :])
            a_cum.append(a_r * a_cum[-1])
        a_inc, b_inc = _scan_rows(a_cum[-1], h_loc[-1], n_groups)
        h_prev = h_ref[:, cols]
        h_end = a_inc * h_prev + b_inc
        h_in = jnp.where(group_row >= 1, pltpu.roll(h_end, 1, axis=0), h_prev)
        h_ref[:, cols] = h_end[n_groups - 1:, :]
        for r in range(SUBLANES):
            hs_ref[j, step(r), :] = h_loc[r] + a_cum[r] * h_in
    hs = jnp.concatenate([hs_ref[j] for j in range(n_tiles)], axis=-1)
    hy = (hs * y).astype(MXU_DTYPE)

    o_ref[...] = x + _dot(hy, wout_ref[...])


def _lru_mixer(x2, batch, norm_w, w_in_all, conv_w, conv_b, wa_all, ba, wx_all, bx, lam, w_out_all,
               layer):
    t, d = x2.shape
    width = w_out_all.shape[1]
    n_blocks, bw = wa_all.shape[1], wa_all.shape[2]
    taps = conv_w.shape[0]
    wa_all = wa_all.reshape(wa_all.shape[0], n_blocks * bw, bw)
    wx_all = wx_all.reshape(wx_all.shape[0], n_blocks * bw, bw)
    tb = LRU_BLOCK
    spb = t // batch // tb
    assert spb * tb * batch == t and taps - 1 <= SUBLANES and tb % SUBLANES == 0
    row = pl.BlockSpec((tb, d), lambda b, s: (b * spb + s, 0))
    vec = lambda v: v.reshape(1, width)
    return pl.pallas_call(
        functools.partial(_lru_kernel, layer=layer, width=width, n_blocks=n_blocks, taps=taps),
        grid=(batch, spb),
        in_specs=[row, _const_spec((1, d)), _HBM, _const_spec(conv_w.shape),
                  _const_spec((1, width)), _HBM, _const_spec((1, width)),
                  _HBM, _const_spec((1, width)), _const_spec((1, width)), _HBM],
        out_specs=row,
        out_shape=jax.ShapeDtypeStruct((t, d), jnp.float32),
        scratch_shapes=[
            pltpu.VMEM((1, width), jnp.float32),
            pltpu.VMEM((tb + SUBLANES, width), jnp.float32),
            pltpu.VMEM((width // LANES, tb, LANES), jnp.float32),
            pltpu.VMEM((width // LANES, tb, LANES), jnp.float32),
            pltpu.VMEM((width // LANES, tb, LANES), jnp.float32),
        ] + _weight_scratch(*w_in_all.shape[1:]) + _weight_scratch(*wa_all.shape[1:])
        + _weight_scratch(*wx_all.shape[1:]) + _weight_scratch(*w_out_all.shape[1:]),
        compiler_params=pltpu.CompilerParams(dimension_semantics=("arbitrary", "arbitrary"),
                                             vmem_limit_bytes=VMEM_LIMIT_BYTES),
        name="rglru_mixer",
    )(x2, norm_w.reshape(1, d), w_in_all, conv_w, vec(conv_b), wa_all, vec(ba), wx_all, vec(bx),
      vec(lam), w_out_all)


def kernel(x, norm_mix, norm_ffn, norm_final, hgrn_w_in, hgrn_lb, hgrn_norm, hgrn_w_out, lru_w_in, lru_conv_w, lru_conv_b, lru_wa, lru_ba, lru_wx, lru_bx, lru_lambda, lru_w_out, ffn_w_in, ffn_w_out):
    batch, seq, d = x.shape
    depth = norm_mix.shape[0]
    n_mixers = 2

    def ffn(l, h):
        return _ffn(h, norm_ffn[l], ffn_w_in, ffn_w_out, l, norm_final, final_norm=(l == depth - 1))

    def layers_from(l, h):
        if l == depth:
            return h
        j = l // n_mixers
        if l % n_mixers == 1:
            h = _lru_mixer(h, batch, norm_mix[l], lru_w_in, lru_conv_w[j], lru_conv_b[j],
                           lru_wa, lru_ba[j], lru_wx, lru_bx[j], lru_lambda[j], lru_w_out, j)
            return layers_from(l + 1, ffn(l, h))
        hgrn = functools.partial(_hgrn_mixer, h, batch, norm_mix[l], hgrn_w_in, hgrn_lb,
                                 hgrn_norm[j], hgrn_w_out, j)
        mixed, worst = hgrn(single_ref=True)
        return lax.cond(jnp.min(worst) >= -HGRN_SAFE_EXP,
                        lambda: layers_from(l + 1, ffn(l, mixed)),
                        lambda: layers_from(l + 1, ffn(l, hgrn(single_ref=False)[0])))

    return layers_from(0, x.reshape(batch * seq, d)).reshape(batch, seq, d)
```

```python
import functools
import math

import jax
import jax.numpy as jnp
from jax import lax
from jax.experimental import pallas as pl
from jax.experimental.pallas import tpu as pltpu

EPS = 1e-6
LRU_C = 8.0
LOG2E = math.log2(math.e)
MXU_DTYPE = jnp.bfloat16
SUBLANES = 8
LANES = 128

HGRN_CHUNK = 128
HGRN_SAFE_EXP = 80.0
HGRN_LOG_FLOOR = -1e4
HGRN_BLOCK = 512
LRU_BLOCK = 512
FFN_BLOCK = 512
FFN_TILE = 256
VMEM_LIMIT_BYTES = 56 * 1024 * 1024
BF16_ROWS = 16
WEIGHT_CHUNK_BYTES = 2 * 1024 * 1024


def _dot(a, b):
    return jnp.dot(a, b, preferred_element_type=jnp.float32)


def _dot_nt(a, b):
    return lax.dot_general(a, b, (((1,), (1,)), ((), ())), preferred_element_type=jnp.float32)


def _rmsnorm(x, w):
    return x * lax.rsqrt(jnp.mean(x * x, axis=-1, keepdims=True) + EPS) * w


def _sigmoid(x):
    return 1.0 / (1.0 + jnp.exp2(x * (-LOG2E)))


def _split(x, terms):
    parts = []
    for i in range(terms):
        p = x.astype(MXU_DTYPE)
        parts.append(p)
        if i + 1 < terms:
            x = x - p.astype(jnp.float32)
    return parts


def _select_sum(sel, parts):
    out = _dot(sel, parts[0])
    for p in parts[1:]:
        out = out + _dot(sel, p)
    return out


def _const_spec(shape):
    return pl.BlockSpec(shape, lambda *_: (0,) * len(shape), pipeline_mode=pl.Buffered(1))


_HBM = pl.BlockSpec(memory_space=pl.ANY)


def _chunk_rows(rows, cols):
    r = max(BF16_ROWS, min(rows, WEIGHT_CHUNK_BYTES // (cols * 4)))
    while rows % r or r % BF16_ROWS:
        r -= 1
    return r


def _weight_scratch(rows, cols):
    return [pltpu.VMEM((rows, cols), MXU_DTYPE),
            pltpu.VMEM((2, _chunk_rows(rows, cols), cols), jnp.float32),
            pltpu.SemaphoreType.DMA((2,))]


def _load_weight(w_hbm, w_ref, stage_ref, sem_ref):
    rows = stage_ref.shape[1]
    n = w_hbm.shape[0] // rows

    def chunk_copy(k):
        return pltpu.make_async_copy(w_hbm.at[pl.ds(k * rows, rows)], stage_ref.at[k % 2], sem_ref.at[k % 2])

    chunk_copy(0).start()
    for k in range(n):
        if k + 1 < n:
            chunk_copy(k + 1).start()
        chunk_copy(k).wait()
        w_ref[k * rows:(k + 1) * rows, :] = stage_ref[k % 2].astype(MXU_DTYPE)


def _ffn_kernel(x_ref, nw_ref, wi_ref, wo_ref, fnw_ref, o_ref, *, d_ff, final_norm):
    x = x_ref[...]
    hn = _rmsnorm(x, nw_ref[...]).astype(MXU_DTYPE)
    acc = x
    for j in range(d_ff // FFN_TILE):
        lo, hi = j * FFN_TILE, (j + 1) * FFN_TILE
        g = _dot(hn, wi_ref[:, lo:hi])
        u = _dot(hn, wi_ref[:, d_ff + lo:d_ff + hi])
        a = (g * _sigmoid(g) * u).astype(MXU_DTYPE)
        acc = acc + _dot(a, wo_ref[lo:hi, :])
    if final_norm:
        acc = _rmsnorm(acc, fnw_ref[...])
    o_ref[...] = acc


def _ffn(x2, norm_w, w_in, w_out, final_norm_w, final_norm):
    t, d = x2.shape
    d_ff = w_in.shape[1] // 2
    w_out = w_out.reshape(d_ff, d)
    assert d_ff % FFN_TILE == 0 and t % FFN_BLOCK == 0
    row = pl.BlockSpec((FFN_BLOCK, d), lambda i: (i, 0))
    return pl.pallas_call(
        functools.partial(_ffn_kernel, d_ff=d_ff, final_norm=final_norm),
        grid=(t // FFN_BLOCK,),
        in_specs=[row, _const_spec((1, d)), _const_spec(w_in.shape), _const_spec(w_out.shape),
                  _const_spec((1, d))],
        out_specs=row,
        out_shape=jax.ShapeDtypeStruct((t, d), jnp.float32),
        compiler_params=pltpu.CompilerParams(dimension_semantics=("arbitrary",),
                                             vmem_limit_bytes=VMEM_LIMIT_BYTES),
        name="swiglu_ffn",
    )(x2, norm_w.reshape(1, d), w_in, w_out, final_norm_w.reshape(1, d))


class _FfnWeightCast:
    def __init__(self, w_in_all, w_out_all, layer, n_steps):
        cols = w_in_all.shape[2]
        self.arrays = [w_in_all, w_out_all.reshape(w_out_all.shape[0], -1, cols)]
        self.layer, self.n_steps = layer, n_steps
        for w in self.arrays:
            assert w.shape[1] % n_steps == 0 and (w.shape[1] // n_steps) % BF16_ROWS == 0

    def in_specs(self, step_of):
        return [pl.BlockSpec((None, w.shape[1] // self.n_steps, w.shape[2]),
                             lambda *g, _s=step_of: (self.layer, _s(*g), 0)) for w in self.arrays]

    def out_specs(self, step_of):
        return [pl.BlockSpec((w.shape[1] // self.n_steps, w.shape[2]),
                             lambda *g, _s=step_of: (_s(*g), 0)) for w in self.arrays]

    def out_shapes(self):
        return [jax.ShapeDtypeStruct(w.shape[1:], MXU_DTYPE) for w in self.arrays]


def _cast_slices(in_refs, out_refs):
    for i_ref, o_ref in zip(in_refs, out_refs):
        o_ref[...] = i_ref[...].astype(MXU_DTYPE)


def _hgrn_scores_single_ref(qd, kd, heads, dk, sc_ref):
    c = qd.shape[0]
    row = lax.broadcasted_iota(jnp.int32, (c, c), 0)
    col = lax.broadcasted_iota(jnp.int32, (c, c), 1)
    causal = col <= row
    for h in range(heads):
        sl = slice(h * dk, (h + 1) * dk)
        s = _dot_nt(qd[:, sl], kd[:, sl])
        sc_ref[h] = jnp.where(causal, s, 0.0).astype(MXU_DTYPE)


def _hgrn_scores_halving(q_c, k_c, lf_parts, heads, dk, sc_ref):
    c = q_c.shape[0]
    row = lax.broadcasted_iota(jnp.int32, (c, c), 0)
    col = lax.broadcasted_iota(jnp.int32, (c, c), 1)
    acc = []
    for h in range(heads):
        sl = slice(h * dk, (h + 1) * dk)
        s = _dot_nt(q_c[:, sl].astype(MXU_DTYPE), k_c[:, sl].astype(MXU_DTYPE))
        acc.append(jnp.where(row == col, s, 0.0))
    m = c // 2
    while m >= 1:
        blk = 2 * m
        ref_of_row = (row // blk) * blk + (m - 1)
        ref_of_col = (col // blk) * blk + (m - 1)
        row_upper = (row % blk) >= m
        sel_q = (row_upper & (col > ref_of_row) & (col <= row)).astype(MXU_DTYPE)
        sel_k = ((~row_upper) & (col > row) & (col <= ref_of_row)).astype(MXU_DTYPE)
        qd = (q_c * jnp.exp(_select_sum(sel_q, lf_parts))).astype(MXU_DTYPE)
        kd = (k_c * jnp.exp(_select_sum(sel_k, lf_parts))).astype(MXU_DTYPE)
        pair = (ref_of_row == ref_of_col) & row_upper & ((col % blk) < m)
        for h in range(heads):
            sl = slice(h * dk, (h + 1) * dk)
            acc[h] = acc[h] + jnp.where(pair, _dot_nt(qd[:, sl], kd[:, sl]), 0.0)
        m //= 2
    for h in range(heads):
        sc_ref[h] = acc[h].astype(MXU_DTYPE)


def _hgrn_kernel(x_ref, nw_ref, win_hbm, lbp_ref, gnw_ref, wout_hbm, fwi_f32, fwo_f32,
                 o_ref, worst_ref, fwi_ref, fwo_ref, st_ref, proj_ref, k_ref, sc_ref, y_ref,
                 win_ref, win_stage, win_sem, wout_ref, wout_stage, wout_sem,
                 *, layer, heads, dk, single_ref):
    kd_total = heads * dk
    tb = x_ref.shape[0]
    c = HGRN_CHUNK
    n_chunks = tb // c
    first = (pl.program_id(0) == 0) & (pl.program_id(1) == 0)

    @pl.when(pl.program_id(1) == 0)
    def _():
        st_ref[...] = jnp.zeros_like(st_ref)

    @pl.when(first)
    def _():
        worst_ref[...] = jnp.zeros_like(worst_ref)
        _load_weight(win_hbm.at[layer], win_ref, win_stage, win_sem)
        _load_weight(wout_hbm.at[layer], wout_ref, wout_stage, wout_sem)

    _cast_slices((fwi_f32, fwo_f32), (fwi_ref, fwo_ref))
    x = x_ref[...]
    hn = _rmsnorm(x, nw_ref[...]).astype(MXU_DTYPE)
    proj = _dot(hn, win_ref[...])

    lp = lbp_ref[...]
    e = jnp.exp(lp - jnp.max(lp, axis=0, keepdims=True))
    lb = jnp.sum(e[: layer + 1], axis=0, keepdims=True) / jnp.sum(e, axis=0, keepdims=True)
    one_m_lb = 1.0 - lb

    q = proj[:, :kd_total]
    fl = proj[:, kd_total:2 * kd_total]
    g = proj[:, 3 * kd_total:]
    en = jnp.exp(-jnp.abs(fl))
    rn = 1.0 / (1.0 + en)
    small = en * rn
    pos = fl >= 0.0
    kk = one_m_lb * jnp.where(pos, small, rn)
    f = lb + one_m_lb * jnp.where(pos, rn, small)
    log_f = jnp.maximum(jnp.log(f), HGRN_LOG_FLOOR)

    proj_ref[:, :kd_total] = q * _sigmoid(q)
    proj_ref[:, kd_total:2 * kd_total] = log_f
    proj_ref[:, 2 * kd_total:3 * kd_total] = proj[:, 2 * kd_total:3 * kd_total]
    proj_ref[:, 3 * kd_total:] = g * _sigmoid(g)
    k_ref[...] = kk

    row = lax.broadcasted_iota(jnp.int32, (c, c), 0)
    col = lax.broadcasted_iota(jnp.int32, (c, c), 1)
    tril = (col <= row).astype(MXU_DTYPE)
    gnw = gnw_ref[...]

    for ci in range(n_chunks):
        rows = slice(ci * c, (ci + 1) * c)
        q_c = proj_ref[rows, :kd_total]
        lf_parts = _split(proj_ref[rows, kd_total:2 * kd_total], 2 if single_ref else 3)
        k_c = k_ref[rows, :]
        b = _select_sum(tril, lf_parts)
        b_last = b[c - 1:c, :]
        dec = jnp.exp(b_last)
        qd = (q_c * jnp.exp(b)).astype(MXU_DTYPE)
        if single_ref:
            kd_f = k_c * jnp.exp(-b)
            _hgrn_scores_single_ref(qd, kd_f.astype(MXU_DTYPE), heads, dk, sc_ref)
            ke = (kd_f * dec).astype(MXU_DTYPE)
            folded = b_last
            width = kd_total
            while width > worst_ref.shape[1]:
                width //= 2
                folded = jnp.minimum(folded[:, :width], folded[:, width:])
            worst_ref[...] = jnp.minimum(worst_ref[...], folded)
        else:
            _hgrn_scores_halving(q_c, k_c, lf_parts, heads, dk, sc_ref)
            ke = (k_c * jnp.exp(b_last - b)).astype(MXU_DTYPE)
        for h in range(heads):
            sl = slice(h * dk, (h + 1) * dk)
            v_h = proj_ref[rows, 2 * kd_total + h * dk:2 * kd_total + (h + 1) * dk]
            st = st_ref[h]
            o = _dot(sc_ref[h], v_h.astype(MXU_DTYPE)) + _dot_nt(qd[:, sl], st.astype(MXU_DTYPE))
            st_ref[h] = st * dec[:, sl] + _dot(v_h.T.astype(MXU_DTYPE), ke[:, sl])
            o = o * lax.rsqrt(jnp.mean(o * o, axis=-1, keepdims=True) + EPS) * gnw
            sg = proj_ref[rows, 3 * kd_total + h * dk:3 * kd_total + (h + 1) * dk]
            y_ref[rows, sl] = (o * sg).astype(MXU_DTYPE)

    o_ref[...] = x + _dot(y_ref[...], wout_ref[...])


def _hgrn_mixer(x2, batch, norm_w, w_in_all, lb_param, gn_w, w_out_all, layer, ffn_w_in_all,
                ffn_w_out_all, ffn_layer, single_ref):
    t, d = x2.shape
    dv = gn_w.shape[-1]
    w_in_shape, w_out_shape = w_in_all.shape[1:], w_out_all.shape[1:]
    heads = w_out_shape[0] // dv
    kd_total = (w_in_shape[1] - 2 * heads * dv) // 2
    dk = kd_total // heads
    assert dk == dv and kd_total == heads * dv, "kernel assumes key dim == value dim"
    tb = HGRN_BLOCK
    spb = t // batch // tb
    assert spb * tb * batch == t and tb % HGRN_CHUNK == 0
    step_of = lambda b, s: b * spb + s
    row = pl.BlockSpec((tb, d), lambda b, s: (step_of(b, s), 0))
    cast = _FfnWeightCast(ffn_w_in_all, ffn_w_out_all, ffn_layer, batch * spb)
    return pl.pallas_call(
        functools.partial(_hgrn_kernel, layer=layer, heads=heads, dk=dk, single_ref=single_ref),
        grid=(batch, spb),
        in_specs=[row, _const_spec((1, d)), _HBM, _const_spec(lb_param.shape),
                  _const_spec((1, dv)), _HBM] + cast.in_specs(step_of),
        out_specs=[row, pl.BlockSpec((1, dk), lambda b, s: (0, 0))] + cast.out_specs(step_of),
        out_shape=[jax.ShapeDtypeStruct((t, d), jnp.float32),
                   jax.ShapeDtypeStruct((1, dk), jnp.float32)] + cast.out_shapes(),
        scratch_shapes=[
            pltpu.VMEM((heads, dv, dk), jnp.float32),
            pltpu.VMEM((tb, w_in_shape[1]), jnp.float32),
            pltpu.VMEM((tb, kd_total), jnp.float32),
            pltpu.VMEM((heads, HGRN_CHUNK, HGRN_CHUNK), MXU_DTYPE),
            pltpu.VMEM((tb, heads * dv), MXU_DTYPE),
        ] + _weight_scratch(*w_in_shape) + _weight_scratch(*w_out_shape),
        compiler_params=pltpu.CompilerParams(dimension_semantics=("arbitrary", "arbitrary"),
                                             vmem_limit_bytes=VMEM_LIMIT_BYTES),
        name="hgrn2_mixer" if single_ref else "hgrn2_mixer_any_decay",
    )(x2, norm_w.reshape(1, d), w_in_all, lb_param, gn_w.reshape(1, dv), w_out_all, *cast.arrays)


def _gelu_tanh(x):
    c = -2.0 * math.sqrt(2.0 / math.pi) * LOG2E
    return x * (1.0 / (1.0 + jnp.exp2(x * (c + (c * 0.044715) * (x * x)))))


def _scan_rows(a, b, n):
    rows = lax.broadcasted_iota(jnp.int32, a.shape, 0)
    d = 1
    while d < n:
        live = rows >= d
        a_sh = jnp.where(live, pltpu.roll(a, d, axis=0), 1.0)
        b_sh = jnp.where(live, pltpu.roll(b, d, axis=0), 0.0)
        b = a * b_sh + b
        a = a * a_sh
        d *= 2
    return a, b


def _lru_kernel(x_ref, nw_ref, win_hbm, cw_ref, cb_ref, wa_hbm, ba_ref, wx_hbm, bx_ref, lam_ref,
                wout_hbm, fwi_f32, fwo_f32, o_ref, fwi_ref, fwo_ref, h_ref, ext_ref, a_ref, b_ref, hs_ref,
                win_ref, win_stage, win_sem, wa_ref, wa_stage, wa_sem, wx_ref, wx_stage, wx_sem,
                wout_ref, wout_stage, wout_sem, *, layer, width, n_blocks, taps):
    tb = x_ref.shape[0]
    bw = width // n_blocks
    pad = SUBLANES
    n_groups = tb // SUBLANES

    @pl.when((pl.program_id(0) == 0) & (pl.program_id(1) == 0))
    def _():
        _load_weight(win_hbm.at[layer], win_ref, win_stage, win_sem)
        _load_weight(wa_hbm.at[layer], wa_ref, wa_stage, wa_sem)
        _load_weight(wx_hbm.at[layer], wx_ref, wx_stage, wx_sem)
        _load_weight(wout_hbm.at[layer], wout_ref, wout_stage, wout_sem)

    @pl.when(pl.program_id(1) == 0)
    def _():
        h_ref[...] = jnp.zeros_like(h_ref)
        ext_ref[:pad, :] = jnp.zeros((pad, width), jnp.float32)

    _cast_slices((fwi_f32, fwo_f32), (fwi_ref, fwo_ref))
    x = x_ref[...]
    hn = _rmsnorm(x, nw_ref[...]).astype(MXU_DTYPE)
    proj = _dot(hn, win_ref[...])
    y = _gelu_tanh(proj[:, :width])
    u = proj[:, width:]

    ext_ref[pad:, :] = u
    cw = cw_ref[...]
    uc = cb_ref[...] + u * cw[taps - 1:taps, :]
    for j in range(1, taps):
        uc = uc + ext_ref[pad - j:pad - j + tb, :] * cw[taps - 1 - j:taps - j, :]
    ext_ref[:pad, :] = u[tb - pad:, :]

    ra, rx = [], []
    for n in range(n_blocks):
        ub = uc[:, n * bw:(n + 1) * bw].astype(MXU_DTYPE)
        ra.append(_dot(ub, wa_ref[n * bw:(n + 1) * bw, :]))
        rx.append(_dot(ub, wx_ref[n * bw:(n + 1) * bw, :]))
    r = _sigmoid(jnp.concatenate(ra, axis=-1) + ba_ref[...])
    ig = _sigmoid(jnp.concatenate(rx, axis=-1) + bx_ref[...])

    nl = -lam_ref[...]
    softplus = jnp.maximum(nl, 0.0) + jnp.log1p(jnp.exp(-jnp.abs(nl)))
    rate = (-LRU_C) * softplus
    log_a = rate * r
    a = jnp.exp2((rate * LOG2E) * r)
    b_in = jnp.sqrt(-jnp.tanh(log_a) * (a * a + 1.0)) * (ig * uc)

    n_tiles = width // LANES
    for j in range(n_tiles):
        cols = slice(j * LANES, (j + 1) * LANES)
        a_ref[j] = a[:, cols]
        b_ref[j] = b_in[:, cols]
    group_row = lax.broadcasted_iota(jnp.int32, (n_groups, LANES), 0)
    for j in range(n_tiles):
        cols = slice(j * LANES, (j + 1) * LANES)
        step = lambda r: pl.ds(r, n_groups, stride=SUBLANES)
        a_cum = [a_ref[j, step(0), :]]
        h_loc = [b_ref[j, step(0), :]]
        for r in range(1, SUBLANES):
            a_r = a_ref[j, step(r), :]
            h_loc.append(a_r * h_loc[-1] + b_ref[j, step(r), :])
            a_cum.append(a_r * a_cum[-1])
        a_inc, b_inc = _scan_rows(a_cum[-1], h_loc[-1], n_groups)
        h_prev = h_ref[:, cols]
        h_end = a_inc * h_prev + b_inc
        h_in = jnp.where(group_row >= 1, pltpu.roll(h_end, 1, axis=0), h_prev)
        h_ref[:, cols] = h_end[n_groups - 1:, :]
        for r in range(SUBLANES):
            hs_ref[j, step(r), :] = h_loc[r] + a_cum[r] * h_in
    hs = jnp.concatenate([hs_ref[j] for j in range(n_tiles)], axis=-1)
    hy = (hs * y).astype(MXU_DTYPE)

    o_ref[...] = x + _dot(hy, wout_ref[...])


def _lru_mixer(x2, batch, norm_w, w_in_all, conv_w, conv_b, wa_all, ba, wx_all, bx, lam, w_out_all,
               layer, ffn_w_in_all, ffn_w_out_all, ffn_layer):
    t, d = x2.shape
    width = w_out_all.shape[1]
    n_blocks, bw = wa_all.shape[1], wa_all.shape[2]
    taps = conv_w.shape[0]
    wa_all = wa_all.reshape(wa_all.shape[0], n_blocks * bw, bw)
    wx_all = wx_all.reshape(wx_all.shape[0], n_blocks * bw, bw)
    tb = LRU_BLOCK
    spb = t // batch // tb
    assert spb * tb * batch == t and taps - 1 <= SUBLANES and tb % SUBLANES == 0
    step_of = lambda b, s: b * spb + s
    row = pl.BlockSpec((tb, d), lambda b, s: (step_of(b, s), 0))
    vec = lambda v: v.reshape(1, width)
    cast = _FfnWeightCast(ffn_w_in_all, ffn_w_out_all, ffn_layer, batch * spb)
    return pl.pallas_call(
        functools.partial(_lru_kernel, layer=layer, width=width, n_blocks=n_blocks, taps=taps),
        grid=(batch, spb),
        in_specs=[row, _const_spec((1, d)), _HBM, _const_spec(conv_w.shape),
                  _const_spec((1, width)), _HBM, _const_spec((1, width)),
                  _HBM, _const_spec((1, width)), _const_spec((1, width)), _HBM]
        + cast.in_specs(step_of),
        out_specs=[row] + cast.out_specs(step_of),
        out_shape=[jax.ShapeDtypeStruct((t, d), jnp.float32)] + cast.out_shapes(),
        scratch_shapes=[
            pltpu.VMEM((1, width), jnp.float32),
            pltpu.VMEM((tb + SUBLANES, width), jnp.float32),
            pltpu.VMEM((width // LANES, tb, LANES), jnp.float32),
            pltpu.VMEM((width // LANES, tb, LANES), jnp.float32),
            pltpu.VMEM((width // LANES, tb, LANES), jnp.float32),
        ] + _weight_scratch(*w_in_all.shape[1:]) + _weight_scratch(*wa_all.shape[1:])
        + _weight_scratch(*wx_all.shape[1:]) + _weight_scratch(*w_out_all.shape[1:]),
        compiler_params=pltpu.CompilerParams(dimension_semantics=("arbitrary", "arbitrary"),
                                             vmem_limit_bytes=VMEM_LIMIT_BYTES),
        name="rglru_mixer",
    )(x2, norm_w.reshape(1, d), w_in_all, conv_w, vec(conv_b), wa_all, vec(ba), wx_all, vec(bx),
      vec(lam), w_out_all, *cast.arrays)


def kernel(x, norm_mix, norm_ffn, norm_final, hgrn_w_in, hgrn_lb, hgrn_norm, hgrn_w_out, lru_w_in, lru_conv_w, lru_conv_b, lru_wa, lru_ba, lru_wx, lru_bx, lru_lambda, lru_w_out, ffn_w_in, ffn_w_out):
    batch, seq, d = x.shape
    depth = norm_mix.shape[0]
    n_mixers = 2

    def ffn(l, h, w_in, w_out):
        return _ffn(h, norm_ffn[l], w_in, w_out, norm_final, final_norm=(l == depth - 1))

    def layers_from(l, h):
        if l == depth:
            return h
        j = l // n_mixers
        if l % n_mixers == 1:
            mixed = _lru_mixer(h, batch, norm_mix[l], lru_w_in, lru_conv_w[j], lru_conv_b[j],
                               lru_wa, lru_ba[j], lru_wx, lru_bx[j], lru_lambda[j], lru_w_out, j,
                               ffn_w_in, ffn_w_out, l)
            return layers_from(l + 1, ffn(l, *mixed))
        hgrn = functools.partial(_hgrn_mixer, h, batch, norm_mix[l], hgrn_w_in, hgrn_lb,
                                 hgrn_norm[j], hgrn_w_out, j, ffn_w_in, ffn_w_out, l)
        mixed, worst, w_in, w_out = hgrn(single_ref=True)

        def any_decay():
            mixed, _, w_in, w_out = hgrn(single_ref=False)
            return layers_from(l + 1, ffn(l, mixed, w_in, w_out))

        return lax.cond(jnp.min(worst) >= -HGRN_SAFE_EXP,
                        lambda: layers_from(l + 1, ffn(l, mixed, w_in, w_out)), any_decay)

    return layers_from(0, x.reshape(batch * seq, d)).reshape(batch, seq, d)
```

```python
import functools
import math

import jax
import jax.numpy as jnp
from jax import lax
from jax.experimental import pallas as pl
from jax.experimental.pallas import tpu as pltpu

EPS = 1e-6
LRU_C = 8.0
LOG2E = math.log2(math.e)
MXU_DTYPE = jnp.bfloat16
SUBLANES = 8
LANES = 128

HGRN_CHUNK = 128
HGRN_SAFE_EXP = 80.0
HGRN_LOG_FLOOR = -1e4
HGRN_BLOCK = 512
LRU_BLOCK = 512
FFN_BLOCK = 512
FFN_TILE = 256
VMEM_LIMIT_BYTES = 56 * 1024 * 1024
BF16_ROWS = 16
WEIGHT_CHUNK_BYTES = 2 * 1024 * 1024


def _dot(a, b):
    return jnp.dot(a, b, preferred_element_type=jnp.float32)


def _dot_nt(a, b):
    return lax.dot_general(a, b, (((1,), (1,)), ((), ())), preferred_element_type=jnp.float32)


def _rmsnorm(x, w):
    return x * lax.rsqrt(jnp.mean(x * x, axis=-1, keepdims=True) + EPS) * w


def _sigmoid(x):
    return 1.0 / (1.0 + jnp.exp2(x * (-LOG2E)))


def _split(x, terms):
    parts = []
    for i in range(terms):
        p = x.astype(MXU_DTYPE)
        parts.append(p)
        if i + 1 < terms:
            x = x - p.astype(jnp.float32)
    return parts


def _select_sum(sel, parts):
    out = _dot(sel, parts[0])
    for p in parts[1:]:
        out = out + _dot(sel, p)
    return out


def _const_spec(shape):
    return pl.BlockSpec(shape, lambda *_: (0,) * len(shape), pipeline_mode=pl.Buffered(1))


_HBM = pl.BlockSpec(memory_space=pl.ANY)


def _chunk_rows(rows, cols):
    r = max(BF16_ROWS, min(rows, WEIGHT_CHUNK_BYTES // (cols * 4)))
    while rows % r or r % BF16_ROWS:
        r -= 1
    return r


def _weight_scratch(rows, cols):
    return [pltpu.VMEM((rows, cols), MXU_DTYPE),
            pltpu.VMEM((2, _chunk_rows(rows, cols), cols), jnp.float32),
            pltpu.SemaphoreType.DMA((2,))]


def _load_weight(w_hbm, w_ref, stage_ref, sem_ref):
    rows = stage_ref.shape[1]
    n = w_hbm.shape[0] // rows

    def chunk_copy(k):
        return pltpu.make_async_copy(w_hbm.at[pl.ds(k * rows, rows)], stage_ref.at[k % 2], sem_ref.at[k % 2])

    chunk_copy(0).start()
    for k in range(n):
        if k + 1 < n:
            chunk_copy(k + 1).start()
        chunk_copy(k).wait()
        w_ref[k * rows:(k + 1) * rows, :] = stage_ref[k % 2].astype(MXU_DTYPE)


def _ffn_kernel(x_ref, nw_ref, wi_ref, wo_ref, fnw_ref, o_ref, *, d_ff, final_norm):
    x = x_ref[...]
    hn = _rmsnorm(x, nw_ref[...]).astype(MXU_DTYPE)
    acc = x
    for j in range(d_ff // FFN_TILE):
        lo, hi = j * FFN_TILE, (j + 1) * FFN_TILE
        g = _dot(hn, wi_ref[:, lo:hi])
        u = _dot(hn, wi_ref[:, d_ff + lo:d_ff + hi])
        a = (g * _sigmoid(g) * u).astype(MXU_DTYPE)
        acc = acc + _dot(a, wo_ref[lo:hi, :])
    if final_norm:
        acc = _rmsnorm(acc, fnw_ref[...])
    o_ref[...] = acc


def _ffn(x2, norm_w, w_in, w_out, final_norm_w, final_norm):
    t, d = x2.shape
    d_ff = w_out.shape[0]
    assert d_ff % FFN_TILE == 0 and t % FFN_BLOCK == 0
    row = pl.BlockSpec((FFN_BLOCK, d), lambda i: (i, 0))
    return pl.pallas_call(
        functools.partial(_ffn_kernel, d_ff=d_ff, final_norm=final_norm),
        grid=(t // FFN_BLOCK,),
        in_specs=[row, _const_spec((1, d)), _const_spec(w_in.shape), _const_spec(w_out.shape),
                  _const_spec((1, d))],
        out_specs=row,
        out_shape=jax.ShapeDtypeStruct((t, d), jnp.float32),
        compiler_params=pltpu.CompilerParams(dimension_semantics=("arbitrary",),
                                             vmem_limit_bytes=VMEM_LIMIT_BYTES),
        name="swiglu_ffn",
    )(x2, norm_w.reshape(1, d), w_in, w_out, final_norm_w.reshape(1, d))


class _FfnWeightCast:
    def __init__(self, w_in_all, w_out_all, layer, n_steps):
        self.arrays = [w_in_all, w_out_all]
        self.layer = layer
        self.plan = []
        for w in self.arrays:
            rows, share = w.shape[1], 1
            while (rows * share) % (n_steps * BF16_ROWS):
                share *= 2
                assert share <= n_steps, "weight rows do not split into packed-row blocks"
            self.plan.append((rows * share // n_steps, share))

    def in_specs(self, step_of):
        return [pl.BlockSpec((None, r, w.shape[2]), lambda *g, _k=k: (self.layer, step_of(*g) // _k, 0))
                for w, (r, k) in zip(self.arrays, self.plan)]

    def out_specs(self, step_of):
        return [pl.BlockSpec((r, w.shape[2]), lambda *g, _k=k: (step_of(*g) // _k, 0))
                for w, (r, k) in zip(self.arrays, self.plan)]

    def out_shapes(self):
        return [jax.ShapeDtypeStruct(w.shape[1:], MXU_DTYPE) for w in self.arrays]


def _cast_slices(in_refs, out_refs):
    for i_ref, o_ref in zip(in_refs, out_refs):
        o_ref[...] = i_ref[...].astype(MXU_DTYPE)


def _hgrn_scores_single_ref(qd, kd, heads, dk, sc_ref):
    c = qd.shape[0]
    row = lax.broadcasted_iota(jnp.int32, (c, c), 0)
    col = lax.broadcasted_iota(jnp.int32, (c, c), 1)
    causal = col <= row
    for h in range(heads):
        sl = slice(h * dk, (h + 1) * dk)
        s = _dot_nt(qd[:, sl], kd[:, sl])
        sc_ref[h] = jnp.where(causal, s, 0.0).astype(MXU_DTYPE)


def _hgrn_scores_halving(q_c, k_c, lf_parts, heads, dk, sc_ref):
    c = q_c.shape[0]
    row = lax.broadcasted_iota(jnp.int32, (c, c), 0)
    col = lax.broadcasted_iota(jnp.int32, (c, c), 1)
    acc = []
    for h in range(heads):
        sl = slice(h * dk, (h + 1) * dk)
        s = _dot_nt(q_c[:, sl].astype(MXU_DTYPE), k_c[:, sl].astype(MXU_DTYPE))
        acc.append(jnp.where(row == col, s, 0.0))
    m = c // 2
    while m >= 1:
        blk = 2 * m
        ref_of_row = (row // blk) * blk + (m - 1)
        ref_of_col = (col // blk) * blk + (m - 1)
        row_upper = (row % blk) >= m
        sel_q = (row_upper & (col > ref_of_row) & (col <= row)).astype(MXU_DTYPE)
        sel_k = ((~row_upper) & (col > row) & (col <= ref_of_row)).astype(MXU_DTYPE)
        qd = (q_c * jnp.exp(_select_sum(sel_q, lf_parts))).astype(MXU_DTYPE)
        kd = (k_c * jnp.exp(_select_sum(sel_k, lf_parts))).astype(MXU_DTYPE)
        pair = (ref_of_row == ref_of_col) & row_upper & ((col % blk) < m)
        for h in range(heads):
            sl = slice(h * dk, (h + 1) * dk)
            acc[h] = acc[h] + jnp.where(pair, _dot_nt(qd[:, sl], kd[:, sl]), 0.0)
        m //= 2
    for h in range(heads):
        sc_ref[h] = acc[h].astype(MXU_DTYPE)


def _hgrn_kernel(x_ref, nw_ref, win_hbm, lbp_ref, gnw_ref, wout_hbm, fwi_f32, fwo_f32,
                 o_ref, worst_ref, fwi_ref, fwo_ref, st_ref, proj_ref, k_ref, sc_ref, y_ref,
                 win_ref, win_stage, win_sem, wout_ref, wout_stage, wout_sem,
                 *, layer, heads, dk, single_ref):
    kd_total = heads * dk
    tb = x_ref.shape[0]
    c = HGRN_CHUNK
    n_chunks = tb // c
    first = (pl.program_id(0) == 0) & (pl.program_id(1) == 0)

    @pl.when(pl.program_id(1) == 0)
    def _():
        st_ref[...] = jnp.zeros_like(st_ref)

    @pl.when(first)
    def _():
        worst_ref[...] = jnp.zeros_like(worst_ref)
        _load_weight(win_hbm.at[layer], win_ref, win_stage, win_sem)
        _load_weight(wout_hbm.at[layer], wout_ref, wout_stage, wout_sem)

    _cast_slices((fwi_f32, fwo_f32), (fwi_ref, fwo_ref))
    x = x_ref[...]
    hn = _rmsnorm(x, nw_ref[...]).astype(MXU_DTYPE)
    proj = _dot(hn, win_ref[...])

    lp = lbp_ref[...]
    e = jnp.exp(lp - jnp.max(lp, axis=0, keepdims=True))
    lb = jnp.sum(e[: layer + 1], axis=0, keepdims=True) / jnp.sum(e, axis=0, keepdims=True)
    one_m_lb = 1.0 - lb

    q = proj[:, :kd_total]
    fl = proj[:, kd_total:2 * kd_total]
    g = proj[:, 3 * kd_total:]
    en = jnp.exp(-jnp.abs(fl))
    rn = 1.0 / (1.0 + en)
    small = en * rn
    pos = fl >= 0.0
    kk = one_m_lb * jnp.where(pos, small, rn)
    f = lb + one_m_lb * jnp.where(pos, rn, small)
    log_f = jnp.maximum(jnp.log(f), HGRN_LOG_FLOOR)

    proj_ref[:, :kd_total] = q * _sigmoid(q)
    proj_ref[:, kd_total:2 * kd_total] = log_f
    proj_ref[:, 2 * kd_total:3 * kd_total] = proj[:, 2 * kd_total:3 * kd_total]
    proj_ref[:, 3 * kd_total:] = g * _sigmoid(g)
    k_ref[...] = kk

    row = lax.broadcasted_iota(jnp.int32, (c, c), 0)
    col = lax.broadcasted_iota(jnp.int32, (c, c), 1)
    tril = (col <= row).astype(MXU_DTYPE)
    gnw = gnw_ref[...]

    for ci in range(n_chunks):
        rows = slice(ci * c, (ci + 1) * c)
        q_c = proj_ref[rows, :kd_total]
        lf_parts = _split(proj_ref[rows, kd_total:2 * kd_total], 2 if single_ref else 3)
        k_c = k_ref[rows, :]
        b = _select_sum(tril, lf_parts)
        b_last = b[c - 1:c, :]
        dec = jnp.exp(b_last)
        qd = (q_c * jnp.exp(b)).astype(MXU_DTYPE)
        if single_ref:
            kd_f = k_c * jnp.exp(-b)
            _hgrn_scores_single_ref(qd, kd_f.astype(MXU_DTYPE), heads, dk, sc_ref)
            ke = (kd_f * dec).astype(MXU_DTYPE)
            folded = b_last
            width = kd_total
            while width > worst_ref.shape[1]:
                width //= 2
                folded = jnp.minimum(folded[:, :width], folded[:, width:])
            worst_ref[...] = jnp.minimum(worst_ref[...], folded)
        else:
            _hgrn_scores_halving(q_c, k_c, lf_parts, heads, dk, sc_ref)
            ke = (k_c * jnp.exp(b_last - b)).astype(MXU_DTYPE)
        for h in range(heads):
            sl = slice(h * dk, (h + 1) * dk)
            v_h = proj_ref[rows, 2 * kd_total + h * dk:2 * kd_total + (h + 1) * dk]
            st = st_ref[h]
            o = _dot(sc_ref[h], v_h.astype(MXU_DTYPE)) + _dot_nt(qd[:, sl], st.astype(MXU_DTYPE))
            st_ref[h] = st * dec[:, sl] + _dot(v_h.T.astype(MXU_DTYPE), ke[:, sl])
            o = o * lax.rsqrt(jnp.mean(o * o, axis=-1, keepdims=True) + EPS) * gnw
            sg = proj_ref[rows, 3 * kd_total + h * dk:3 * kd_total + (h + 1) * dk]
            y_ref[rows, sl] = (o * sg).astype(MXU_DTYPE)

    o_ref[...] = x + _dot(y_ref[...], wout_ref[...])


def _hgrn_mixer(x2, batch, norm_w, w_in_all, lb_param, gn_w, w_out_all, layer, ffn_w_in_all,
                ffn_w_out_all, ffn_layer, single_ref):
    t, d = x2.shape
    dv = gn_w.shape[-1]
    w_in_shape, w_out_shape = w_in_all.shape[1:], w_out_all.shape[1:]
    heads = w_out_shape[0] // dv
    kd_total = (w_in_shape[1] - 2 * heads * dv) // 2
    dk = kd_total // heads
    assert dk == dv and kd_total == heads * dv, "kernel assumes key dim == value dim"
    tb = HGRN_BLOCK
    spb = t // batch // tb
    assert spb * tb * batch == t and tb % HGRN_CHUNK == 0
    step_of = lambda b, s: b * spb + s
    row = pl.BlockSpec((tb, d), lambda b, s: (step_of(b, s), 0))
    cast = _FfnWeightCast(ffn_w_in_all, ffn_w_out_all, ffn_layer, batch * spb)
    return pl.pallas_call(
        functools.partial(_hgrn_kernel, layer=layer, heads=heads, dk=dk, single_ref=single_ref),
        grid=(batch, spb),
        in_specs=[row, _const_spec((1, d)), _HBM, _const_spec(lb_param.shape),
                  _const_spec((1, dv)), _HBM] + cast.in_specs(step_of),
        out_specs=[row, pl.BlockSpec((1, dk), lambda b, s: (0, 0))] + cast.out_specs(step_of),
        out_shape=[jax.ShapeDtypeStruct((t, d), jnp.float32),
                   jax.ShapeDtypeStruct((1, dk), jnp.float32)] + cast.out_shapes(),
        scratch_shapes=[
            pltpu.VMEM((heads, dv, dk), jnp.float32),
            pltpu.VMEM((tb, w_in_shape[1]), jnp.float32),
            pltpu.VMEM((tb, kd_total), jnp.float32),
            pltpu.VMEM((heads, HGRN_CHUNK, HGRN_CHUNK), MXU_DTYPE),
            pltpu.VMEM((tb, heads * dv), MXU_DTYPE),
        ] + _weight_scratch(*w_in_shape) + _weight_scratch(*w_out_shape),
        compiler_params=pltpu.CompilerParams(dimension_semantics=("arbitrary", "arbitrary"),
                                             vmem_limit_bytes=VMEM_LIMIT_BYTES),
        name="hgrn2_mixer" if single_ref else "hgrn2_mixer_any_decay",
    )(x2, norm_w.reshape(1, d), w_in_all, lb_param, gn_w.reshape(1, dv), w_out_all, *cast.arrays)


def _gelu_tanh(x):
    c = -2.0 * math.sqrt(2.0 / math.pi) * LOG2E
    return x * (1.0 / (1.0 + jnp.exp2(x * (c + (c * 0.044715) * (x * x)))))


def _scan_rows(a, b, n):
    rows = lax.broadcasted_iota(jnp.int32, a.shape, 0)
    d = 1
    while d < n:
        live = rows >= d
        a_sh = jnp.where(live, pltpu.roll(a, d, axis=0), 1.0)
        b_sh = jnp.where(live, pltpu.roll(b, d, axis=0), 0.0)
        b = a * b_sh + b
        a = a * a_sh
        d *= 2
    return a, b


def _lru_kernel(x_ref, nw_ref, win_hbm, cw_ref, cb_ref, wa_hbm, ba_ref, wx_hbm, bx_ref, lam_ref,
                wout_hbm, fwi_f32, fwo_f32, o_ref, fwi_ref, fwo_ref, h_ref, ext_ref, a_ref, b_ref, hs_ref,
                win_ref, win_stage, win_sem, wa_ref, wa_stage, wa_sem, wx_ref, wx_stage, wx_sem,
                wout_ref, wout_stage, wout_sem, *, layer, width, n_blocks, taps):
    tb = x_ref.shape[0]
    bw = width // n_blocks
    pad = SUBLANES
    n_groups = tb // SUBLANES

    @pl.when((pl.program_id(0) == 0) & (pl.program_id(1) == 0))
    def _():
        _load_weight(win_hbm.at[layer], win_ref, win_stage, win_sem)
        _load_weight(wa_hbm.at[layer], wa_ref, wa_stage, wa_sem)
        _load_weight(wx_hbm.at[layer], wx_ref, wx_stage, wx_sem)
        _load_weight(wout_hbm.at[layer], wout_ref, wout_stage, wout_sem)

    @pl.when(pl.program_id(1) == 0)
    def _():
        h_ref[...] = jnp.zeros_like(h_ref)
        ext_ref[:pad, :] = jnp.zeros((pad, width), jnp.float32)

    _cast_slices((fwi_f32, fwo_f32), (fwi_ref, fwo_ref))
    x = x_ref[...]
    hn = _rmsnorm(x, nw_ref[...]).astype(MXU_DTYPE)
    proj = _dot(hn, win_ref[...])
    y = _gelu_tanh(proj[:, :width])
    u = proj[:, width:]

    ext_ref[pad:, :] = u
    cw = cw_ref[...]
    uc = cb_ref[...] + u * cw[taps - 1:taps, :]
    for j in range(1, taps):
        uc = uc + ext_ref[pad - j:pad - j + tb, :] * cw[taps - 1 - j:taps - j, :]
    ext_ref[:pad, :] = u[tb - pad:, :]

    ra, rx = [], []
    for n in range(n_blocks):
        ub = uc[:, n * bw:(n + 1) * bw].astype(MXU_DTYPE)
        ra.append(_dot(ub, wa_ref[n * bw:(n + 1) * bw, :]))
        rx.append(_dot(ub, wx_ref[n * bw:(n + 1) * bw, :]))
    r = _sigmoid(jnp.concatenate(ra, axis=-1) + ba_ref[...])
    ig = _sigmoid(jnp.concatenate(rx, axis=-1) + bx_ref[...])

    nl = -lam_ref[...]
    softplus = jnp.maximum(nl, 0.0) + jnp.log1p(jnp.exp(-jnp.abs(nl)))
    rate = (-LRU_C) * softplus
    log_a = rate * r
    a = jnp.exp2((rate * LOG2E) * r)
    b_in = jnp.sqrt(-jnp.tanh(log_a) * (a * a + 1.0)) * (ig * uc)

    n_tiles = width // LANES
    for j in range(n_tiles):
        cols = slice(j * LANES, (j + 1) * LANES)
        a_ref[j] = a[:, cols]
        b_ref[j] = b_in[:, cols]
    group_row = lax.broadcasted_iota(jnp.int32, (n_groups, LANES), 0)
    for j in range(n_tiles):
        cols = slice(j * LANES, (j + 1) * LANES)
        step = lambda r: pl.ds(r, n_groups, stride=SUBLANES)
        a_cum = [a_ref[j, step(0), :]]
        h_loc = [b_ref[j, step(0), :]]
        for r in range(1, SUBLANES):
            a_r = a_ref[j, step(r), :]
            h_loc.append(a_r * h_loc[-1] + b_ref[j, step(r), :])
            a_cum.append(a_r * a_cum[-1])
        a_inc, b_inc = _scan_rows(a_cum[-1], h_loc[-1], n_groups)
        h_prev = h_ref[:, cols]
        h_end = a_inc * h_prev + b_inc
        h_in = jnp.where(group_row >= 1, pltpu.roll(h_end, 1, axis=0), h_prev)
        h_ref[:, cols] = h_end[n_groups - 1:, :]
        for r in range(SUBLANES):
            hs_ref[j, step(r), :] = h_loc[r] + a_cum[r] * h_in
    hs = jnp.concatenate([hs_ref[j] for j in range(n_tiles)], axis=-1)
    hy = (hs * y).astype(MXU_DTYPE)

    o_ref[...] = x + _dot(hy, wout_ref[...])


def _lru_mixer(x2, batch, norm_w, w_in_all, conv_w, conv_b, wa_all, ba, wx_all, bx, lam, w_out_all,
               layer, ffn_w_in_all, ffn_w_out_all, ffn_layer):
    t, d = x2.shape
    width = w_out_all.shape[1]
    n_blocks, bw = wa_all.shape[1], wa_all.shape[2]
    taps = conv_w.shape[0]
    wa_all = wa_all.reshape(wa_all.shape[0], n_blocks * bw, bw)
    wx_all = wx_all.reshape(wx_all.shape[0], n_blocks * bw, bw)
    tb = LRU_BLOCK
    spb = t // batch // tb
    assert spb * tb * batch == t and taps - 1 <= SUBLANES and tb % SUBLANES == 0
    step_of = lambda b, s: b * spb + s
    row = pl.BlockSpec((tb, d), lambda b, s: (step_of(b, s), 0))
    vec = lambda v: v.reshape(1, width)
    cast = _FfnWeightCast(ffn_w_in_all, ffn_w_out_all, ffn_layer, batch * spb)
    return pl.pallas_call(
        functools.partial(_lru_kernel, layer=layer, width=width, n_blocks=n_blocks, taps=taps),
        grid=(batch, spb),
        in_specs=[row, _const_spec((1, d)), _HBM, _const_spec(conv_w.shape),
                  _const_spec((1, width)), _HBM, _const_spec((1, width)),
                  _HBM, _const_spec((1, width)), _const_spec((1, width)), _HBM]
        + cast.in_specs(step_of),
        out_specs=[row] + cast.out_specs(step_of),
        out_shape=[jax.ShapeDtypeStruct((t, d), jnp.float32)] + cast.out_shapes(),
        scratch_shapes=[
            pltpu.VMEM((1, width), jnp.float32),
            pltpu.VMEM((tb + SUBLANES, width), jnp.float32),
            pltpu.VMEM((width // LANES, tb, LANES), jnp.float32),
            pltpu.VMEM((width // LANES, tb, LANES), jnp.float32),
            pltpu.VMEM((width // LANES, tb, LANES), jnp.float32),
        ] + _weight_scratch(*w_in_all.shape[1:]) + _weight_scratch(*wa_all.shape[1:])
        + _weight_scratch(*wx_all.shape[1:]) + _weight_scratch(*w_out_all.shape[1:]),
        compiler_params=pltpu.CompilerParams(dimension_semantics=("arbitrary", "arbitrary"),
                                             vmem_limit_bytes=VMEM_LIMIT_BYTES),
        name="rglru_mixer",
    )(x2, norm_w.reshape(1, d), w_in_all, conv_w, vec(conv_b), wa_all, vec(ba), wx_all, vec(bx),
      vec(lam), w_out_all, *cast.arrays)


def kernel(x, norm_mix, norm_ffn, norm_final, hgrn_w_in, hgrn_lb, hgrn_norm, hgrn_w_out, lru_w_in, lru_conv_w, lru_conv_b, lru_wa, lru_ba, lru_wx, lru_bx, lru_lambda, lru_w_out, ffn_w_in, ffn_w_out):
    batch, seq, d = x.shape
    depth = norm_mix.shape[0]
    n_mixers = 2

    def ffn(l, h, w_in, w_out):
        return _ffn(h, norm_ffn[l], w_in, w_out, norm_final, final_norm=(l == depth - 1))

    def layers_from(l, h):
        if l == depth:
            return h
        j = l // n_mixers
        if l % n_mixers == 1:
            mixed = _lru_mixer(h, batch, norm_mix[l], lru_w_in, lru_conv_w[j], lru_conv_b[j],
                               lru_wa, lru_ba[j], lru_wx, lru_bx[j], lru_lambda[j], lru_w_out, j,
                               ffn_w_in, ffn_w_out, l)
            return layers_from(l + 1, ffn(l, *mixed))
        hgrn = functools.partial(_hgrn_mixer, h, batch, norm_mix[l], hgrn_w_in, hgrn_lb,
                                 hgrn_norm[j], hgrn_w_out, j, ffn_w_in, ffn_w_out, l)
        mixed, worst, w_in, w_out = hgrn(single_ref=True)

        def any_decay():
            mixed, _, w_in, w_out = hgrn(single_ref=False)
            return layers_from(l + 1, ffn(l, mixed, w_in, w_out))

        return lax.cond(jnp.min(worst) >= -HGRN_SAFE_EXP,
                        lambda: layers_from(l + 1, ffn(l, mixed, w_in, w_out)), any_decay)

    return layers_from(0, x.reshape(batch * seq, d)).reshape(batch, seq, d)
```

```python
import functools
import math

import jax
import jax.numpy as jnp
from jax import lax
from jax.experimental import pallas as pl
from jax.experimental.pallas import tpu as pltpu

EPS = 1e-6
LRU_C = 8.0
LOG2E = math.log2(math.e)
MXU_DTYPE = jnp.bfloat16
SUBLANES = 8
LANES = 128
SCAN_RADIX = 4
SCAN_LEVELS = 4

HGRN_CHUNK = 128
HGRN_SAFE_EXP = 80.0
HGRN_LOG_FLOOR = -1e4
HGRN_BLOCK = 512
LRU_BLOCK = 512
FFN_BLOCK = 512
FFN_TILE = 256
VMEM_LIMIT_BYTES = 56 * 1024 * 1024
BF16_ROWS = 16
WEIGHT_CHUNK_BYTES = 2 * 1024 * 1024


def _dot(a, b):
    return jnp.dot(a, b, preferred_element_type=jnp.float32)


def _dot_nt(a, b):
    return lax.dot_general(a, b, (((1,), (1,)), ((), ())), preferred_element_type=jnp.float32)


def _rmsnorm(x, w):
    return x * lax.rsqrt(jnp.mean(x * x, axis=-1, keepdims=True) + EPS) * w


def _sigmoid(x):
    return 1.0 / (1.0 + jnp.exp2(x * (-LOG2E)))


def _split(x, terms):
    parts = []
    for i in range(terms):
        p = x.astype(MXU_DTYPE)
        parts.append(p)
        if i + 1 < terms:
            x = x - p.astype(jnp.float32)
    return parts


def _select_sum(sel, parts):
    out = _dot(sel, parts[0])
    for p in parts[1:]:
        out = out + _dot(sel, p)
    return out


def _const_spec(shape):
    return pl.BlockSpec(shape, lambda *_: (0,) * len(shape), pipeline_mode=pl.Buffered(1))


_HBM = pl.BlockSpec(memory_space=pl.ANY)


def _chunk_rows(rows, cols):
    r = max(BF16_ROWS, min(rows, WEIGHT_CHUNK_BYTES // (cols * 4)))
    while rows % r or r % BF16_ROWS:
        r -= 1
    return r


def _weight_scratch(rows, cols):
    return [pltpu.VMEM((rows, cols), MXU_DTYPE),
            pltpu.VMEM((2, _chunk_rows(rows, cols), cols), jnp.float32),
            pltpu.SemaphoreType.DMA((2,))]


def _load_weight(w_hbm, w_ref, stage_ref, sem_ref):
    rows = stage_ref.shape[1]
    n = w_hbm.shape[0] // rows

    def chunk_copy(k):
        return pltpu.make_async_copy(w_hbm.at[pl.ds(k * rows, rows)], stage_ref.at[k % 2], sem_ref.at[k % 2])

    chunk_copy(0).start()
    for k in range(n):
        if k + 1 < n:
            chunk_copy(k + 1).start()
        chunk_copy(k).wait()
        w_ref[k * rows:(k + 1) * rows, :] = stage_ref[k % 2].astype(MXU_DTYPE)


def _ffn_kernel(x_ref, nw_ref, wi_ref, wo_ref, fnw_ref, o_ref, *, d_ff, final_norm):
    x = x_ref[...]
    hn = _rmsnorm(x, nw_ref[...]).astype(MXU_DTYPE)
    acc = x
    for j in range(d_ff // FFN_TILE):
        lo, hi = j * FFN_TILE, (j + 1) * FFN_TILE
        g = _dot(hn, wi_ref[:, lo:hi])
        u = _dot(hn, wi_ref[:, d_ff + lo:d_ff + hi])
        a = (g * _sigmoid(g) * u).astype(MXU_DTYPE)
        acc = acc + _dot(a, wo_ref[lo:hi, :])
    if final_norm:
        acc = _rmsnorm(acc, fnw_ref[...])
    o_ref[...] = acc


def _ffn(x2, norm_w, w_in, w_out, final_norm_w, final_norm):
    t, d = x2.shape
    d_ff = w_out.shape[0]
    assert d_ff % FFN_TILE == 0 and t % FFN_BLOCK == 0
    row = pl.BlockSpec((FFN_BLOCK, d), lambda i: (i, 0))
    return pl.pallas_call(
        functools.partial(_ffn_kernel, d_ff=d_ff, final_norm=final_norm),
        grid=(t // FFN_BLOCK,),
        in_specs=[row, _const_spec((1, d)), _const_spec(w_in.shape), _const_spec(w_out.shape),
                  _const_spec((1, d))],
        out_specs=row,
        out_shape=jax.ShapeDtypeStruct((t, d), jnp.float32),
        compiler_params=pltpu.CompilerParams(dimension_semantics=("arbitrary",),
                                             vmem_limit_bytes=VMEM_LIMIT_BYTES),
        name="swiglu_ffn",
    )(x2, norm_w.reshape(1, d), w_in, w_out, final_norm_w.reshape(1, d))


class _FfnWeightCast:
    def __init__(self, w_in_all, w_out_all, layer, n_steps):
        self.arrays = [w_in_all, w_out_all]
        self.layer = layer
        self.plan = []
        for w in self.arrays:
            rows, share = w.shape[1], 1
            while (rows * share) % (n_steps * BF16_ROWS):
                share *= 2
                assert share <= n_steps, "weight rows do not split into packed-row blocks"
            self.plan.append((rows * share // n_steps, share))

    def in_specs(self, step_of):
        return [pl.BlockSpec((None, r, w.shape[2]), lambda *g, _k=k: (self.layer, step_of(*g) // _k, 0))
                for w, (r, k) in zip(self.arrays, self.plan)]

    def out_specs(self, step_of):
        return [pl.BlockSpec((r, w.shape[2]), lambda *g, _k=k: (step_of(*g) // _k, 0))
                for w, (r, k) in zip(self.arrays, self.plan)]

    def out_shapes(self):
        return [jax.ShapeDtypeStruct(w.shape[1:], MXU_DTYPE) for w in self.arrays]


def _cast_slices(in_refs, out_refs):
    for i_ref, o_ref in zip(in_refs, out_refs):
        o_ref[...] = i_ref[...].astype(MXU_DTYPE)


def _hgrn_scores_single_ref(qd, kd, heads, dk, sc_ref):
    c = qd.shape[0]
    row = lax.broadcasted_iota(jnp.int32, (c, c), 0)
    col = lax.broadcasted_iota(jnp.int32, (c, c), 1)
    causal = col <= row
    for h in range(heads):
        sl = slice(h * dk, (h + 1) * dk)
        s = _dot_nt(qd[:, sl], kd[:, sl])
        sc_ref[h] = jnp.where(causal, s, 0.0).astype(MXU_DTYPE)


def _hgrn_scores_halving(q_c, k_c, lf_parts, heads, dk, sc_ref):
    c = q_c.shape[0]
    row = lax.broadcasted_iota(jnp.int32, (c, c), 0)
    col = lax.broadcasted_iota(jnp.int32, (c, c), 1)
    acc = []
    for h in range(heads):
        sl = slice(h * dk, (h + 1) * dk)
        s = _dot_nt(q_c[:, sl].astype(MXU_DTYPE), k_c[:, sl].astype(MXU_DTYPE))
        acc.append(jnp.where(row == col, s, 0.0))
    m = c // 2
    while m >= 1:
        blk = 2 * m
        ref_of_row = (row // blk) * blk + (m - 1)
        ref_of_col = (col // blk) * blk + (m - 1)
        row_upper = (row % blk) >= m
        sel_q = (row_upper & (col > ref_of_row) & (col <= row)).astype(MXU_DTYPE)
        sel_k = ((~row_upper) & (col > row) & (col <= ref_of_row)).astype(MXU_DTYPE)
        qd = (q_c * jnp.exp(_select_sum(sel_q, lf_parts))).astype(MXU_DTYPE)
        kd = (k_c * jnp.exp(_select_sum(sel_k, lf_parts))).astype(MXU_DTYPE)
        pair = (ref_of_row == ref_of_col) & row_upper & ((col % blk) < m)
        for h in range(heads):
            sl = slice(h * dk, (h + 1) * dk)
            acc[h] = acc[h] + jnp.where(pair, _dot_nt(qd[:, sl], kd[:, sl]), 0.0)
        m //= 2
    for h in range(heads):
        sc_ref[h] = acc[h].astype(MXU_DTYPE)


def _hgrn_kernel(x_ref, nw_ref, win_hbm, lbp_ref, gnw_ref, wout_hbm, fwi_f32, fwo_f32,
                 o_ref, worst_ref, fwi_ref, fwo_ref, st_ref, proj_ref, k_ref, sc_ref, y_ref,
                 win_ref, win_stage, win_sem, wout_ref, wout_stage, wout_sem,
                 *, layer, heads, dk, single_ref):
    kd_total = heads * dk
    tb = x_ref.shape[0]
    c = HGRN_CHUNK
    n_chunks = tb // c
    first = (pl.program_id(0) == 0) & (pl.program_id(1) == 0)

    @pl.when(pl.program_id(1) == 0)
    def _():
        st_ref[...] = jnp.zeros_like(st_ref)

    @pl.when(first)
    def _():
        worst_ref[...] = jnp.zeros_like(worst_ref)
        _load_weight(win_hbm.at[layer], win_ref, win_stage, win_sem)
        _load_weight(wout_hbm.at[layer], wout_ref, wout_stage, wout_sem)

    _cast_slices((fwi_f32, fwo_f32), (fwi_ref, fwo_ref))
    x = x_ref[...]
    hn = _rmsnorm(x, nw_ref[...]).astype(MXU_DTYPE)
    proj = _dot(hn, win_ref[...])

    lp = lbp_ref[...]
    e = jnp.exp(lp - jnp.max(lp, axis=0, keepdims=True))
    lb = jnp.sum(e[: layer + 1], axis=0, keepdims=True) / jnp.sum(e, axis=0, keepdims=True)
    one_m_lb = 1.0 - lb

    q = proj[:, :kd_total]
    fl = proj[:, kd_total:2 * kd_total]
    g = proj[:, 3 * kd_total:]
    en = jnp.exp(-jnp.abs(fl))
    rn = 1.0 / (1.0 + en)
    small = en * rn
    pos = fl >= 0.0
    kk = one_m_lb * jnp.where(pos, small, rn)
    f = lb + one_m_lb * jnp.where(pos, rn, small)
    log_f = jnp.maximum(jnp.log(f), HGRN_LOG_FLOOR)

    proj_ref[:, :kd_total] = q * _sigmoid(q)
    proj_ref[:, kd_total:2 * kd_total] = log_f
    proj_ref[:, 2 * kd_total:3 * kd_total] = proj[:, 2 * kd_total:3 * kd_total]
    proj_ref[:, 3 * kd_total:] = g * _sigmoid(g)
    k_ref[...] = kk

    row = lax.broadcasted_iota(jnp.int32, (c, c), 0)
    col = lax.broadcasted_iota(jnp.int32, (c, c), 1)
    tril = (col <= row).astype(MXU_DTYPE)
    gnw = gnw_ref[...]

    for ci in range(n_chunks):
        rows = slice(ci * c, (ci + 1) * c)
        q_c = proj_ref[rows, :kd_total]
        lf_parts = _split(proj_ref[rows, kd_total:2 * kd_total], 2 if single_ref else 3)
        k_c = k_ref[rows, :]
        b = _select_sum(tril, lf_parts)
        b_last = b[c - 1:c, :]
        dec = jnp.exp(b_last)
        eb = jnp.exp(b)
        qd = (q_c * eb).astype(MXU_DTYPE)
        if single_ref:
            kd = (k_c * (1.0 / eb)).astype(MXU_DTYPE)
            _hgrn_scores_single_ref(qd, kd, heads, dk, sc_ref)
            ke = kd * dec.astype(MXU_DTYPE)
            folded = b_last
            width = kd_total
            while width > worst_ref.shape[1]:
                width //= 2
                folded = jnp.minimum(folded[:, :width], folded[:, width:])
            worst_ref[...] = jnp.minimum(worst_ref[...], folded)
        else:
            _hgrn_scores_halving(q_c, k_c, lf_parts, heads, dk, sc_ref)
            ke = (k_c * jnp.exp(b_last - b)).astype(MXU_DTYPE)
        for h in range(heads):
            sl = slice(h * dk, (h + 1) * dk)
            v_h = proj_ref[rows, 2 * kd_total + h * dk:2 * kd_total + (h + 1) * dk]
            st = st_ref[h]
            o = _dot(sc_ref[h], v_h.astype(MXU_DTYPE)) + _dot_nt(qd[:, sl], st.astype(MXU_DTYPE))
            st_ref[h] = st * dec[:, sl] + _dot(v_h.T.astype(MXU_DTYPE), ke[:, sl])
            o = o * lax.rsqrt(jnp.mean(o * o, axis=-1, keepdims=True) + EPS) * gnw
            sg = proj_ref[rows, 3 * kd_total + h * dk:3 * kd_total + (h + 1) * dk]
            y_ref[rows, sl] = (o * sg).astype(MXU_DTYPE)

    o_ref[...] = x + _dot(y_ref[...], wout_ref[...])


def _hgrn_mixer(x2, batch, norm_w, w_in_all, lb_param, gn_w, w_out_all, layer, ffn_w_in_all,
                ffn_w_out_all, ffn_layer, single_ref):
    t, d = x2.shape
    dv = gn_w.shape[-1]
    w_in_shape, w_out_shape = w_in_all.shape[1:], w_out_all.shape[1:]
    heads = w_out_shape[0] // dv
    kd_total = (w_in_shape[1] - 2 * heads * dv) // 2
    dk = kd_total // heads
    assert dk == dv and kd_total == heads * dv, "kernel assumes key dim == value dim"
    tb = HGRN_BLOCK
    spb = t // batch // tb
    assert spb * tb * batch == t and tb % HGRN_CHUNK == 0
    step_of = lambda b, s: b * spb + s
    row = pl.BlockSpec((tb, d), lambda b, s: (step_of(b, s), 0))
    cast = _FfnWeightCast(ffn_w_in_all, ffn_w_out_all, ffn_layer, batch * spb)
    return pl.pallas_call(
        functools.partial(_hgrn_kernel, layer=layer, heads=heads, dk=dk, single_ref=single_ref),
        grid=(batch, spb),
        in_specs=[row, _const_spec((1, d)), _HBM, _const_spec(lb_param.shape),
                  _const_spec((1, dv)), _HBM] + cast.in_specs(step_of),
        out_specs=[row, pl.BlockSpec((1, dk), lambda b, s: (0, 0))] + cast.out_specs(step_of),
        out_shape=[jax.ShapeDtypeStruct((t, d), jnp.float32),
                   jax.ShapeDtypeStruct((1, dk), jnp.float32)] + cast.out_shapes(),
        scratch_shapes=[
            pltpu.VMEM((heads, dv, dk), jnp.float32),
            pltpu.VMEM((tb, w_in_shape[1]), jnp.float32),
            pltpu.VMEM((tb, kd_total), jnp.float32),
            pltpu.VMEM((heads, HGRN_CHUNK, HGRN_CHUNK), MXU_DTYPE),
            pltpu.VMEM((tb, heads * dv), MXU_DTYPE),
        ] + _weight_scratch(*w_in_shape) + _weight_scratch(*w_out_shape),
        compiler_params=pltpu.CompilerParams(dimension_semantics=("arbitrary", "arbitrary"),
                                             vmem_limit_bytes=VMEM_LIMIT_BYTES),
        name="hgrn2_mixer" if single_ref else "hgrn2_mixer_any_decay",
    )(x2, norm_w.reshape(1, d), w_in_all, lb_param, gn_w.reshape(1, dv), w_out_all, *cast.arrays)


def _gelu_tanh(x):
    c = -2.0 * math.sqrt(2.0 / math.pi) * LOG2E
    return x * (1.0 / (1.0 + jnp.exp2(x * (c + (c * 0.044715) * (x * x)))))


def _scan_rows(a, b, n):
    rows = lax.broadcasted_iota(jnp.int32, a.shape, 0)
    d = 1
    while d < n:
        live = rows >= d
        a_sh = jnp.where(live, pltpu.roll(a, d, axis=0), 1.0)
        b_sh = jnp.where(live, pltpu.roll(b, d, axis=0), 0.0)
        b = a * b_sh + b
        a = a * a_sh
        d *= 2
    return a, b


def _scan_level(levels, lvl, j, h0):
    a_ref, b_ref, h_ref = levels[lvl]
    n = a_ref.shape[1]
    if lvl + 1 == len(levels) or n % SCAN_RADIX or n <= SUBLANES:
        a_inc, b_inc = _scan_rows(a_ref[j], b_ref[j], n)
        h_ref[j] = a_inc * h0 + b_inc
        return
    m = n // SCAN_RADIX
    step = lambda r: pl.ds(r, m, stride=SCAN_RADIX)
    a_cum = [a_ref[j, step(0), :]]
    h_loc = [b_ref[j, step(0), :]]
    for r in range(1, SCAN_RADIX):
        a_r = a_ref[j, step(r), :]
        h_loc.append(a_r * h_loc[-1] + b_ref[j, step(r), :])
        a_cum.append(a_r * a_cum[-1])
    a_next, b_next, h_next = levels[lvl + 1]
    a_next[j] = a_cum[-1]
    b_next[j] = h_loc[-1]
    _scan_level(levels, lvl + 1, j, h0)
    group = lax.broadcasted_iota(jnp.int32, (m, LANES), 0)
    h_in = jnp.where(group >= 1, pltpu.roll(h_next[j], 1, axis=0), h0)
    for r in range(SCAN_RADIX):
        h_ref[j, step(r), :] = h_loc[r] + a_cum[r] * h_in


def _lru_kernel(x_ref, nw_ref, win_hbm, cw_ref, cb_ref, wa_hbm, ba_ref, wx_hbm, bx_ref, lam_ref,
                wout_hbm, fwi_f32, fwo_f32, o_ref, fwi_ref, fwo_ref, h_ref, ext_ref, a_ref, b_ref, hs_ref,
                a1_ref, b1_ref, h1_ref, a2_ref, b2_ref, h2_ref, a3_ref, b3_ref, h3_ref,
                win_ref, win_stage, win_sem, wa_ref, wa_stage, wa_sem, wx_ref, wx_stage, wx_sem,
                wout_ref, wout_stage, wout_sem, *, layer, width, n_blocks, taps):
    tb = x_ref.shape[0]
    bw = width // n_blocks
    pad = SUBLANES

    @pl.when((pl.program_id(0) == 0) & (pl.program_id(1) == 0))
    def _():
        _load_weight(win_hbm.at[layer], win_ref, win_stage, win_sem)
        _load_weight(wa_hbm.at[layer], wa_ref, wa_stage, wa_sem)
        _load_weight(wx_hbm.at[layer], wx_ref, wx_stage, wx_sem)
        _load_weight(wout_hbm.at[layer], wout_ref, wout_stage, wout_sem)

    @pl.when(pl.program_id(1) == 0)
    def _():
        h_ref[...] = jnp.zeros_like(h_ref)
        ext_ref[:pad, :] = jnp.zeros((pad, width), jnp.float32)

    _cast_slices((fwi_f32, fwo_f32), (fwi_ref, fwo_ref))
    x = x_ref[...]
    hn = _rmsnorm(x, nw_ref[...]).astype(MXU_DTYPE)
    proj = _dot(hn, win_ref[...])
    y = _gelu_tanh(proj[:, :width])
    u = proj[:, width:]

    ext_ref[pad:, :] = u
    cw = cw_ref[...]
    uc = cb_ref[...] + u * cw[taps - 1:taps, :]
    for j in range(1, taps):
        uc = uc + ext_ref[pad - j:pad - j + tb, :] * cw[taps - 1 - j:taps - j, :]
    ext_ref[:pad, :] = u[tb - pad:, :]

    ra, rx = [], []
    for n in range(n_blocks):
        ub = uc[:, n * bw:(n + 1) * bw].astype(MXU_DTYPE)
        ra.append(_dot(ub, wa_ref[n * bw:(n + 1) * bw, :]))
        rx.append(_dot(ub, wx_ref[n * bw:(n + 1) * bw, :]))
    r = _sigmoid(jnp.concatenate(ra, axis=-1) + ba_ref[...])
    ig = _sigmoid(jnp.concatenate(rx, axis=-1) + bx_ref[...])

    nl = -lam_ref[...]
    softplus = jnp.maximum(nl, 0.0) + jnp.log1p(jnp.exp(-jnp.abs(nl)))
    rate = (-LRU_C) * softplus
    log_a = rate * r
    a = jnp.exp2((rate * LOG2E) * r)
    z = -jnp.tanh(log_a) * (a * a + 1.0)
    b_in = jnp.where(z > 0.0, z * lax.rsqrt(z), 0.0) * (ig * uc)

    levels = [(a_ref, b_ref, hs_ref), (a1_ref, b1_ref, h1_ref), (a2_ref, b2_ref, h2_ref),
              (a3_ref, b3_ref, h3_ref)]
    n_tiles = width // LANES
    for j in range(n_tiles):
        cols = slice(j * LANES, (j + 1) * LANES)
        a_ref[j] = a[:, cols]
        b_ref[j] = b_in[:, cols]
    for j in range(n_tiles):
        cols = slice(j * LANES, (j + 1) * LANES)
        _scan_level(levels, 0, j, h_ref[:, cols])
        h_ref[:, cols] = hs_ref[j, tb - 1:tb, :]
    hs = jnp.concatenate([hs_ref[j] for j in range(n_tiles)], axis=-1)
    hy = (hs * y).astype(MXU_DTYPE)

    o_ref[...] = x + _dot(hy, wout_ref[...])


def _lru_mixer(x2, batch, norm_w, w_in_all, conv_w, conv_b, wa_all, ba, wx_all, bx, lam, w_out_all,
               layer, ffn_w_in_all, ffn_w_out_all, ffn_layer):
    t, d = x2.shape
    width = w_out_all.shape[1]
    n_blocks, bw = wa_all.shape[1], wa_all.shape[2]
    taps = conv_w.shape[0]
    wa_all = wa_all.reshape(wa_all.shape[0], n_blocks * bw, bw)
    wx_all = wx_all.reshape(wx_all.shape[0], n_blocks * bw, bw)
    tb = LRU_BLOCK
    spb = t // batch // tb
    assert spb * tb * batch == t and taps - 1 <= SUBLANES and tb % SUBLANES == 0
    step_of = lambda b, s: b * spb + s
    row = pl.BlockSpec((tb, d), lambda b, s: (step_of(b, s), 0))
    vec = lambda v: v.reshape(1, width)
    cast = _FfnWeightCast(ffn_w_in_all, ffn_w_out_all, ffn_layer, batch * spb)
    return pl.pallas_call(
        functools.partial(_lru_kernel, layer=layer, width=width, n_blocks=n_blocks, taps=taps),
        grid=(batch, spb),
        in_specs=[row, _const_spec((1, d)), _HBM, _const_spec(conv_w.shape),
                  _const_spec((1, width)), _HBM, _const_spec((1, width)),
                  _HBM, _const_spec((1, width)), _const_spec((1, width)), _HBM]
        + cast.in_specs(step_of),
        out_specs=[row] + cast.out_specs(step_of),
        out_shape=[jax.ShapeDtypeStruct((t, d), jnp.float32)] + cast.out_shapes(),
        scratch_shapes=[
            pltpu.VMEM((1, width), jnp.float32),
            pltpu.VMEM((tb + SUBLANES, width), jnp.float32),
        ] + [pltpu.VMEM((width // LANES, tb // SCAN_RADIX ** lvl, LANES), jnp.float32)
             for lvl in range(SCAN_LEVELS) for _ in "abh"]
        + _weight_scratch(*w_in_all.shape[1:]) + _weight_scratch(*wa_all.shape[1:])
        + _weight_scratch(*wx_all.shape[1:]) + _weight_scratch(*w_out_all.shape[1:]),
        compiler_params=pltpu.CompilerParams(dimension_semantics=("arbitrary", "arbitrary"),
                                             vmem_limit_bytes=VMEM_LIMIT_BYTES),
        name="rglru_mixer",
    )(x2, norm_w.reshape(1, d), w_in_all, conv_w, vec(conv_b), wa_all, vec(ba), wx_all, vec(bx),
      vec(lam), w_out_all, *cast.arrays)


def kernel(x, norm_mix, norm_ffn, norm_final, hgrn_w_in, hgrn_lb, hgrn_norm, hgrn_w_out, lru_w_in, lru_conv_w, lru_conv_b, lru_wa, lru_ba, lru_wx, lru_bx, lru_lambda, lru_w_out, ffn_w_in, ffn_w_out):
    batch, seq, d = x.shape
    depth = norm_mix.shape[0]
    n_mixers = 2

    def ffn(l, h, w_in, w_out):
        return _ffn(h, norm_ffn[l], w_in, w_out, norm_final, final_norm=(l == depth - 1))

    def layers_from(l, h):
        if l == depth:
            return h
        j = l // n_mixers
        if l % n_mixers == 1:
            mixed = _lru_mixer(h, batch, norm_mix[l], lru_w_in, lru_conv_w[j], lru_conv_b[j],
                               lru_wa, lru_ba[j], lru_wx, lru_bx[j], lru_lambda[j], lru_w_out, j,
                               ffn_w_in, ffn_w_out, l)
            return layers_from(l + 1, ffn(l, *mixed))
        hgrn = functools.partial(_hgrn_mixer, h, batch, norm_mix[l], hgrn_w_in, hgrn_lb,
                                 hgrn_norm[j], hgrn_w_out, j, ffn_w_in, ffn_w_out, l)
        mixed, worst, w_in, w_out = hgrn(single_ref=True)

        def any_decay():
            mixed, _, w_in, w_out = hgrn(single_ref=False)
            return layers_from(l + 1, ffn(l, mixed, w_in, w_out))

        return lax.cond(jnp.min(worst) >= -HGRN_SAFE_EXP,
                        lambda: layers_from(l + 1, ffn(l, mixed, w_in, w_out)), any_decay)

    return layers_from(0, x.reshape(batch * seq, d)).reshape(batch, seq, d)
```

```python
import functools
import math

import jax
import jax.numpy as jnp
from jax import lax
from jax.experimental import pallas as pl
from jax.experimental.pallas import tpu as pltpu

EPS = 1e-6
LRU_C = 8.0
LOG2E = math.log2(math.e)
MXU_DTYPE = jnp.bfloat16
SUBLANES = 8
LANES = 128
SCAN_RADIX = 4
SCAN_LEVELS = 4

HGRN_CHUNK = 128
HGRN_SAFE_EXP = 80.0
HGRN_LOG_FLOOR = -1e4
HGRN_BLOCK = 512
LRU_BLOCK = 512
FFN_BLOCK = 512
FFN_TILE = 256
VMEM_LIMIT_BYTES = 56 * 1024 * 1024
BF16_ROWS = 16
WEIGHT_CHUNK_BYTES = 2 * 1024 * 1024


def _dot(a, b):
    return jnp.dot(a, b, preferred_element_type=jnp.float32)


def _dot_nt(a, b):
    return lax.dot_general(a, b, (((1,), (1,)), ((), ())), preferred_element_type=jnp.float32)


def _rmsnorm(x, w):
    return x * lax.rsqrt(jnp.mean(x * x, axis=-1, keepdims=True) + EPS) * w


def _sigmoid(x):
    return 1.0 / (1.0 + jnp.exp2(x * (-LOG2E)))


def _split(x, terms):
    parts = []
    for i in range(terms):
        p = x.astype(MXU_DTYPE)
        parts.append(p)
        if i + 1 < terms:
            x = x - p.astype(jnp.float32)
    return parts


def _select_sum(sel, parts):
    out = _dot(sel, parts[0])
    for p in parts[1:]:
        out = out + _dot(sel, p)
    return out


def _const_spec(shape):
    return pl.BlockSpec(shape, lambda *_: (0,) * len(shape), pipeline_mode=pl.Buffered(1))


_HBM = pl.BlockSpec(memory_space=pl.ANY)


def _chunk_rows(rows, cols):
    r = max(BF16_ROWS, min(rows, WEIGHT_CHUNK_BYTES // (cols * 4)))
    while rows % r or r % BF16_ROWS:
        r -= 1
    return r


def _weight_scratch(rows, cols):
    return [pltpu.VMEM((rows, cols), MXU_DTYPE),
            pltpu.VMEM((2, _chunk_rows(rows, cols), cols), jnp.float32),
            pltpu.SemaphoreType.DMA((2,))]


def _load_weight(w_hbm, w_ref, stage_ref, sem_ref):
    rows = stage_ref.shape[1]
    n = w_hbm.shape[0] // rows

    def chunk_copy(k):
        return pltpu.make_async_copy(w_hbm.at[pl.ds(k * rows, rows)], stage_ref.at[k % 2], sem_ref.at[k % 2])

    chunk_copy(0).start()
    for k in range(n):
        if k + 1 < n:
            chunk_copy(k + 1).start()
        chunk_copy(k).wait()
        w_ref[k * rows:(k + 1) * rows, :] = stage_ref[k % 2].astype(MXU_DTYPE)


def _ffn_kernel(x_ref, nw_ref, wi_ref, wo_ref, fnw_ref, o_ref, *, d_ff, final_norm):
    x = x_ref[...]
    hn = _rmsnorm(x, nw_ref[...]).astype(MXU_DTYPE)
    acc = x
    for j in range(d_ff // FFN_TILE):
        lo, hi = j * FFN_TILE, (j + 1) * FFN_TILE
        g = _dot(hn, wi_ref[:, lo:hi])
        u = _dot(hn, wi_ref[:, d_ff + lo:d_ff + hi])
        a = (g * _sigmoid(g) * u).astype(MXU_DTYPE)
        acc = acc + _dot(a, wo_ref[lo:hi, :])
    if final_norm:
        acc = _rmsnorm(acc, fnw_ref[...])
    o_ref[...] = acc


def _ffn(x2, norm_w, w_in, w_out, final_norm_w, final_norm):
    t, d = x2.shape
    d_ff = w_out.shape[0]
    assert d_ff % FFN_TILE == 0 and t % FFN_BLOCK == 0
    row = pl.BlockSpec((FFN_BLOCK, d), lambda i: (i, 0))
    return pl.pallas_call(
        functools.partial(_ffn_kernel, d_ff=d_ff, final_norm=final_norm),
        grid=(t // FFN_BLOCK,),
        in_specs=[row, _const_spec((1, d)), _const_spec(w_in.shape), _const_spec(w_out.shape),
                  _const_spec((1, d))],
        out_specs=row,
        out_shape=jax.ShapeDtypeStruct((t, d), jnp.float32),
        compiler_params=pltpu.CompilerParams(dimension_semantics=("arbitrary",),
                                             vmem_limit_bytes=VMEM_LIMIT_BYTES),
        name="swiglu_ffn",
    )(x2, norm_w.reshape(1, d), w_in, w_out, final_norm_w.reshape(1, d))


class _FfnWeightCast:
    def __init__(self, w_in_all, w_out_all, layer, n_steps):
        self.arrays = [w_in_all, w_out_all]
        self.layer = layer
        self.plan = []
        for w in self.arrays:
            rows, share = w.shape[1], 1
            while (rows * share) % (n_steps * BF16_ROWS):
                share *= 2
                assert share <= n_steps, "weight rows do not split into packed-row blocks"
            self.plan.append((rows * share // n_steps, share))

    def in_specs(self, step_of):
        return [pl.BlockSpec((None, r, w.shape[2]), lambda *g, _k=k: (self.layer, step_of(*g) // _k, 0))
                for w, (r, k) in zip(self.arrays, self.plan)]

    def out_specs(self, step_of):
        return [pl.BlockSpec((r, w.shape[2]), lambda *g, _k=k: (step_of(*g) // _k, 0))
                for w, (r, k) in zip(self.arrays, self.plan)]

    def out_shapes(self):
        return [jax.ShapeDtypeStruct(w.shape[1:], MXU_DTYPE) for w in self.arrays]


def _cast_slices(in_refs, out_refs):
    for i_ref, o_ref in zip(in_refs, out_refs):
        o_ref[...] = i_ref[...].astype(MXU_DTYPE)


def _hgrn_scores_single_ref(qd, kd, heads, dk, sc_ref):
    c = qd.shape[0]
    row = lax.broadcasted_iota(jnp.int32, (c, c), 0)
    col = lax.broadcasted_iota(jnp.int32, (c, c), 1)
    causal = col <= row
    for h in range(heads):
        sl = slice(h * dk, (h + 1) * dk)
        s = _dot_nt(qd[:, sl], kd[:, sl])
        sc_ref[h] = jnp.where(causal, s, 0.0).astype(MXU_DTYPE)


def _hgrn_scores_halving(q_c, k_c, lf_parts, heads, dk, sc_ref):
    c = q_c.shape[0]
    row = lax.broadcasted_iota(jnp.int32, (c, c), 0)
    col = lax.broadcasted_iota(jnp.int32, (c, c), 1)
    acc = []
    for h in range(heads):
        sl = slice(h * dk, (h + 1) * dk)
        s = _dot_nt(q_c[:, sl].astype(MXU_DTYPE), k_c[:, sl].astype(MXU_DTYPE))
        acc.append(jnp.where(row == col, s, 0.0))
    m = c // 2
    while m >= 1:
        blk = 2 * m
        ref_of_row = (row // blk) * blk + (m - 1)
        ref_of_col = (col // blk) * blk + (m - 1)
        row_upper = (row % blk) >= m
        sel_q = (row_upper & (col > ref_of_row) & (col <= row)).astype(MXU_DTYPE)
        sel_k = ((~row_upper) & (col > row) & (col <= ref_of_row)).astype(MXU_DTYPE)
        qd = (q_c * jnp.exp(_select_sum(sel_q, lf_parts))).astype(MXU_DTYPE)
        kd = (k_c * jnp.exp(_select_sum(sel_k, lf_parts))).astype(MXU_DTYPE)
        pair = (ref_of_row == ref_of_col) & row_upper & ((col % blk) < m)
        for h in range(heads):
            sl = slice(h * dk, (h + 1) * dk)
            acc[h] = acc[h] + jnp.where(pair, _dot_nt(qd[:, sl], kd[:, sl]), 0.0)
        m //= 2
    for h in range(heads):
        sc_ref[h] = acc[h].astype(MXU_DTYPE)


def _hgrn_kernel(x_ref, nw_ref, win_hbm, lbp_ref, gnw_ref, wout_hbm, fwi_f32, fwo_f32,
                 o_ref, worst_ref, fwi_ref, fwo_ref, st_ref, proj_ref, k_ref, sc_ref, y_ref,
                 win_ref, win_stage, win_sem, wout_ref, wout_stage, wout_sem,
                 *, layer, heads, dk, single_ref):
    kd_total = heads * dk
    tb = x_ref.shape[0]
    c = HGRN_CHUNK
    n_chunks = tb // c
    first = (pl.program_id(0) == 0) & (pl.program_id(1) == 0)

    @pl.when(pl.program_id(1) == 0)
    def _():
        st_ref[...] = jnp.zeros_like(st_ref)

    @pl.when(first)
    def _():
        worst_ref[...] = jnp.zeros_like(worst_ref)
        _load_weight(win_hbm.at[layer], win_ref, win_stage, win_sem)
        _load_weight(wout_hbm.at[layer], wout_ref, wout_stage, wout_sem)

    _cast_slices((fwi_f32, fwo_f32), (fwi_ref, fwo_ref))
    x = x_ref[...]
    hn = _rmsnorm(x, nw_ref[...]).astype(MXU_DTYPE)
    proj = _dot(hn, win_ref[...])

    lp = lbp_ref[...]
    e = jnp.exp(lp - jnp.max(lp, axis=0, keepdims=True))
    lb = jnp.sum(e[: layer + 1], axis=0, keepdims=True) / jnp.sum(e, axis=0, keepdims=True)
    one_m_lb = 1.0 - lb

    q = proj[:, :kd_total]
    fl = proj[:, kd_total:2 * kd_total]
    g = proj[:, 3 * kd_total:]
    en = jnp.exp(-jnp.abs(fl))
    rn = 1.0 / (1.0 + en)
    small = en * rn
    pos = fl >= 0.0
    kk = one_m_lb * jnp.where(pos, small, rn)
    f = lb + one_m_lb * jnp.where(pos, rn, small)
    log_f = jnp.maximum(jnp.log(f), HGRN_LOG_FLOOR)

    proj_ref[:, :kd_total] = q * _sigmoid(q)
    proj_ref[:, kd_total:2 * kd_total] = log_f
    proj_ref[:, 2 * kd_total:3 * kd_total] = proj[:, 2 * kd_total:3 * kd_total]
    proj_ref[:, 3 * kd_total:] = g * _sigmoid(g)
    k_ref[...] = kk

    row = lax.broadcasted_iota(jnp.int32, (c, c), 0)
    col = lax.broadcasted_iota(jnp.int32, (c, c), 1)
    tril = (col <= row).astype(MXU_DTYPE)
    gnw = gnw_ref[...]

    for ci in range(n_chunks):
        rows = slice(ci * c, (ci + 1) * c)
        q_c = proj_ref[rows, :kd_total]
        lf_parts = _split(proj_ref[rows, kd_total:2 * kd_total], 2 if single_ref else 3)
        k_c = k_ref[rows, :]
        b = _select_sum(tril, lf_parts)
        b_last = b[c - 1:c, :]
        dec = jnp.exp(b_last)
        eb = jnp.exp(b)
        qd = (q_c * eb).astype(MXU_DTYPE)
        if single_ref:
            kd = (k_c * (1.0 / eb)).astype(MXU_DTYPE)
            _hgrn_scores_single_ref(qd, kd, heads, dk, sc_ref)
            ke = kd * dec.astype(MXU_DTYPE)
            folded = b_last
            width = kd_total
            while width > worst_ref.shape[1]:
                width //= 2
                folded = jnp.minimum(folded[:, :width], folded[:, width:])
            worst_ref[...] = jnp.minimum(worst_ref[...], folded)
        else:
            _hgrn_scores_halving(q_c, k_c, lf_parts, heads, dk, sc_ref)
            ke = (k_c * jnp.exp(b_last - b)).astype(MXU_DTYPE)
        for h in range(heads):
            sl = slice(h * dk, (h + 1) * dk)
            v_h = proj_ref[rows, 2 * kd_total + h * dk:2 * kd_total + (h + 1) * dk]
            st = st_ref[h]
            o = _dot(sc_ref[h], v_h.astype(MXU_DTYPE)) + _dot_nt(qd[:, sl], st.astype(MXU_DTYPE))
            st_ref[h] = st * dec[:, sl] + _dot(v_h.T.astype(MXU_DTYPE), ke[:, sl])
            o = o * lax.rsqrt(jnp.mean(o * o, axis=-1, keepdims=True) + EPS) * gnw
            sg = proj_ref[rows, 3 * kd_total + h * dk:3 * kd_total + (h + 1) * dk]
            y_ref[rows, sl] = (o * sg).astype(MXU_DTYPE)

    o_ref[...] = x + _dot(y_ref[...], wout_ref[...])


def _hgrn_mixer(x2, batch, norm_w, w_in_all, lb_param, gn_w, w_out_all, layer, ffn_w_in_all,
                ffn_w_out_all, ffn_layer, single_ref):
    t, d = x2.shape
    dv = gn_w.shape[-1]
    w_in_shape, w_out_shape = w_in_all.shape[1:], w_out_all.shape[1:]
    heads = w_out_shape[0] // dv
    kd_total = (w_in_shape[1] - 2 * heads * dv) // 2
    dk = kd_total // heads
    assert dk == dv and kd_total == heads * dv, "kernel assumes key dim == value dim"
    tb = HGRN_BLOCK
    spb = t // batch // tb
    assert spb * tb * batch == t and tb % HGRN_CHUNK == 0
    step_of = lambda b, s: b * spb + s
    row = pl.BlockSpec((tb, d), lambda b, s: (step_of(b, s), 0))
    cast = _FfnWeightCast(ffn_w_in_all, ffn_w_out_all, ffn_layer, batch * spb)
    return pl.pallas_call(
        functools.partial(_hgrn_kernel, layer=layer, heads=heads, dk=dk, single_ref=single_ref),
        grid=(batch, spb),
        in_specs=[row, _const_spec((1, d)), _HBM, _const_spec(lb_param.shape),
                  _const_spec((1, dv)), _HBM] + cast.in_specs(step_of),
        out_specs=[row, pl.BlockSpec((1, dk), lambda b, s: (0, 0))] + cast.out_specs(step_of),
        out_shape=[jax.ShapeDtypeStruct((t, d), jnp.float32),
                   jax.ShapeDtypeStruct((1, dk), jnp.float32)] + cast.out_shapes(),
        scratch_shapes=[
            pltpu.VMEM((heads, dv, dk), jnp.float32),
            pltpu.VMEM((tb, w_in_shape[1]), jnp.float32),
            pltpu.VMEM((tb, kd_total), jnp.float32),
            pltpu.VMEM((heads, HGRN_CHUNK, HGRN_CHUNK), MXU_DTYPE),
            pltpu.VMEM((tb, heads * dv), MXU_DTYPE),
        ] + _weight_scratch(*w_in_shape) + _weight_scratch(*w_out_shape),
        compiler_params=pltpu.CompilerParams(dimension_semantics=("arbitrary", "arbitrary"),
                                             vmem_limit_bytes=VMEM_LIMIT_BYTES),
        name="hgrn2_mixer" if single_ref else "hgrn2_mixer_any_decay",
    )(x2, norm_w.reshape(1, d), w_in_all, lb_param, gn_w.reshape(1, dv), w_out_all, *cast.arrays)


def _gelu_tanh(x):
    c = -2.0 * math.sqrt(2.0 / math.pi) * LOG2E
    return x * (1.0 / (1.0 + jnp.exp2(x * (c + (c * 0.044715) * (x * x)))))


def _scan_rows(a, b, n):
    rows = lax.broadcasted_iota(jnp.int32, a.shape, 0)
    d = 1
    while d < n:
        live = rows >= d
        a_sh = jnp.where(live, pltpu.roll(a, d, axis=0), 1.0)
        b_sh = jnp.where(live, pltpu.roll(b, d, axis=0), 0.0)
        b = a * b_sh + b
        a = a * a_sh
        d *= 2
    return a, b


def _scan_level(levels, lvl, j, h0, grouped=False):
    a_ref, b_ref, h_ref = levels[lvl]
    n = a_ref.shape[1]
    if lvl + 1 == len(levels) or n % SCAN_RADIX or n <= SUBLANES:
        a_inc, b_inc = _scan_rows(a_ref[j], b_ref[j], n)
        h_ref[j] = a_inc * h0 + b_inc
        return
    m = n // SCAN_RADIX
    step = lambda r: pl.ds(r, m, stride=SCAN_RADIX)
    read = (lambda r: pl.ds(r * m, m)) if grouped else step
    a_cum = [a_ref[j, read(0), :]]
    h_loc = [b_ref[j, read(0), :]]
    for r in range(1, SCAN_RADIX):
        a_r = a_ref[j, read(r), :]
        h_loc.append(a_r * h_loc[-1] + b_ref[j, read(r), :])
        a_cum.append(a_r * a_cum[-1])
    a_next, b_next, h_next = levels[lvl + 1]
    a_next[j] = a_cum[-1]
    b_next[j] = h_loc[-1]
    _scan_level(levels, lvl + 1, j, h0)
    group = lax.broadcasted_iota(jnp.int32, (m, LANES), 0)
    h_in = jnp.where(group >= 1, pltpu.roll(h_next[j], 1, axis=0), h0)
    for r in range(SCAN_RADIX):
        h_ref[j, step(r), :] = h_loc[r] + a_cum[r] * h_in


def _conv_grouped(u, ext_ref, cw_ref, cb_ref):
    tb, width = u.shape
    taps = cw_ref.shape[0]
    pad = ext_ref.shape[1] - tb
    m = tb // SCAN_RADIX
    cw = cw_ref[...]
    cb = cb_ref[...]
    tiles = []
    for j in range(width // LANES):
        cols = slice(j * LANES, (j + 1) * LANES)
        ext_ref[j, pad:, :] = u[:, cols]
        steps = []
        for r in range(SCAN_RADIX):
            acc = cb[:, cols]
            for t in range(taps):
                shifted = ext_ref[j, pl.ds(pad + r - t, m, stride=SCAN_RADIX), :]
                acc = acc + shifted * cw[taps - 1 - t:taps - t, cols]
            steps.append(acc)
        ext_ref[j, :pad, :] = u[tb - pad:, cols]
        tiles.append(jnp.concatenate(steps, axis=0))
    return jnp.concatenate(tiles, axis=-1)


def _lru_kernel(x_ref, nw_ref, win_hbm, cw_ref, cb_ref, wa_hbm, ba_ref, wx_hbm, bx_ref, lam_ref,
                wout_hbm, fwi_f32, fwo_f32, o_ref, fwi_ref, fwo_ref, h_ref, ext_ref, a_ref, b_ref, hs_ref,
                a1_ref, b1_ref, h1_ref, a2_ref, b2_ref, h2_ref, a3_ref, b3_ref, h3_ref,
                win_ref, win_stage, win_sem, wa_ref, wa_stage, wa_sem, wx_ref, wx_stage, wx_sem,
                wout_ref, wout_stage, wout_sem, *, layer, width, n_blocks):
    tb = x_ref.shape[0]
    bw = width // n_blocks

    @pl.when((pl.program_id(0) == 0) & (pl.program_id(1) == 0))
    def _():
        _load_weight(win_hbm.at[layer], win_ref, win_stage, win_sem)
        _load_weight(wa_hbm.at[layer], wa_ref, wa_stage, wa_sem)
        _load_weight(wx_hbm.at[layer], wx_ref, wx_stage, wx_sem)
        _load_weight(wout_hbm.at[layer], wout_ref, wout_stage, wout_sem)

    @pl.when(pl.program_id(1) == 0)
    def _():
        h_ref[...] = jnp.zeros_like(h_ref)
        ext_ref[:, :SUBLANES, :] = jnp.zeros((ext_ref.shape[0], SUBLANES, LANES), jnp.float32)

    _cast_slices((fwi_f32, fwo_f32), (fwi_ref, fwo_ref))
    x = x_ref[...]
    hn = _rmsnorm(x, nw_ref[...]).astype(MXU_DTYPE)
    proj = _dot(hn, win_ref[...])
    y = _gelu_tanh(proj[:, :width])
    u = proj[:, width:]

    uc = _conv_grouped(u, ext_ref, cw_ref, cb_ref)

    ra, rx = [], []
    for n in range(n_blocks):
        ub = uc[:, n * bw:(n + 1) * bw].astype(MXU_DTYPE)
        ra.append(_dot(ub, wa_ref[n * bw:(n + 1) * bw, :]))
        rx.append(_dot(ub, wx_ref[n * bw:(n + 1) * bw, :]))
    r = _sigmoid(jnp.concatenate(ra, axis=-1) + ba_ref[...])
    ig = _sigmoid(jnp.concatenate(rx, axis=-1) + bx_ref[...])

    nl = -lam_ref[...]
    softplus = jnp.maximum(nl, 0.0) + jnp.log1p(jnp.exp(-jnp.abs(nl)))
    rate = (-LRU_C) * softplus
    log_a = rate * r
    a = jnp.exp2((rate * LOG2E) * r)
    z = -jnp.tanh(log_a) * (a * a + 1.0)
    b_in = jnp.where(z > 0.0, z * lax.rsqrt(z), 0.0) * (ig * uc)

    levels = [(a_ref, b_ref, hs_ref), (a1_ref, b1_ref, h1_ref), (a2_ref, b2_ref, h2_ref),
              (a3_ref, b3_ref, h3_ref)]
    n_tiles = width // LANES
    for j in range(n_tiles):
        cols = slice(j * LANES, (j + 1) * LANES)
        a_ref[j] = a[:, cols]
        b_ref[j] = b_in[:, cols]
    for j in range(n_tiles):
        cols = slice(j * LANES, (j + 1) * LANES)
        _scan_level(levels, 0, j, h_ref[:, cols], grouped=True)
        h_ref[:, cols] = hs_ref[j, tb - 1:tb, :]
    hs = jnp.concatenate([hs_ref[j] for j in range(n_tiles)], axis=-1)
    hy = (hs * y).astype(MXU_DTYPE)

    o_ref[...] = x + _dot(hy, wout_ref[...])


def _lru_mixer(x2, batch, norm_w, w_in_all, conv_w, conv_b, wa_all, ba, wx_all, bx, lam, w_out_all,
               layer, ffn_w_in_all, ffn_w_out_all, ffn_layer):
    t, d = x2.shape
    width = w_out_all.shape[1]
    n_blocks, bw = wa_all.shape[1], wa_all.shape[2]
    taps = conv_w.shape[0]
    wa_all = wa_all.reshape(wa_all.shape[0], n_blocks * bw, bw)
    wx_all = wx_all.reshape(wx_all.shape[0], n_blocks * bw, bw)
    tb = LRU_BLOCK
    spb = t // batch // tb
    assert spb * tb * batch == t and taps - 1 <= SUBLANES and tb % (SUBLANES * SCAN_RADIX) == 0
    step_of = lambda b, s: b * spb + s
    row = pl.BlockSpec((tb, d), lambda b, s: (step_of(b, s), 0))
    vec = lambda v: v.reshape(1, width)
    cast = _FfnWeightCast(ffn_w_in_all, ffn_w_out_all, ffn_layer, batch * spb)
    return pl.pallas_call(
        functools.partial(_lru_kernel, layer=layer, width=width, n_blocks=n_blocks),
        grid=(batch, spb),
        in_specs=[row, _const_spec((1, d)), _HBM, _const_spec(conv_w.shape),
                  _const_spec((1, width)), _HBM, _const_spec((1, width)),
                  _HBM, _const_spec((1, width)), _const_spec((1, width)), _HBM]
        + cast.in_specs(step_of),
        out_specs=[row] + cast.out_specs(step_of),
        out_shape=[jax.ShapeDtypeStruct((t, d), jnp.float32)] + cast.out_shapes(),
        scratch_shapes=[
            pltpu.VMEM((1, width), jnp.float32),
            pltpu.VMEM((width // LANES, SUBLANES + tb, LANES), jnp.float32),
        ] + [pltpu.VMEM((width // LANES, tb // SCAN_RADIX ** lvl, LANES), jnp.float32)
             for lvl in range(SCAN_LEVELS) for _ in "abh"]
        + _weight_scratch(*w_in_all.shape[1:]) + _weight_scratch(*wa_all.shape[1:])
        + _weight_scratch(*wx_all.shape[1:]) + _weight_scratch(*w_out_all.shape[1:]),
        compiler_params=pltpu.CompilerParams(dimension_semantics=("arbitrary", "arbitrary"),
                                             vmem_limit_bytes=VMEM_LIMIT_BYTES),
        name="rglru_mixer",
    )(x2, norm_w.reshape(1, d), w_in_all, conv_w, vec(conv_b), wa_all, vec(ba), wx_all, vec(bx),
      vec(lam), w_out_all, *cast.arrays)


def kernel(x, norm_mix, norm_ffn, norm_final, hgrn_w_in, hgrn_lb, hgrn_norm, hgrn_w_out, lru_w_in, lru_conv_w, lru_conv_b, lru_wa, lru_ba, lru_wx, lru_bx, lru_lambda, lru_w_out, ffn_w_in, ffn_w_out):
    batch, seq, d = x.shape
    depth = norm_mix.shape[0]
    n_mixers = 2

    def ffn(l, h, w_in, w_out):
        return _ffn(h, norm_ffn[l], w_in, w_out, norm_final, final_norm=(l == depth - 1))

    def layers_from(l, h):
        if l == depth:
            return h
        j = l // n_mixers
        if l % n_mixers == 1:
            mixed = _lru_mixer(h, batch, norm_mix[l], lru_w_in, lru_conv_w[j], lru_conv_b[j],
                               lru_wa, lru_ba[j], lru_wx, lru_bx[j], lru_lambda[j], lru_w_out, j,
                               ffn_w_in, ffn_w_out, l)
            return layers_from(l + 1, ffn(l, *mixed))
        hgrn = functools.partial(_hgrn_mixer, h, batch, norm_mix[l], hgrn_w_in, hgrn_lb,
                                 hgrn_norm[j], hgrn_w_out, j, ffn_w_in, ffn_w_out, l)
        mixed, worst, w_in, w_out = hgrn(single_ref=True)

        def any_decay():
            mixed, _, w_in, w_out = hgrn(single_ref=False)
            return layers_from(l + 1, ffn(l, mixed, w_in, w_out))

        return lax.cond(jnp.min(worst) >= -HGRN_SAFE_EXP,
                        lambda: layers_from(l + 1, ffn(l, mixed, w_in, w_out)), any_decay)

    return layers_from(0, x.reshape(batch * seq, d)).reshape(batch, seq, d)
```

```python
import functools
import math

import jax
import jax.numpy as jnp
from jax import lax
from jax.experimental import pallas as pl
from jax.experimental.pallas import tpu as pltpu

EPS = 1e-6
LRU_C = 8.0
LOG2E = math.log2(math.e)
MXU_DTYPE = jnp.bfloat16
SUBLANES = 8
LANES = 128
SCAN_RADIX = 4
SCAN_LEVELS = 4

HGRN_CHUNK = 128
HGRN_SAFE_EXP = 80.0
HGRN_LOG_FLOOR = -1e4
HGRN_BLOCK = 512
LRU_BLOCK = 512
FFN_BLOCK = 1024
FFN_TILE = 256
VMEM_LIMIT_BYTES = 56 * 1024 * 1024
BF16_ROWS = 16
WEIGHT_CHUNK_BYTES = 2 * 1024 * 1024


def _dot(a, b):
    return jnp.dot(a, b, preferred_element_type=jnp.float32)


def _dot_nt(a, b):
    return lax.dot_general(a, b, (((1,), (1,)), ((), ())), preferred_element_type=jnp.float32)


def _rmsnorm(x, w):
    return x * lax.rsqrt(jnp.mean(x * x, axis=-1, keepdims=True) + EPS) * w


def _sigmoid(x):
    return 1.0 / (1.0 + jnp.exp2(x * (-LOG2E)))


def _split(x, terms):
    parts = []
    for i in range(terms):
        p = x.astype(MXU_DTYPE)
        parts.append(p)
        if i + 1 < terms:
            x = x - p.astype(jnp.float32)
    return parts


def _select_sum(sel, parts):
    out = _dot(sel, parts[0])
    for p in parts[1:]:
        out = out + _dot(sel, p)
    return out


def _const_spec(shape):
    return pl.BlockSpec(shape, lambda *_: (0,) * len(shape), pipeline_mode=pl.Buffered(1))


_HBM = pl.BlockSpec(memory_space=pl.ANY)


def _chunk_rows(rows, cols):
    r = max(BF16_ROWS, min(rows, WEIGHT_CHUNK_BYTES // (cols * 4)))
    while rows % r or r % BF16_ROWS:
        r -= 1
    return r


def _weight_scratch(rows, cols):
    return [pltpu.VMEM((rows, cols), MXU_DTYPE),
            pltpu.VMEM((2, _chunk_rows(rows, cols), cols), jnp.float32),
            pltpu.SemaphoreType.DMA((2,))]


def _load_weight(w_hbm, w_ref, stage_ref, sem_ref):
    rows = stage_ref.shape[1]
    n = w_hbm.shape[0] // rows

    def chunk_copy(k):
        return pltpu.make_async_copy(w_hbm.at[pl.ds(k * rows, rows)], stage_ref.at[k % 2], sem_ref.at[k % 2])

    chunk_copy(0).start()
    for k in range(n):
        if k + 1 < n:
            chunk_copy(k + 1).start()
        chunk_copy(k).wait()
        w_ref[k * rows:(k + 1) * rows, :] = stage_ref[k % 2].astype(MXU_DTYPE)


def _ffn_kernel(x_ref, nw_ref, wi_ref, wo_ref, fnw_ref, o_ref, *, d_ff, final_norm):
    x = x_ref[...]
    hn = _rmsnorm(x, nw_ref[...]).astype(MXU_DTYPE)
    acc = x
    for j in range(d_ff // FFN_TILE):
        lo, hi = j * FFN_TILE, (j + 1) * FFN_TILE
        g = _dot(hn, wi_ref[:, lo:hi])
        u = _dot(hn, wi_ref[:, d_ff + lo:d_ff + hi])
        a = (g * _sigmoid(g) * u).astype(MXU_DTYPE)
        acc = acc + _dot(a, wo_ref[lo:hi, :])
    if final_norm:
        acc = _rmsnorm(acc, fnw_ref[...])
    o_ref[...] = acc


def _ffn(x2, norm_w, w_in, w_out, final_norm_w, final_norm):
    t, d = x2.shape
    d_ff = w_out.shape[0]
    assert d_ff % FFN_TILE == 0 and t % FFN_BLOCK == 0
    row = pl.BlockSpec((FFN_BLOCK, d), lambda i: (i, 0))
    return pl.pallas_call(
        functools.partial(_ffn_kernel, d_ff=d_ff, final_norm=final_norm),
        grid=(t // FFN_BLOCK,),
        in_specs=[row, _const_spec((1, d)), _const_spec(w_in.shape), _const_spec(w_out.shape),
                  _const_spec((1, d))],
        out_specs=row,
        out_shape=jax.ShapeDtypeStruct((t, d), jnp.float32),
        compiler_params=pltpu.CompilerParams(dimension_semantics=("arbitrary",),
                                             vmem_limit_bytes=VMEM_LIMIT_BYTES),
        name="swiglu_ffn",
    )(x2, norm_w.reshape(1, d), w_in, w_out, final_norm_w.reshape(1, d))


class _FfnWeightCast:
    def __init__(self, w_in_all, w_out_all, layer, n_steps):
        self.arrays = [w_in_all, w_out_all]
        self.layer = layer
        self.plan = []
        for w in self.arrays:
            rows, share = w.shape[1], 1
            while (rows * share) % (n_steps * BF16_ROWS):
                share *= 2
                assert share <= n_steps, "weight rows do not split into packed-row blocks"
            self.plan.append((rows * share // n_steps, share))

    def in_specs(self, step_of):
        return [pl.BlockSpec((None, r, w.shape[2]), lambda *g, _k=k: (self.layer, step_of(*g) // _k, 0))
                for w, (r, k) in zip(self.arrays, self.plan)]

    def out_specs(self, step_of):
        return [pl.BlockSpec((r, w.shape[2]), lambda *g, _k=k: (step_of(*g) // _k, 0))
                for w, (r, k) in zip(self.arrays, self.plan)]

    def out_shapes(self):
        return [jax.ShapeDtypeStruct(w.shape[1:], MXU_DTYPE) for w in self.arrays]


def _cast_slices(in_refs, out_refs):
    for i_ref, o_ref in zip(in_refs, out_refs):
        o_ref[...] = i_ref[...].astype(MXU_DTYPE)


def _hgrn_scores_single_ref(qd, kd, heads, dk, sc_ref):
    c = qd.shape[0]
    row = lax.broadcasted_iota(jnp.int32, (c, c), 0)
    col = lax.broadcasted_iota(jnp.int32, (c, c), 1)
    causal = col <= row
    for h in range(heads):
        sl = slice(h * dk, (h + 1) * dk)
        s = _dot_nt(qd[:, sl], kd[:, sl])
        sc_ref[h] = jnp.where(causal, s, 0.0).astype(MXU_DTYPE)


def _hgrn_scores_halving(q_c, k_c, lf_parts, heads, dk, sc_ref):
    c = q_c.shape[0]
    row = lax.broadcasted_iota(jnp.int32, (c, c), 0)
    col = lax.broadcasted_iota(jnp.int32, (c, c), 1)
    acc = []
    for h in range(heads):
        sl = slice(h * dk, (h + 1) * dk)
        s = _dot_nt(q_c[:, sl].astype(MXU_DTYPE), k_c[:, sl].astype(MXU_DTYPE))
        acc.append(jnp.where(row == col, s, 0.0))
    m = c // 2
    while m >= 1:
        blk = 2 * m
        ref_of_row = (row // blk) * blk + (m - 1)
        ref_of_col = (col // blk) * blk + (m - 1)
        row_upper = (row % blk) >= m
        sel_q = (row_upper & (col > ref_of_row) & (col <= row)).astype(MXU_DTYPE)
        sel_k = ((~row_upper) & (col > row) & (col <= ref_of_row)).astype(MXU_DTYPE)
        qd = (q_c * jnp.exp(_select_sum(sel_q, lf_parts))).astype(MXU_DTYPE)
        kd = (k_c * jnp.exp(_select_sum(sel_k, lf_parts))).astype(MXU_DTYPE)
        pair = (ref_of_row == ref_of_col) & row_upper & ((col % blk) < m)
        for h in range(heads):
            sl = slice(h * dk, (h + 1) * dk)
            acc[h] = acc[h] + jnp.where(pair, _dot_nt(qd[:, sl], kd[:, sl]), 0.0)
        m //= 2
    for h in range(heads):
        sc_ref[h] = acc[h].astype(MXU_DTYPE)


def _hgrn_kernel(x_ref, nw_ref, win_hbm, lbp_ref, gnw_ref, wout_hbm, fwi_f32, fwo_f32,
                 o_ref, worst_ref, fwi_ref, fwo_ref, st_ref, proj_ref, k_ref, sc_ref, y_ref,
                 win_ref, win_stage, win_sem, wout_ref, wout_stage, wout_sem,
                 *, layer, heads, dk, single_ref):
    kd_total = heads * dk
    tb = x_ref.shape[0]
    c = HGRN_CHUNK
    n_chunks = tb // c
    first = (pl.program_id(0) == 0) & (pl.program_id(1) == 0)

    @pl.when(pl.program_id(1) == 0)
    def _():
        st_ref[...] = jnp.zeros_like(st_ref)

    @pl.when(first)
    def _():
        worst_ref[...] = jnp.zeros_like(worst_ref)
        _load_weight(win_hbm.at[layer], win_ref, win_stage, win_sem)
        _load_weight(wout_hbm.at[layer], wout_ref, wout_stage, wout_sem)

    _cast_slices((fwi_f32, fwo_f32), (fwi_ref, fwo_ref))
    x = x_ref[...]
    hn = _rmsnorm(x, nw_ref[...]).astype(MXU_DTYPE)
    proj = _dot(hn, win_ref[...])

    lp = lbp_ref[...]
    e = jnp.exp(lp - jnp.max(lp, axis=0, keepdims=True))
    lb = jnp.sum(e[: layer + 1], axis=0, keepdims=True) / jnp.sum(e, axis=0, keepdims=True)
    one_m_lb = 1.0 - lb

    q = proj[:, :kd_total]
    fl = proj[:, kd_total:2 * kd_total]
    g = proj[:, 3 * kd_total:]
    en = jnp.exp(-jnp.abs(fl))
    rn = 1.0 / (1.0 + en)
    small = en * rn
    pos = fl >= 0.0
    kk = one_m_lb * jnp.where(pos, small, rn)
    f = lb + one_m_lb * jnp.where(pos, rn, small)
    log_f = jnp.maximum(jnp.log(f), HGRN_LOG_FLOOR)

    proj_ref[:, :kd_total] = q * _sigmoid(q)
    proj_ref[:, kd_total:2 * kd_total] = log_f
    proj_ref[:, 2 * kd_total:3 * kd_total] = proj[:, 2 * kd_total:3 * kd_total]
    proj_ref[:, 3 * kd_total:] = g * _sigmoid(g)
    k_ref[...] = kk

    row = lax.broadcasted_iota(jnp.int32, (c, c), 0)
    col = lax.broadcasted_iota(jnp.int32, (c, c), 1)
    tril = (col <= row).astype(MXU_DTYPE)
    gnw = gnw_ref[...]

    for ci in range(n_chunks):
        rows = slice(ci * c, (ci + 1) * c)
        q_c = proj_ref[rows, :kd_total]
        lf_parts = _split(proj_ref[rows, kd_total:2 * kd_total], 2 if single_ref else 3)
        k_c = k_ref[rows, :]
        b = _select_sum(tril, lf_parts)
        b_last = b[c - 1:c, :]
        dec = jnp.exp(b_last)
        eb = jnp.exp(b)
        qd = (q_c * eb).astype(MXU_DTYPE)
        if single_ref:
            kd = (k_c * (1.0 / eb)).astype(MXU_DTYPE)
            _hgrn_scores_single_ref(qd, kd, heads, dk, sc_ref)
            ke = kd * dec.astype(MXU_DTYPE)
            folded = b_last
            width = kd_total
            while width > worst_ref.shape[1]:
                width //= 2
                folded = jnp.minimum(folded[:, :width], folded[:, width:])
            worst_ref[...] = jnp.minimum(worst_ref[...], folded)
        else:
            _hgrn_scores_halving(q_c, k_c, lf_parts, heads, dk, sc_ref)
            ke = (k_c * jnp.exp(b_last - b)).astype(MXU_DTYPE)
        for h in range(heads):
            sl = slice(h * dk, (h + 1) * dk)
            v_h = proj_ref[rows, 2 * kd_total + h * dk:2 * kd_total + (h + 1) * dk]
            st = st_ref[h]
            o = _dot(sc_ref[h], v_h.astype(MXU_DTYPE)) + _dot_nt(qd[:, sl], st.astype(MXU_DTYPE))
            st_ref[h] = st * dec[:, sl] + _dot(v_h.T.astype(MXU_DTYPE), ke[:, sl])
            o = o * lax.rsqrt(jnp.mean(o * o, axis=-1, keepdims=True) + EPS) * gnw
            sg = proj_ref[rows, 3 * kd_total + h * dk:3 * kd_total + (h + 1) * dk]
            y_ref[rows, sl] = (o * sg).astype(MXU_DTYPE)

    o_ref[...] = x + _dot(y_ref[...], wout_ref[...])


def _hgrn_mixer(x2, batch, norm_w, w_in_all, lb_param, gn_w, w_out_all, layer, ffn_w_in_all,
                ffn_w_out_all, ffn_layer, single_ref):
    t, d = x2.shape
    dv = gn_w.shape[-1]
    w_in_shape, w_out_shape = w_in_all.shape[1:], w_out_all.shape[1:]
    heads = w_out_shape[0] // dv
    kd_total = (w_in_shape[1] - 2 * heads * dv) // 2
    dk = kd_total // heads
    assert dk == dv and kd_total == heads * dv, "kernel assumes key dim == value dim"
    tb = HGRN_BLOCK
    spb = t // batch // tb
    assert spb * tb * batch == t and tb % HGRN_CHUNK == 0
    step_of = lambda b, s: b * spb + s
    row = pl.BlockSpec((tb, d), lambda b, s: (step_of(b, s), 0))
    cast = _FfnWeightCast(ffn_w_in_all, ffn_w_out_all, ffn_layer, batch * spb)
    return pl.pallas_call(
        functools.partial(_hgrn_kernel, layer=layer, heads=heads, dk=dk, single_ref=single_ref),
        grid=(batch, spb),
        in_specs=[row, _const_spec((1, d)), _HBM, _const_spec(lb_param.shape),
                  _const_spec((1, dv)), _HBM] + cast.in_specs(step_of),
        out_specs=[row, pl.BlockSpec((1, dk), lambda b, s: (0, 0))] + cast.out_specs(step_of),
        out_shape=[jax.ShapeDtypeStruct((t, d), jnp.float32),
                   jax.ShapeDtypeStruct((1, dk), jnp.float32)] + cast.out_shapes(),
        scratch_shapes=[
            pltpu.VMEM((heads, dv, dk), jnp.float32),
            pltpu.VMEM((tb, w_in_shape[1]), jnp.float32),
            pltpu.VMEM((tb, kd_total), jnp.float32),
            pltpu.VMEM((heads, HGRN_CHUNK, HGRN_CHUNK), MXU_DTYPE),
            pltpu.VMEM((tb, heads * dv), MXU_DTYPE),
        ] + _weight_scratch(*w_in_shape) + _weight_scratch(*w_out_shape),
        compiler_params=pltpu.CompilerParams(dimension_semantics=("arbitrary", "arbitrary"),
                                             vmem_limit_bytes=VMEM_LIMIT_BYTES),
        name="hgrn2_mixer" if single_ref else "hgrn2_mixer_any_decay",
    )(x2, norm_w.reshape(1, d), w_in_all, lb_param, gn_w.reshape(1, dv), w_out_all, *cast.arrays)


def _gelu_tanh(x):
    c = -2.0 * math.sqrt(2.0 / math.pi) * LOG2E
    return x * (1.0 / (1.0 + jnp.exp2(x * (c + (c * 0.044715) * (x * x)))))


def _scan_rows(a, b, n):
    rows = lax.broadcasted_iota(jnp.int32, a.shape, 0)
    d = 1
    while d < n:
        live = rows >= d
        a_sh = jnp.where(live, pltpu.roll(a, d, axis=0), 1.0)
        b_sh = jnp.where(live, pltpu.roll(b, d, axis=0), 0.0)
        b = a * b_sh + b
        a = a * a_sh
        d *= 2
    return a, b


def _scan_level(levels, lvl, j, h0, grouped=False):
    a_ref, b_ref, h_ref = levels[lvl]
    n = a_ref.shape[1]
    if lvl + 1 == len(levels) or n % SCAN_RADIX or n <= SUBLANES:
        a_inc, b_inc = _scan_rows(a_ref[j], b_ref[j], n)
        h_ref[j] = a_inc * h0 + b_inc
        return
    m = n // SCAN_RADIX
    step = lambda r: pl.ds(r, m, stride=SCAN_RADIX)
    read = (lambda r: pl.ds(r * m, m)) if grouped else step
    a_cum = [a_ref[j, read(0), :]]
    h_loc = [b_ref[j, read(0), :]]
    for r in range(1, SCAN_RADIX):
        a_r = a_ref[j, read(r), :]
        h_loc.append(a_r * h_loc[-1] + b_ref[j, read(r), :])
        a_cum.append(a_r * a_cum[-1])
    a_next, b_next, h_next = levels[lvl + 1]
    a_next[j] = a_cum[-1]
    b_next[j] = h_loc[-1]
    _scan_level(levels, lvl + 1, j, h0)
    group = lax.broadcasted_iota(jnp.int32, (m, LANES), 0)
    h_in = jnp.where(group >= 1, pltpu.roll(h_next[j], 1, axis=0), h0)
    for r in range(SCAN_RADIX):
        h_ref[j, step(r), :] = h_loc[r] + a_cum[r] * h_in


def _conv_grouped(u, ext_ref, cw_ref, cb_ref):
    tb, width = u.shape
    taps = cw_ref.shape[0]
    pad = ext_ref.shape[1] - tb
    m = tb // SCAN_RADIX
    cw = cw_ref[...]
    cb = cb_ref[...]
    tiles = []
    for j in range(width // LANES):
        cols = slice(j * LANES, (j + 1) * LANES)
        ext_ref[j, pad:, :] = u[:, cols]
        steps = []
        for r in range(SCAN_RADIX):
            acc = cb[:, cols]
            for t in range(taps):
                shifted = ext_ref[j, pl.ds(pad + r - t, m, stride=SCAN_RADIX), :]
                acc = acc + shifted * cw[taps - 1 - t:taps - t, cols]
            steps.append(acc)
        ext_ref[j, :pad, :] = u[tb - pad:, cols]
        tiles.append(jnp.concatenate(steps, axis=0))
    return jnp.concatenate(tiles, axis=-1)


def _lru_kernel(x_ref, nw_ref, win_hbm, cw_ref, cb_ref, wa_hbm, ba_ref, wx_hbm, bx_ref, lam_ref,
                wout_hbm, fwi_f32, fwo_f32, o_ref, fwi_ref, fwo_ref, h_ref, ext_ref, a_ref, b_ref, hs_ref,
                a1_ref, b1_ref, h1_ref, a2_ref, b2_ref, h2_ref, a3_ref, b3_ref, h3_ref,
                win_ref, win_stage, win_sem, wa_ref, wa_stage, wa_sem, wx_ref, wx_stage, wx_sem,
                wout_ref, wout_stage, wout_sem, *, layer, width, n_blocks):
    tb = x_ref.shape[0]
    bw = width // n_blocks

    @pl.when((pl.program_id(0) == 0) & (pl.program_id(1) == 0))
    def _():
        _load_weight(win_hbm.at[layer], win_ref, win_stage, win_sem)
        _load_weight(wa_hbm.at[layer], wa_ref, wa_stage, wa_sem)
        _load_weight(wx_hbm.at[layer], wx_ref, wx_stage, wx_sem)
        _load_weight(wout_hbm.at[layer], wout_ref, wout_stage, wout_sem)

    @pl.when(pl.program_id(1) == 0)
    def _():
        h_ref[...] = jnp.zeros_like(h_ref)
        ext_ref[:, :SUBLANES, :] = jnp.zeros((ext_ref.shape[0], SUBLANES, LANES), jnp.float32)

    _cast_slices((fwi_f32, fwo_f32), (fwi_ref, fwo_ref))
    x = x_ref[...]
    hn = _rmsnorm(x, nw_ref[...]).astype(MXU_DTYPE)
    proj = _dot(hn, win_ref[...])
    y = _gelu_tanh(proj[:, :width])
    u = proj[:, width:]

    uc = _conv_grouped(u, ext_ref, cw_ref, cb_ref)

    ra, rx = [], []
    for n in range(n_blocks):
        ub = uc[:, n * bw:(n + 1) * bw].astype(MXU_DTYPE)
        ra.append(_dot(ub, wa_ref[n * bw:(n + 1) * bw, :]))
        rx.append(_dot(ub, wx_ref[n * bw:(n + 1) * bw, :]))
    r = _sigmoid(jnp.concatenate(ra, axis=-1) + ba_ref[...])
    ig = _sigmoid(jnp.concatenate(rx, axis=-1) + bx_ref[...])

    nl = -lam_ref[...]
    softplus = jnp.maximum(nl, 0.0) + jnp.log1p(jnp.exp(-jnp.abs(nl)))
    rate = (-LRU_C) * softplus
    log_a = rate * r
    a = jnp.exp2((rate * LOG2E) * r)
    z = -jnp.tanh(log_a) * (a * a + 1.0)
    b_in = jnp.where(z > 0.0, z * lax.rsqrt(z), 0.0) * (ig * uc)

    levels = [(a_ref, b_ref, hs_ref), (a1_ref, b1_ref, h1_ref), (a2_ref, b2_ref, h2_ref),
              (a3_ref, b3_ref, h3_ref)]
    n_tiles = width // LANES
    for j in range(n_tiles):
        cols = slice(j * LANES, (j + 1) * LANES)
        a_ref[j] = a[:, cols]
        b_ref[j] = b_in[:, cols]
    for j in range(n_tiles):
        cols = slice(j * LANES, (j + 1) * LANES)
        _scan_level(levels, 0, j, h_ref[:, cols], grouped=True)
        h_ref[:, cols] = hs_ref[j, tb - 1:tb, :]
    hs = jnp.concatenate([hs_ref[j] for j in range(n_tiles)], axis=-1)
    hy = (hs * y).astype(MXU_DTYPE)

    o_ref[...] = x + _dot(hy, wout_ref[...])


def _lru_mixer(x2, batch, norm_w, w_in_all, conv_w, conv_b, wa_all, ba, wx_all, bx, lam, w_out_all,
               layer, ffn_w_in_all, ffn_w_out_all, ffn_layer):
    t, d = x2.shape
    width = w_out_all.shape[1]
    n_blocks, bw = wa_all.shape[1], wa_all.shape[2]
    taps = conv_w.shape[0]
    wa_all = wa_all.reshape(wa_all.shape[0], n_blocks * bw, bw)
    wx_all = wx_all.reshape(wx_all.shape[0], n_blocks * bw, bw)
    tb = LRU_BLOCK
    spb = t // batch // tb
    assert spb * tb * batch == t and taps - 1 <= SUBLANES and tb % (SUBLANES * SCAN_RADIX) == 0
    step_of = lambda b, s: b * spb + s
    row = pl.BlockSpec((tb, d), lambda b, s: (step_of(b, s), 0))
    vec = lambda v: v.reshape(1, width)
    cast = _FfnWeightCast(ffn_w_in_all, ffn_w_out_all, ffn_layer, batch * spb)
    return pl.pallas_call(
        functools.partial(_lru_kernel, layer=layer, width=width, n_blocks=n_blocks),
        grid=(batch, spb),
        in_specs=[row, _const_spec((1, d)), _HBM, _const_spec(conv_w.shape),
                  _const_spec((1, width)), _HBM, _const_spec((1, width)),
                  _HBM, _const_spec((1, width)), _const_spec((1, width)), _HBM]
        + cast.in_specs(step_of),
        out_specs=[row] + cast.out_specs(step_of),
        out_shape=[jax.ShapeDtypeStruct((t, d), jnp.float32)] + cast.out_shapes(),
        scratch_shapes=[
            pltpu.VMEM((1, width), jnp.float32),
            pltpu.VMEM((width // LANES, SUBLANES + tb, LANES), jnp.float32),
        ] + [pltpu.VMEM((width // LANES, tb // SCAN_RADIX ** lvl, LANES), jnp.float32)
             for lvl in range(SCAN_LEVELS) for _ in "abh"]
        + _weight_scratch(*w_in_all.shape[1:]) + _weight_scratch(*wa_all.shape[1:])
        + _weight_scratch(*wx_all.shape[1:]) + _weight_scratch(*w_out_all.shape[1:]),
        compiler_params=pltpu.CompilerParams(dimension_semantics=("arbitrary", "arbitrary"),
                                             vmem_limit_bytes=VMEM_LIMIT_BYTES),
        name="rglru_mixer",
    )(x2, norm_w.reshape(1, d), w_in_all, conv_w, vec(conv_b), wa_all, vec(ba), wx_all, vec(bx),
      vec(lam), w_out_all, *cast.arrays)


def kernel(x, norm_mix, norm_ffn, norm_final, hgrn_w_in, hgrn_lb, hgrn_norm, hgrn_w_out, lru_w_in, lru_conv_w, lru_conv_b, lru_wa, lru_ba, lru_wx, lru_bx, lru_lambda, lru_w_out, ffn_w_in, ffn_w_out):
    batch, seq, d = x.shape
    depth = norm_mix.shape[0]
    n_mixers = 2

    def ffn(l, h, w_in, w_out):
        return _ffn(h, norm_ffn[l], w_in, w_out, norm_final, final_norm=(l == depth - 1))

    def layers_from(l, h):
        if l == depth:
            return h
        j = l // n_mixers
        if l % n_mixers == 1:
            mixed = _lru_mixer(h, batch, norm_mix[l], lru_w_in, lru_conv_w[j], lru_conv_b[j],
                               lru_wa, lru_ba[j], lru_wx, lru_bx[j], lru_lambda[j], lru_w_out, j,
                               ffn_w_in, ffn_w_out, l)
            return layers_from(l + 1, ffn(l, *mixed))
        hgrn = functools.partial(_hgrn_mixer, h, batch, norm_mix[l], hgrn_w_in, hgrn_lb,
                                 hgrn_norm[j], hgrn_w_out, j, ffn_w_in, ffn_w_out, l)
        mixed, worst, w_in, w_out = hgrn(single_ref=True)

        def any_decay():
            mixed, _, w_in, w_out = hgrn(single_ref=False)
            return layers_from(l + 1, ffn(l, mixed, w_in, w_out))

        return lax.cond(jnp.min(worst) >= -HGRN_SAFE_EXP,
                        lambda: layers_from(l + 1, ffn(l, mixed, w_in, w_out)), any_decay)

    return layers_from(0, x.reshape(batch * seq, d)).reshape(batch, seq, d)
```

```python
import functools
import math

import jax
import jax.numpy as jnp
from jax import lax
from jax.experimental import pallas as pl
from jax.experimental.pallas import tpu as pltpu

EPS = 1e-6
LRU_C = 8.0
LOG2E = math.log2(math.e)
MXU_DTYPE = jnp.bfloat16
SUBLANES = 8
LANES = 128
SCAN_RADIX = 4
SCAN_LEVELS = 4

HGRN_CHUNK = 128
HGRN_SAFE_EXP = 80.0
HGRN_LOG_FLOOR = -1e4
HGRN_BLOCK = 512
LRU_BLOCK = 512
FFN_BLOCK = 512
FFN_TILE = 256
VMEM_LIMIT_BYTES = 56 * 1024 * 1024
BF16_ROWS = 16
WEIGHT_CHUNK_BYTES = 2 * 1024 * 1024


def _dot(a, b):
    return jnp.dot(a, b, preferred_element_type=jnp.float32)


def _dot_nt(a, b):
    return lax.dot_general(a, b, (((1,), (1,)), ((), ())), preferred_element_type=jnp.float32)


def _rmsnorm(x, w):
    return x * lax.rsqrt(jnp.mean(x * x, axis=-1, keepdims=True) + EPS) * w


def _sigmoid(x):
    return 1.0 / (1.0 + jnp.exp2(x * (-LOG2E)))


def _split(x, terms):
    parts = []
    for i in range(terms):
        p = x.astype(MXU_DTYPE)
        parts.append(p)
        if i + 1 < terms:
            x = x - p.astype(jnp.float32)
    return parts


def _select_sum(sel, parts):
    out = _dot(sel, parts[0])
    for p in parts[1:]:
        out = out + _dot(sel, p)
    return out


def _const_spec(shape):
    return pl.BlockSpec(shape, lambda *_: (0,) * len(shape), pipeline_mode=pl.Buffered(1))


_HBM = pl.BlockSpec(memory_space=pl.ANY)


def _chunk_rows(rows, cols):
    r = max(BF16_ROWS, min(rows, WEIGHT_CHUNK_BYTES // (cols * 4)))
    while rows % r or r % BF16_ROWS:
        r -= 1
    return r


def _weight_scratch(rows, cols):
    return [pltpu.VMEM((rows, cols), MXU_DTYPE),
            pltpu.VMEM((2, _chunk_rows(rows, cols), cols), jnp.float32),
            pltpu.SemaphoreType.DMA((2,))]


def _load_weight(w_hbm, w_ref, stage_ref, sem_ref):
    rows = stage_ref.shape[1]
    n = w_hbm.shape[0] // rows

    def chunk_copy(k):
        return pltpu.make_async_copy(w_hbm.at[pl.ds(k * rows, rows)], stage_ref.at[k % 2], sem_ref.at[k % 2])

    chunk_copy(0).start()
    for k in range(n):
        if k + 1 < n:
            chunk_copy(k + 1).start()
        chunk_copy(k).wait()
        w_ref[k * rows:(k + 1) * rows, :] = stage_ref[k % 2].astype(MXU_DTYPE)


def _ffn_kernel(x_hbm, nw_ref, wi_ref, wo_ref, fnw_ref, o_hbm, act_ref, *, d_ff, final_norm):
    t, d = x_hbm.shape

    def block(x_ref, o_ref):
        x = x_ref[...]
        hn = _rmsnorm(x, nw_ref[...]).astype(MXU_DTYPE)
        for j in range(d_ff // FFN_TILE):
            lo, hi = j * FFN_TILE, (j + 1) * FFN_TILE
            g = _dot(hn, wi_ref[:, lo:hi])
            u = _dot(hn, wi_ref[:, d_ff + lo:d_ff + hi])
            act_ref[:, lo:hi] = (g * _sigmoid(g) * u).astype(MXU_DTYPE)
        acc = x + _dot(act_ref[...], wo_ref[...])
        if final_norm:
            acc = _rmsnorm(acc, fnw_ref[...])
        o_ref[...] = acc

    row = pl.BlockSpec((FFN_BLOCK, d), lambda i: (i, 0))
    pltpu.emit_pipeline(block, grid=(t // FFN_BLOCK,), in_specs=[row], out_specs=[row])(x_hbm, o_hbm)


def _ffn(x2, norm_w, w_in, w_out, final_norm_w, final_norm):
    t, d = x2.shape
    d_ff = w_out.shape[0]
    assert d_ff % FFN_TILE == 0 and t % FFN_BLOCK == 0
    vmem = pl.BlockSpec(memory_space=pltpu.VMEM)
    return pl.pallas_call(
        functools.partial(_ffn_kernel, d_ff=d_ff, final_norm=final_norm),
        in_specs=[_HBM, vmem, vmem, vmem, vmem],
        out_specs=_HBM,
        out_shape=jax.ShapeDtypeStruct((t, d), jnp.float32),
        scratch_shapes=[pltpu.VMEM((FFN_BLOCK, d_ff), MXU_DTYPE)],
        compiler_params=pltpu.CompilerParams(vmem_limit_bytes=VMEM_LIMIT_BYTES),
        name="swiglu_ffn",
    )(x2, norm_w.reshape(1, d), w_in, w_out, final_norm_w.reshape(1, d))


class _FfnWeightCast:
    def __init__(self, w_in_all, w_out_all, layer, n_steps):
        self.arrays = [w_in_all, w_out_all]
        self.layer = layer
        self.plan = []
        for w in self.arrays:
            rows, share = w.shape[1], 1
            while (rows * share) % (n_steps * BF16_ROWS):
                share *= 2
                assert share <= n_steps, "weight rows do not split into packed-row blocks"
            self.plan.append((rows * share // n_steps, share))

    def in_specs(self, step_of):
        return [pl.BlockSpec((None, r, w.shape[2]), lambda *g, _k=k: (self.layer, step_of(*g) // _k, 0))
                for w, (r, k) in zip(self.arrays, self.plan)]

    def out_specs(self, step_of):
        return [pl.BlockSpec((r, w.shape[2]), lambda *g, _k=k: (step_of(*g) // _k, 0))
                for w, (r, k) in zip(self.arrays, self.plan)]

    def out_shapes(self):
        return [jax.ShapeDtypeStruct(w.shape[1:], MXU_DTYPE) for w in self.arrays]


def _cast_slices(in_refs, out_refs):
    for i_ref, o_ref in zip(in_refs, out_refs):
        o_ref[...] = i_ref[...].astype(MXU_DTYPE)


def _hgrn_scores_single_ref(qd, kd, heads, dk, sc_ref):
    c = qd.shape[0]
    row = lax.broadcasted_iota(jnp.int32, (c, c), 0)
    col = lax.broadcasted_iota(jnp.int32, (c, c), 1)
    causal = col <= row
    for h in range(heads):
        sl = slice(h * dk, (h + 1) * dk)
        s = _dot_nt(qd[:, sl], kd[:, sl])
        sc_ref[h] = jnp.where(causal, s, 0.0).astype(MXU_DTYPE)


def _hgrn_scores_halving(q_c, k_c, lf_parts, heads, dk, sc_ref):
    c = q_c.shape[0]
    row = lax.broadcasted_iota(jnp.int32, (c, c), 0)
    col = lax.broadcasted_iota(jnp.int32, (c, c), 1)
    acc = []
    for h in range(heads):
        sl = slice(h * dk, (h + 1) * dk)
        s = _dot_nt(q_c[:, sl].astype(MXU_DTYPE), k_c[:, sl].astype(MXU_DTYPE))
        acc.append(jnp.where(row == col, s, 0.0))
    m = c // 2
    while m >= 1:
        blk = 2 * m
        ref_of_row = (row // blk) * blk + (m - 1)
        ref_of_col = (col // blk) * blk + (m - 1)
        row_upper = (row % blk) >= m
        sel_q = (row_upper & (col > ref_of_row) & (col <= row)).astype(MXU_DTYPE)
        sel_k = ((~row_upper) & (col > row) & (col <= ref_of_row)).astype(MXU_DTYPE)
        qd = (q_c * jnp.exp(_select_sum(sel_q, lf_parts))).astype(MXU_DTYPE)
        kd = (k_c * jnp.exp(_select_sum(sel_k, lf_parts))).astype(MXU_DTYPE)
        pair = (ref_of_row == ref_of_col) & row_upper & ((col % blk) < m)
        for h in range(heads):
            sl = slice(h * dk, (h + 1) * dk)
            acc[h] = acc[h] + jnp.where(pair, _dot_nt(qd[:, sl], kd[:, sl]), 0.0)
        m //= 2
    for h in range(heads):
        sc_ref[h] = acc[h].astype(MXU_DTYPE)


def _hgrn_kernel(x_ref, nw_ref, win_hbm, lbp_ref, gnw_ref, wout_hbm, fwi_f32, fwo_f32,
                 o_ref, worst_ref, fwi_ref, fwo_ref, st_ref, proj_ref, k_ref, sc_ref, y_ref,
                 win_ref, win_stage, win_sem, wout_ref, wout_stage, wout_sem,
                 *, layer, heads, dk, single_ref):
    kd_total = heads * dk
    tb = x_ref.shape[0]
    c = HGRN_CHUNK
    n_chunks = tb // c
    first = (pl.program_id(0) == 0) & (pl.program_id(1) == 0)

    @pl.when(pl.program_id(1) == 0)
    def _():
        st_ref[...] = jnp.zeros_like(st_ref)

    @pl.when(first)
    def _():
        worst_ref[...] = jnp.zeros_like(worst_ref)
        _load_weight(win_hbm.at[layer], win_ref, win_stage, win_sem)
        _load_weight(wout_hbm.at[layer], wout_ref, wout_stage, wout_sem)

    _cast_slices((fwi_f32, fwo_f32), (fwi_ref, fwo_ref))
    x = x_ref[...]
    hn = _rmsnorm(x, nw_ref[...]).astype(MXU_DTYPE)
    proj = _dot(hn, win_ref[...])

    lp = lbp_ref[...]
    e = jnp.exp(lp - jnp.max(lp, axis=0, keepdims=True))
    lb = jnp.sum(e[: layer + 1], axis=0, keepdims=True) / jnp.sum(e, axis=0, keepdims=True)
    one_m_lb = 1.0 - lb

    q = proj[:, :kd_total]
    fl = proj[:, kd_total:2 * kd_total]
    g = proj[:, 3 * kd_total:]
    en = jnp.exp(-jnp.abs(fl))
    rn = 1.0 / (1.0 + en)
    small = en * rn
    pos = fl >= 0.0
    kk = one_m_lb * jnp.where(pos, small, rn)
    f = lb + one_m_lb * jnp.where(pos, rn, small)
    log_f = jnp.maximum(jnp.log(f), HGRN_LOG_FLOOR)

    proj_ref[:, :kd_total] = q * _sigmoid(q)
    proj_ref[:, kd_total:2 * kd_total] = log_f
    proj_ref[:, 2 * kd_total:3 * kd_total] = proj[:, 2 * kd_total:3 * kd_total]
    proj_ref[:, 3 * kd_total:] = g * _sigmoid(g)
    k_ref[...] = kk

    row = lax.broadcasted_iota(jnp.int32, (c, c), 0)
    col = lax.broadcasted_iota(jnp.int32, (c, c), 1)
    tril = (col <= row).astype(MXU_DTYPE)
    gnw = gnw_ref[...]

    for ci in range(n_chunks):
        rows = slice(ci * c, (ci + 1) * c)
        q_c = proj_ref[rows, :kd_total]
        lf_parts = _split(proj_ref[rows, kd_total:2 * kd_total], 2 if single_ref else 3)
        k_c = k_ref[rows, :]
        b = _select_sum(tril, lf_parts)
        b_last = b[c - 1:c, :]
        dec = jnp.exp(b_last)
        eb = jnp.exp(b)
        qd = (q_c * eb).astype(MXU_DTYPE)
        if single_ref:
            kd = (k_c * (1.0 / eb)).astype(MXU_DTYPE)
            _hgrn_scores_single_ref(qd, kd, heads, dk, sc_ref)
            ke = kd * dec.astype(MXU_DTYPE)
            folded = b_last
            width = kd_total
            while width > worst_ref.shape[1]:
                width //= 2
                folded = jnp.minimum(folded[:, :width], folded[:, width:])
            worst_ref[...] = jnp.minimum(worst_ref[...], folded)
        else:
            _hgrn_scores_halving(q_c, k_c, lf_parts, heads, dk, sc_ref)
            ke = (k_c * jnp.exp(b_last - b)).astype(MXU_DTYPE)
        for h in range(heads):
            sl = slice(h * dk, (h + 1) * dk)
            v_h = proj_ref[rows, 2 * kd_total + h * dk:2 * kd_total + (h + 1) * dk]
            st = st_ref[h]
            o = _dot(sc_ref[h], v_h.astype(MXU_DTYPE)) + _dot_nt(qd[:, sl], st.astype(MXU_DTYPE))
            st_ref[h] = st * dec[:, sl] + _dot(v_h.T.astype(MXU_DTYPE), ke[:, sl])
            o = o * lax.rsqrt(jnp.mean(o * o, axis=-1, keepdims=True) + EPS) * gnw
            sg = proj_ref[rows, 3 * kd_total + h * dk:3 * kd_total + (h + 1) * dk]
            y_ref[rows, sl] = (o * sg).astype(MXU_DTYPE)

    o_ref[...] = x + _dot(y_ref[...], wout_ref[...])


def _hgrn_mixer(x2, batch, norm_w, w_in_all, lb_param, gn_w, w_out_all, layer, ffn_w_in_all,
                ffn_w_out_all, ffn_layer, single_ref):
    t, d = x2.shape
    dv = gn_w.shape[-1]
    w_in_shape, w_out_shape = w_in_all.shape[1:], w_out_all.shape[1:]
    heads = w_out_shape[0] // dv
    kd_total = (w_in_shape[1] - 2 * heads * dv) // 2
    dk = kd_total // heads
    assert dk == dv and kd_total == heads * dv, "kernel assumes key dim == value dim"
    tb = HGRN_BLOCK
    spb = t // batch // tb
    assert spb * tb * batch == t and tb % HGRN_CHUNK == 0
    step_of = lambda b, s: b * spb + s
    row = pl.BlockSpec((tb, d), lambda b, s: (step_of(b, s), 0))
    cast = _FfnWeightCast(ffn_w_in_all, ffn_w_out_all, ffn_layer, batch * spb)
    return pl.pallas_call(
        functools.partial(_hgrn_kernel, layer=layer, heads=heads, dk=dk, single_ref=single_ref),
        grid=(batch, spb),
        in_specs=[row, _const_spec((1, d)), _HBM, _const_spec(lb_param.shape),
                  _const_spec((1, dv)), _HBM] + cast.in_specs(step_of),
        out_specs=[row, pl.BlockSpec((1, dk), lambda b, s: (0, 0))] + cast.out_specs(step_of),
        out_shape=[jax.ShapeDtypeStruct((t, d), jnp.float32),
                   jax.ShapeDtypeStruct((1, dk), jnp.float32)] + cast.out_shapes(),
        scratch_shapes=[
            pltpu.VMEM((heads, dv, dk), jnp.float32),
            pltpu.VMEM((tb, w_in_shape[1]), jnp.float32),
            pltpu.VMEM((tb, kd_total), jnp.float32),
            pltpu.VMEM((heads, HGRN_CHUNK, HGRN_CHUNK), MXU_DTYPE),
            pltpu.VMEM((tb, heads * dv), MXU_DTYPE),
        ] + _weight_scratch(*w_in_shape) + _weight_scratch(*w_out_shape),
        compiler_params=pltpu.CompilerParams(dimension_semantics=("arbitrary", "arbitrary"),
                                             vmem_limit_bytes=VMEM_LIMIT_BYTES),
        name="hgrn2_mixer" if single_ref else "hgrn2_mixer_any_decay",
    )(x2, norm_w.reshape(1, d), w_in_all, lb_param, gn_w.reshape(1, dv), w_out_all, *cast.arrays)


def _gelu_tanh(x):
    c = -2.0 * math.sqrt(2.0 / math.pi) * LOG2E
    return x * (1.0 / (1.0 + jnp.exp2(x * (c + (c * 0.044715) * (x * x)))))


def _scan_rows(a, b, n):
    rows = lax.broadcasted_iota(jnp.int32, a.shape, 0)
    d = 1
    while d < n:
        live = rows >= d
        a_sh = jnp.where(live, pltpu.roll(a, d, axis=0), 1.0)
        b_sh = jnp.where(live, pltpu.roll(b, d, axis=0), 0.0)
        b = a * b_sh + b
        a = a * a_sh
        d *= 2
    return a, b


def _scan_level(levels, lvl, j, h0, grouped=False):
    a_ref, b_ref, h_ref = levels[lvl]
    n = a_ref.shape[1]
    if lvl + 1 == len(levels) or n % SCAN_RADIX or n <= SUBLANES:
        a_inc, b_inc = _scan_rows(a_ref[j], b_ref[j], n)
        h_ref[j] = a_inc * h0 + b_inc
        return
    m = n // SCAN_RADIX
    step = lambda r: pl.ds(r, m, stride=SCAN_RADIX)
    read = (lambda r: pl.ds(r * m, m)) if grouped else step
    a_cum = [a_ref[j, read(0), :]]
    h_loc = [b_ref[j, read(0), :]]
    for r in range(1, SCAN_RADIX):
        a_r = a_ref[j, read(r), :]
        h_loc.append(a_r * h_loc[-1] + b_ref[j, read(r), :])
        a_cum.append(a_r * a_cum[-1])
    a_next, b_next, h_next = levels[lvl + 1]
    a_next[j] = a_cum[-1]
    b_next[j] = h_loc[-1]
    _scan_level(levels, lvl + 1, j, h0)
    group = lax.broadcasted_iota(jnp.int32, (m, LANES), 0)
    h_in = jnp.where(group >= 1, pltpu.roll(h_next[j], 1, axis=0), h0)
    for r in range(SCAN_RADIX):
        h_ref[j, step(r), :] = h_loc[r] + a_cum[r] * h_in


def _conv_grouped(u, ext_ref, cw_ref, cb_ref):
    tb, width = u.shape
    taps = cw_ref.shape[0]
    pad = ext_ref.shape[1] - tb
    m = tb // SCAN_RADIX
    cw = cw_ref[...]
    cb = cb_ref[...]
    tiles = []
    for j in range(width // LANES):
        cols = slice(j * LANES, (j + 1) * LANES)
        ext_ref[j, pad:, :] = u[:, cols]
        steps = []
        for r in range(SCAN_RADIX):
            acc = cb[:, cols]
            for t in range(taps):
                shifted = ext_ref[j, pl.ds(pad + r - t, m, stride=SCAN_RADIX), :]
                acc = acc + shifted * cw[taps - 1 - t:taps - t, cols]
            steps.append(acc)
        ext_ref[j, :pad, :] = u[tb - pad:, cols]
        tiles.append(jnp.concatenate(steps, axis=0))
    return jnp.concatenate(tiles, axis=-1)


def _lru_kernel(x_ref, nw_ref, win_hbm, cw_ref, cb_ref, wa_hbm, ba_ref, wx_hbm, bx_ref, lam_ref,
                wout_hbm, fwi_f32, fwo_f32, o_ref, fwi_ref, fwo_ref, h_ref, ext_ref, a_ref, b_ref, hs_ref,
                a1_ref, b1_ref, h1_ref, a2_ref, b2_ref, h2_ref, a3_ref, b3_ref, h3_ref,
                win_ref, win_stage, win_sem, wa_ref, wa_stage, wa_sem, wx_ref, wx_stage, wx_sem,
                wout_ref, wout_stage, wout_sem, *, layer, width, n_blocks):
    tb = x_ref.shape[0]
    bw = width // n_blocks

    @pl.when((pl.program_id(0) == 0) & (pl.program_id(1) == 0))
    def _():
        _load_weight(win_hbm.at[layer], win_ref, win_stage, win_sem)
        _load_weight(wa_hbm.at[layer], wa_ref, wa_stage, wa_sem)
        _load_weight(wx_hbm.at[layer], wx_ref, wx_stage, wx_sem)
        _load_weight(wout_hbm.at[layer], wout_ref, wout_stage, wout_sem)

    @pl.when(pl.program_id(1) == 0)
    def _():
        h_ref[...] = jnp.zeros_like(h_ref)
        ext_ref[:, :SUBLANES, :] = jnp.zeros((ext_ref.shape[0], SUBLANES, LANES), jnp.float32)

    _cast_slices((fwi_f32, fwo_f32), (fwi_ref, fwo_ref))
    x = x_ref[...]
    hn = _rmsnorm(x, nw_ref[...]).astype(MXU_DTYPE)
    proj = _dot(hn, win_ref[...])
    y = _gelu_tanh(proj[:, :width])
    u = proj[:, width:]

    uc = _conv_grouped(u, ext_ref, cw_ref, cb_ref)

    ra, rx = [], []
    for n in range(n_blocks):
        ub = uc[:, n * bw:(n + 1) * bw].astype(MXU_DTYPE)
        ra.append(_dot(ub, wa_ref[n * bw:(n + 1) * bw, :]))
        rx.append(_dot(ub, wx_ref[n * bw:(n + 1) * bw, :]))
    r = _sigmoid(jnp.concatenate(ra, axis=-1) + ba_ref[...])
    ig = _sigmoid(jnp.concatenate(rx, axis=-1) + bx_ref[...])

    nl = -lam_ref[...]
    softplus = jnp.maximum(nl, 0.0) + jnp.log1p(jnp.exp(-jnp.abs(nl)))
    rate = (-LRU_C) * softplus
    log_a = rate * r
    a = jnp.exp2((rate * LOG2E) * r)
    z = -jnp.tanh(log_a) * (a * a + 1.0)
    b_in = jnp.where(z > 0.0, z * lax.rsqrt(z), 0.0) * (ig * uc)

    levels = [(a_ref, b_ref, hs_ref), (a1_ref, b1_ref, h1_ref), (a2_ref, b2_ref, h2_ref),
              (a3_ref, b3_ref, h3_ref)]
    n_tiles = width // LANES
    for j in range(n_tiles):
        cols = slice(j * LANES, (j + 1) * LANES)
        a_ref[j] = a[:, cols]
        b_ref[j] = b_in[:, cols]
    for j in range(n_tiles):
        cols = slice(j * LANES, (j + 1) * LANES)
        _scan_level(levels, 0, j, h_ref[:, cols], grouped=True)
        h_ref[:, cols] = hs_ref[j, tb - 1:tb, :]
    hs = jnp.concatenate([hs_ref[j] for j in range(n_tiles)], axis=-1)
    hy = (hs * y).astype(MXU_DTYPE)

    o_ref[...] = x + _dot(hy, wout_ref[...])


def _lru_mixer(x2, batch, norm_w, w_in_all, conv_w, conv_b, wa_all, ba, wx_all, bx, lam, w_out_all,
               layer, ffn_w_in_all, ffn_w_out_all, ffn_layer):
    t, d = x2.shape
    width = w_out_all.shape[1]
    n_blocks, bw = wa_all.shape[1], wa_all.shape[2]
    taps = conv_w.shape[0]
    wa_all = wa_all.reshape(wa_all.shape[0], n_blocks * bw, bw)
    wx_all = wx_all.reshape(wx_all.shape[0], n_blocks * bw, bw)
    tb = LRU_BLOCK
    spb = t // batch // tb
    assert spb * tb * batch == t and taps - 1 <= SUBLANES and tb % (SUBLANES * SCAN_RADIX) == 0
    step_of = lambda b, s: b * spb + s
    row = pl.BlockSpec((tb, d), lambda b, s: (step_of(b, s), 0))
    vec = lambda v: v.reshape(1, width)
    cast = _FfnWeightCast(ffn_w_in_all, ffn_w_out_all, ffn_layer, batch * spb)
    return pl.pallas_call(
        functools.partial(_lru_kernel, layer=layer, width=width, n_blocks=n_blocks),
        grid=(batch, spb),
        in_specs=[row, _const_spec((1, d)), _HBM, _const_spec(conv_w.shape),
                  _const_spec((1, width)), _HBM, _const_spec((1, width)),
                  _HBM, _const_spec((1, width)), _const_spec((1, width)), _HBM]
        + cast.in_specs(step_of),
        out_specs=[row] + cast.out_specs(step_of),
        out_shape=[jax.ShapeDtypeStruct((t, d), jnp.float32)] + cast.out_shapes(),
        scratch_shapes=[
            pltpu.VMEM((1, width), jnp.float32),
            pltpu.VMEM((width // LANES, SUBLANES + tb, LANES), jnp.float32),
        ] + [pltpu.VMEM((width // LANES, tb // SCAN_RADIX ** lvl, LANES), jnp.float32)
             for lvl in range(SCAN_LEVELS) for _ in "abh"]
        + _weight_scratch(*w_in_all.shape[1:]) + _weight_scratch(*wa_all.shape[1:])
        + _weight_scratch(*wx_all.shape[1:]) + _weight_scratch(*w_out_all.shape[1:]),
        compiler_params=pltpu.CompilerParams(dimension_semantics=("arbitrary", "arbitrary"),
                                             vmem_limit_bytes=VMEM_LIMIT_BYTES),
        name="rglru_mixer",
    )(x2, norm_w.reshape(1, d), w_in_all, conv_w, vec(conv_b), wa_all, vec(ba), wx_all, vec(bx),
      vec(lam), w_out_all, *cast.arrays)


def kernel(x, norm_mix, norm_ffn, norm_final, hgrn_w_in, hgrn_lb, hgrn_norm, hgrn_w_out, lru_w_in, lru_conv_w, lru_conv_b, lru_wa, lru_ba, lru_wx, lru_bx, lru_lambda, lru_w_out, ffn_w_in, ffn_w_out):
    batch, seq, d = x.shape
    depth = norm_mix.shape[0]
    n_mixers = 2

    def ffn(l, h, w_in, w_out):
        return _ffn(h, norm_ffn[l], w_in, w_out, norm_final, final_norm=(l == depth - 1))

    def layers_from(l, h):
        if l == depth:
            return h
        j = l // n_mixers
        if l % n_mixers == 1:
            mixed = _lru_mixer(h, batch, norm_mix[l], lru_w_in, lru_conv_w[j], lru_conv_b[j],
                               lru_wa, lru_ba[j], lru_wx, lru_bx[j], lru_lambda[j], lru_w_out, j,
                               ffn_w_in, ffn_w_out, l)
            return layers_from(l + 1, ffn(l, *mixed))
        hgrn = functools.partial(_hgrn_mixer, h, batch, norm_mix[l], hgrn_w_in, hgrn_lb,
                                 hgrn_norm[j], hgrn_w_out, j, ffn_w_in, ffn_w_out, l)
        mixed, worst, w_in, w_out = hgrn(single_ref=True)

        def any_decay():
            mixed, _, w_in, w_out = hgrn(single_ref=False)
            return layers_from(l + 1, ffn(l, mixed, w_in, w_out))

        return lax.cond(jnp.min(worst) >= -HGRN_SAFE_EXP,
                        lambda: layers_from(l + 1, ffn(l, mixed, w_in, w_out)), any_decay)

    return layers_from(0, x.reshape(batch * seq, d)).reshape(batch, seq, d)
```

```python
import functools
import math

import jax
import jax.numpy as jnp
from jax import lax
from jax.experimental import pallas as pl
from jax.experimental.pallas import tpu as pltpu

EPS = 1e-6
LRU_C = 8.0
LOG2E = math.log2(math.e)
MXU_DTYPE = jnp.bfloat16
SUBLANES = 8
LANES = 128
SCAN_RADIX = 4
SCAN_LEVELS = 4

HGRN_CHUNK = 128
HGRN_SAFE_EXP = 80.0
HGRN_LOG_FLOOR = -1e4
HGRN_BLOCK = 512
LRU_BLOCK = 512
FFN_BLOCK = 512
FFN_TILE = 256
VMEM_LIMIT_BYTES = 56 * 1024 * 1024
BF16_ROWS = 16
WEIGHT_CHUNK_BYTES = 2 * 1024 * 1024


def _dot(a, b):
    return jnp.dot(a, b, preferred_element_type=jnp.float32)


def _dot_nt(a, b):
    return lax.dot_general(a, b, (((1,), (1,)), ((), ())), preferred_element_type=jnp.float32)


def _rmsnorm(x, w):
    return x * lax.rsqrt(jnp.mean(x * x, axis=-1, keepdims=True) + EPS) * w


def _sigmoid(x):
    return 0.5 + 0.5 * jnp.tanh(0.5 * x)


def _split(x, terms):
    parts = []
    for i in range(terms):
        p = x.astype(MXU_DTYPE)
        parts.append(p)
        if i + 1 < terms:
            x = x - p.astype(jnp.float32)
    return parts


def _select_sum(sel, parts):
    out = _dot(sel, parts[0])
    for p in parts[1:]:
        out = out + _dot(sel, p)
    return out


def _const_spec(shape):
    return pl.BlockSpec(shape, lambda *_: (0,) * len(shape), pipeline_mode=pl.Buffered(1))


_HBM = pl.BlockSpec(memory_space=pl.ANY)


def _chunk_rows(rows, cols):
    r = max(BF16_ROWS, min(rows, WEIGHT_CHUNK_BYTES // (cols * 4)))
    while rows % r or r % BF16_ROWS:
        r -= 1
    return r


def _weight_scratch(rows, cols):
    return [pltpu.VMEM((rows, cols), MXU_DTYPE),
            pltpu.VMEM((2, _chunk_rows(rows, cols), cols), jnp.float32),
            pltpu.SemaphoreType.DMA((2,))]


def _load_weight(w_hbm, w_ref, stage_ref, sem_ref):
    rows = stage_ref.shape[1]
    n = w_hbm.shape[0] // rows

    def chunk_copy(k):
        return pltpu.make_async_copy(w_hbm.at[pl.ds(k * rows, rows)], stage_ref.at[k % 2], sem_ref.at[k % 2])

    chunk_copy(0).start()
    for k in range(n):
        if k + 1 < n:
            chunk_copy(k + 1).start()
        chunk_copy(k).wait()
        w_ref[k * rows:(k + 1) * rows, :] = stage_ref[k % 2].astype(MXU_DTYPE)


def _ffn_kernel(x_ref, nw_ref, wi_ref, wo_ref, fnw_ref, o_ref, act_ref, *, d_ff, final_norm):
    x = x_ref[...]
    hn = _rmsnorm(x, nw_ref[...]).astype(MXU_DTYPE)
    for j in range(d_ff // FFN_TILE):
        lo, hi = j * FFN_TILE, (j + 1) * FFN_TILE
        g = _dot(hn, wi_ref[:, lo:hi])
        u = _dot(hn, wi_ref[:, d_ff + lo:d_ff + hi])
        act_ref[:, lo:hi] = (g * _sigmoid(g) * u).astype(MXU_DTYPE)
    acc = x + _dot(act_ref[...], wo_ref[...])
    if final_norm:
        acc = _rmsnorm(acc, fnw_ref[...])
    o_ref[...] = acc


def _ffn(x2, norm_w, w_in, w_out, final_norm_w, final_norm):
    t, d = x2.shape
    d_ff = w_out.shape[0]
    assert d_ff % FFN_TILE == 0 and t % FFN_BLOCK == 0
    row = pl.BlockSpec((FFN_BLOCK, d), lambda i: (i, 0))
    return pl.pallas_call(
        functools.partial(_ffn_kernel, d_ff=d_ff, final_norm=final_norm),
        grid=(t // FFN_BLOCK,),
        in_specs=[row, _const_spec((1, d)), _const_spec(w_in.shape), _const_spec(w_out.shape),
                  _const_spec((1, d))],
        out_specs=row,
        out_shape=jax.ShapeDtypeStruct((t, d), jnp.float32),
        scratch_shapes=[pltpu.VMEM((FFN_BLOCK, d_ff), MXU_DTYPE)],
        compiler_params=pltpu.CompilerParams(dimension_semantics=("arbitrary",),
                                             vmem_limit_bytes=VMEM_LIMIT_BYTES),
        name="swiglu_ffn",
    )(x2, norm_w.reshape(1, d), w_in, w_out, final_norm_w.reshape(1, d))


class _FfnWeightCast:
    def __init__(self, w_in_all, w_out_all, layer, n_steps):
        self.arrays = [w_in_all, w_out_all]
        self.layer = layer
        self.plan = []
        for w in self.arrays:
            rows, share = w.shape[1], 1
            while (rows * share) % (n_steps * BF16_ROWS):
                share *= 2
                assert share <= n_steps, "weight rows do not split into packed-row blocks"
            self.plan.append((rows * share // n_steps, share))

    def in_specs(self, step_of):
        return [pl.BlockSpec((None, r, w.shape[2]), lambda *g, _k=k: (self.layer, step_of(*g) // _k, 0))
                for w, (r, k) in zip(self.arrays, self.plan)]

    def out_specs(self, step_of):
        return [pl.BlockSpec((r, w.shape[2]), lambda *g, _k=k: (step_of(*g) // _k, 0))
                for w, (r, k) in zip(self.arrays, self.plan)]

    def out_shapes(self):
        return [jax.ShapeDtypeStruct(w.shape[1:], MXU_DTYPE) for w in self.arrays]


def _cast_slices(in_refs, out_refs):
    for i_ref, o_ref in zip(in_refs, out_refs):
        o_ref[...] = i_ref[...].astype(MXU_DTYPE)


def _hgrn_scores_single_ref(qd, kd, heads, dk, sc_ref):
    c = qd.shape[0]
    row = lax.broadcasted_iota(jnp.int32, (c, c), 0)
    col = lax.broadcasted_iota(jnp.int32, (c, c), 1)
    causal = col <= row
    for h in range(heads):
        sl = slice(h * dk, (h + 1) * dk)
        s = _dot_nt(qd[:, sl], kd[:, sl])
        sc_ref[h] = jnp.where(causal, s, 0.0).astype(MXU_DTYPE)


def _hgrn_scores_halving(q_c, k_c, lf_parts, heads, dk, sc_ref):
    c = q_c.shape[0]
    row = lax.broadcasted_iota(jnp.int32, (c, c), 0)
    col = lax.broadcasted_iota(jnp.int32, (c, c), 1)
    acc = []
    for h in range(heads):
        sl = slice(h * dk, (h + 1) * dk)
        s = _dot_nt(q_c[:, sl].astype(MXU_DTYPE), k_c[:, sl].astype(MXU_DTYPE))
        acc.append(jnp.where(row == col, s, 0.0))
    m = c // 2
    while m >= 1:
        blk = 2 * m
        ref_of_row = (row // blk) * blk + (m - 1)
        ref_of_col = (col // blk) * blk + (m - 1)
        row_upper = (row % blk) >= m
        sel_q = (row_upper & (col > ref_of_row) & (col <= row)).astype(MXU_DTYPE)
        sel_k = ((~row_upper) & (col > row) & (col <= ref_of_row)).astype(MXU_DTYPE)
        qd = (q_c * jnp.exp(_select_sum(sel_q, lf_parts))).astype(MXU_DTYPE)
        kd = (k_c * jnp.exp(_select_sum(sel_k, lf_parts))).astype(MXU_DTYPE)
        pair = (ref_of_row == ref_of_col) & row_upper & ((col % blk) < m)
        for h in range(heads):
            sl = slice(h * dk, (h + 1) * dk)
            acc[h] = acc[h] + jnp.where(pair, _dot_nt(qd[:, sl], kd[:, sl]), 0.0)
        m //= 2
    for h in range(heads):
        sc_ref[h] = acc[h].astype(MXU_DTYPE)


def _hgrn_kernel(x_ref, nw_ref, win_hbm, lbp_ref, gnw_ref, wout_hbm, fwi_f32, fwo_f32,
                 o_ref, worst_ref, fwi_ref, fwo_ref, st_ref, proj_ref, k_ref, sc_ref, y_ref,
                 win_ref, win_stage, win_sem, wout_ref, wout_stage, wout_sem,
                 *, layer, heads, dk, single_ref):
    kd_total = heads * dk
    tb = x_ref.shape[0]
    c = HGRN_CHUNK
    n_chunks = tb // c
    first = (pl.program_id(0) == 0) & (pl.program_id(1) == 0)

    @pl.when(pl.program_id(1) == 0)
    def _():
        st_ref[...] = jnp.zeros_like(st_ref)

    @pl.when(first)
    def _():
        worst_ref[...] = jnp.zeros_like(worst_ref)
        _load_weight(win_hbm.at[layer], win_ref, win_stage, win_sem)
        _load_weight(wout_hbm.at[layer], wout_ref, wout_stage, wout_sem)

    _cast_slices((fwi_f32, fwo_f32), (fwi_ref, fwo_ref))
    x = x_ref[...]
    hn = _rmsnorm(x, nw_ref[...]).astype(MXU_DTYPE)
    proj = _dot(hn, win_ref[...])

    lp = lbp_ref[...]
    e = jnp.exp(lp - jnp.max(lp, axis=0, keepdims=True))
    lb = jnp.sum(e[: layer + 1], axis=0, keepdims=True) / jnp.sum(e, axis=0, keepdims=True)
    one_m_lb = 1.0 - lb

    q = proj[:, :kd_total]
    fl = proj[:, kd_total:2 * kd_total]
    g = proj[:, 3 * kd_total:]
    en = jnp.exp(-jnp.abs(fl))
    rn = 1.0 / (1.0 + en)
    small = en * rn
    pos = fl >= 0.0
    kk = one_m_lb * jnp.where(pos, small, rn)
    f = lb + one_m_lb * jnp.where(pos, rn, small)
    log_f = jnp.maximum(jnp.log(f), HGRN_LOG_FLOOR)

    proj_ref[:, :kd_total] = q * _sigmoid(q)
    proj_ref[:, kd_total:2 * kd_total] = log_f
    proj_ref[:, 2 * kd_total:3 * kd_total] = proj[:, 2 * kd_total:3 * kd_total]
    proj_ref[:, 3 * kd_total:] = g * _sigmoid(g)
    k_ref[...] = kk

    row = lax.broadcasted_iota(jnp.int32, (c, c), 0)
    col = lax.broadcasted_iota(jnp.int32, (c, c), 1)
    tril = (col <= row).astype(MXU_DTYPE)
    gnw = gnw_ref[...]

    for ci in range(n_chunks):
        rows = slice(ci * c, (ci + 1) * c)
        q_c = proj_ref[rows, :kd_total]
        lf_parts = _split(proj_ref[rows, kd_total:2 * kd_total], 2 if single_ref else 3)
        k_c = k_ref[rows, :]
        b = _select_sum(tril, lf_parts)
        b_last = b[c - 1:c, :]
        dec = jnp.exp(b_last)
        eb = jnp.exp(b)
        qd = (q_c * eb).astype(MXU_DTYPE)
        if single_ref:
            kd = (k_c * (1.0 / eb)).astype(MXU_DTYPE)
            _hgrn_scores_single_ref(qd, kd, heads, dk, sc_ref)
            ke = kd * dec.astype(MXU_DTYPE)
            folded = b_last
            width = kd_total
            while width > worst_ref.shape[1]:
                width //= 2
                folded = jnp.minimum(folded[:, :width], folded[:, width:])
            worst_ref[...] = jnp.minimum(worst_ref[...], folded)
        else:
            _hgrn_scores_halving(q_c, k_c, lf_parts, heads, dk, sc_ref)
            ke = (k_c * jnp.exp(b_last - b)).astype(MXU_DTYPE)
        for h in range(heads):
            sl = slice(h * dk, (h + 1) * dk)
            v_h = proj_ref[rows, 2 * kd_total + h * dk:2 * kd_total + (h + 1) * dk]
            st = st_ref[h]
            o = _dot(sc_ref[h], v_h.astype(MXU_DTYPE)) + _dot_nt(qd[:, sl], st.astype(MXU_DTYPE))
            st_ref[h] = st * dec[:, sl] + _dot(v_h.T.astype(MXU_DTYPE), ke[:, sl])
            o = o * lax.rsqrt(jnp.mean(o * o, axis=-1, keepdims=True) + EPS) * gnw
            sg = proj_ref[rows, 3 * kd_total + h * dk:3 * kd_total + (h + 1) * dk]
            y_ref[rows, sl] = (o * sg).astype(MXU_DTYPE)

    o_ref[...] = x + _dot(y_ref[...], wout_ref[...])


def _hgrn_mixer(x2, batch, norm_w, w_in_all, lb_param, gn_w, w_out_all, layer, ffn_w_in_all,
                ffn_w_out_all, ffn_layer, single_ref):
    t, d = x2.shape
    dv = gn_w.shape[-1]
    w_in_shape, w_out_shape = w_in_all.shape[1:], w_out_all.shape[1:]
    heads = w_out_shape[0] // dv
    kd_total = (w_in_shape[1] - 2 * heads * dv) // 2
    dk = kd_total // heads
    assert dk == dv and kd_total == heads * dv, "kernel assumes key dim == value dim"
    tb = HGRN_BLOCK
    spb = t // batch // tb
    assert spb * tb * batch == t and tb % HGRN_CHUNK == 0
    step_of = lambda b, s: b * spb + s
    row = pl.BlockSpec((tb, d), lambda b, s: (step_of(b, s), 0))
    cast = _FfnWeightCast(ffn_w_in_all, ffn_w_out_all, ffn_layer, batch * spb)
    return pl.pallas_call(
        functools.partial(_hgrn_kernel, layer=layer, heads=heads, dk=dk, single_ref=single_ref),
        grid=(batch, spb),
        in_specs=[row, _const_spec((1, d)), _HBM, _const_spec(lb_param.shape),
                  _const_spec((1, dv)), _HBM] + cast.in_specs(step_of),
        out_specs=[row, pl.BlockSpec((1, dk), lambda b, s: (0, 0))] + cast.out_specs(step_of),
        out_shape=[jax.ShapeDtypeStruct((t, d), jnp.float32),
                   jax.ShapeDtypeStruct((1, dk), jnp.float32)] + cast.out_shapes(),
        scratch_shapes=[
            pltpu.VMEM((heads, dv, dk), jnp.float32),
            pltpu.VMEM((tb, w_in_shape[1]), jnp.float32),
            pltpu.VMEM((tb, kd_total), jnp.float32),
            pltpu.VMEM((heads, HGRN_CHUNK, HGRN_CHUNK), MXU_DTYPE),
            pltpu.VMEM((tb, heads * dv), MXU_DTYPE),
        ] + _weight_scratch(*w_in_shape) + _weight_scratch(*w_out_shape),
        compiler_params=pltpu.CompilerParams(dimension_semantics=("arbitrary", "arbitrary"),
                                             vmem_limit_bytes=VMEM_LIMIT_BYTES),
        name="hgrn2_mixer" if single_ref else "hgrn2_mixer_any_decay",
    )(x2, norm_w.reshape(1, d), w_in_all, lb_param, gn_w.reshape(1, dv), w_out_all, *cast.arrays)


def _gelu_tanh(x):
    c = math.sqrt(2.0 / math.pi)
    return (0.5 * x) * (1.0 + jnp.tanh(x * (c + (c * 0.044715) * (x * x))))


def _scan_rows(a, b, n):
    rows = lax.broadcasted_iota(jnp.int32, a.shape, 0)
    d = 1
    while d < n:
        live = rows >= d
        a_sh = jnp.where(live, pltpu.roll(a, d, axis=0), 1.0)
        b_sh = jnp.where(live, pltpu.roll(b, d, axis=0), 0.0)
        b = a * b_sh + b
        a = a * a_sh
        d *= 2
    return a, b


def _scan_level(levels, lvl, j, h0, grouped=False):
    a_ref, b_ref, h_ref = levels[lvl]
    n = a_ref.shape[1]
    if lvl + 1 == len(levels) or n % SCAN_RADIX or n <= SUBLANES:
        a_inc, b_inc = _scan_rows(a_ref[j], b_ref[j], n)
        h_ref[j] = a_inc * h0 + b_inc
        return
    m = n // SCAN_RADIX
    step = lambda r: pl.ds(r, m, stride=SCAN_RADIX)
    read = (lambda r: pl.ds(r * m, m)) if grouped else step
    a_cum = [a_ref[j, read(0), :]]
    h_loc = [b_ref[j, read(0), :]]
    for r in range(1, SCAN_RADIX):
        a_r = a_ref[j, read(r), :]
        h_loc.append(a_r * h_loc[-1] + b_ref[j, read(r), :])
        a_cum.append(a_r * a_cum[-1])
    a_next, b_next, h_next = levels[lvl + 1]
    a_next[j] = a_cum[-1]
    b_next[j] = h_loc[-1]
    _scan_level(levels, lvl + 1, j, h0)
    group = lax.broadcasted_iota(jnp.int32, (m, LANES), 0)
    h_in = jnp.where(group >= 1, pltpu.roll(h_next[j], 1, axis=0), h0)
    for r in range(SCAN_RADIX):
        h_ref[j, step(r), :] = h_loc[r] + a_cum[r] * h_in


def _conv_grouped(u, ext_ref, cw_ref, cb_ref):
    tb, width = u.shape
    taps = cw_ref.shape[0]
    pad = ext_ref.shape[1] - tb
    m = tb // SCAN_RADIX
    cw = cw_ref[...]
    cb = cb_ref[...]
    tiles = []
    for j in range(width // LANES):
        cols = slice(j * LANES, (j + 1) * LANES)
        ext_ref[j, pad:, :] = u[:, cols]
        steps = []
        for r in range(SCAN_RADIX):
            acc = cb[:, cols]
            for t in range(taps):
                shifted = ext_ref[j, pl.ds(pad + r - t, m, stride=SCAN_RADIX), :]
                acc = acc + shifted * cw[taps - 1 - t:taps - t, cols]
            steps.append(acc)
        ext_ref[j, :pad, :] = u[tb - pad:, cols]
        tiles.append(jnp.concatenate(steps, axis=0))
    return jnp.concatenate(tiles, axis=-1)


def _lru_kernel(x_ref, nw_ref, win_hbm, cw_ref, cb_ref, wa_hbm, ba_ref, wx_hbm, bx_ref, lam_ref,
                wout_hbm, fwi_f32, fwo_f32, o_ref, fwi_ref, fwo_ref, h_ref, ext_ref, a_ref, b_ref, hs_ref,
                a1_ref, b1_ref, h1_ref, a2_ref, b2_ref, h2_ref, a3_ref, b3_ref, h3_ref,
                win_ref, win_stage, win_sem, wa_ref, wa_stage, wa_sem, wx_ref, wx_stage, wx_sem,
                wout_ref, wout_stage, wout_sem, *, layer, width, n_blocks):
    tb = x_ref.shape[0]
    bw = width // n_blocks

    @pl.when((pl.program_id(0) == 0) & (pl.program_id(1) == 0))
    def _():
        _load_weight(win_hbm.at[layer], win_ref, win_stage, win_sem)
        _load_weight(wa_hbm.at[layer], wa_ref, wa_stage, wa_sem)
        _load_weight(wx_hbm.at[layer], wx_ref, wx_stage, wx_sem)
        _load_weight(wout_hbm.at[layer], wout_ref, wout_stage, wout_sem)

    @pl.when(pl.program_id(1) == 0)
    def _():
        h_ref[...] = jnp.zeros_like(h_ref)
        ext_ref[:, :SUBLANES, :] = jnp.zeros((ext_ref.shape[0], SUBLANES, LANES), jnp.float32)

    _cast_slices((fwi_f32, fwo_f32), (fwi_ref, fwo_ref))
    x = x_ref[...]
    hn = _rmsnorm(x, nw_ref[...]).astype(MXU_DTYPE)
    proj = _dot(hn, win_ref[...])
    y = _gelu_tanh(proj[:, :width])
    u = proj[:, width:]

    uc = _conv_grouped(u, ext_ref, cw_ref, cb_ref)

    ra, rx = [], []
    for n in range(n_blocks):
        ub = uc[:, n * bw:(n + 1) * bw].astype(MXU_DTYPE)
        ra.append(_dot(ub, wa_ref[n * bw:(n + 1) * bw, :]))
        rx.append(_dot(ub, wx_ref[n * bw:(n + 1) * bw, :]))
    r = _sigmoid(jnp.concatenate(ra, axis=-1) + ba_ref[...])
    ig = _sigmoid(jnp.concatenate(rx, axis=-1) + bx_ref[...])

    nl = -lam_ref[...]
    softplus = jnp.maximum(nl, 0.0) + jnp.log1p(jnp.exp(-jnp.abs(nl)))
    rate = (-LRU_C) * softplus
    log_a = rate * r
    a = jnp.exp2((rate * LOG2E) * r)
    z = -jnp.tanh(log_a) * (a * a + 1.0)
    b_in = jnp.where(z > 0.0, z * lax.rsqrt(z), 0.0) * (ig * uc)

    levels = [(a_ref, b_ref, hs_ref), (a1_ref, b1_ref, h1_ref), (a2_ref, b2_ref, h2_ref),
              (a3_ref, b3_ref, h3_ref)]
    n_tiles = width // LANES
    for j in range(n_tiles):
        cols = slice(j * LANES, (j + 1) * LANES)
        a_ref[j] = a[:, cols]
        b_ref[j] = b_in[:, cols]
    for j in range(n_tiles):
        cols = slice(j * LANES, (j + 1) * LANES)
        _scan_level(levels, 0, j, h_ref[:, cols], grouped=True)
        h_ref[:, cols] = hs_ref[j, tb - 1:tb, :]
    hs = jnp.concatenate([hs_ref[j] for j in range(n_tiles)], axis=-1)
    hy = (hs * y).astype(MXU_DTYPE)

    o_ref[...] = x + _dot(hy, wout_ref[...])


def _lru_mixer(x2, batch, norm_w, w_in_all, conv_w, conv_b, wa_all, ba, wx_all, bx, lam, w_out_all,
               layer, ffn_w_in_all, ffn_w_out_all, ffn_layer):
    t, d = x2.shape
    width = w_out_all.shape[1]
    n_blocks, bw = wa_all.shape[1], wa_all.shape[2]
    taps = conv_w.shape[0]
    wa_all = wa_all.reshape(wa_all.shape[0], n_blocks * bw, bw)
    wx_all = wx_all.reshape(wx_all.shape[0], n_blocks * bw, bw)
    tb = LRU_BLOCK
    spb = t // batch // tb
    assert spb * tb * batch == t and taps - 1 <= SUBLANES and tb % (SUBLANES * SCAN_RADIX) == 0
    step_of = lambda b, s: b * spb + s
    row = pl.BlockSpec((tb, d), lambda b, s: (step_of(b, s), 0))
    vec = lambda v: v.reshape(1, width)
    cast = _FfnWeightCast(ffn_w_in_all, ffn_w_out_all, ffn_layer, batch * spb)
    return pl.pallas_call(
        functools.partial(_lru_kernel, layer=layer, width=width, n_blocks=n_blocks),
        grid=(batch, spb),
        in_specs=[row, _const_spec((1, d)), _HBM, _const_spec(conv_w.shape),
                  _const_spec((1, width)), _HBM, _const_spec((1, width)),
                  _HBM, _const_spec((1, width)), _const_spec((1, width)), _HBM]
        + cast.in_specs(step_of),
        out_specs=[row] + cast.out_specs(step_of),
        out_shape=[jax.ShapeDtypeStruct((t, d), jnp.float32)] + cast.out_shapes(),
        scratch_shapes=[
            pltpu.VMEM((1, width), jnp.float32),
            pltpu.VMEM((width // LANES, SUBLANES + tb, LANES), jnp.float32),
        ] + [pltpu.VMEM((width // LANES, tb // SCAN_RADIX ** lvl, LANES), jnp.float32)
             for lvl in range(SCAN_LEVELS) for _ in "abh"]
        + _weight_scratch(*w_in_all.shape[1:]) + _weight_scratch(*wa_all.shape[1:])
        + _weight_scratch(*wx_all.shape[1:]) + _weight_scratch(*w_out_all.shape[1:]),
        compiler_params=pltpu.CompilerParams(dimension_semantics=("arbitrary", "arbitrary"),
                                             vmem_limit_bytes=VMEM_LIMIT_BYTES),
        name="rglru_mixer",
    )(x2, norm_w.reshape(1, d), w_in_all, conv_w, vec(conv_b), wa_all, vec(ba), wx_all, vec(bx),
      vec(lam), w_out_all, *cast.arrays)


def kernel(x, norm_mix, norm_ffn, norm_final, hgrn_w_in, hgrn_lb, hgrn_norm, hgrn_w_out, lru_w_in, lru_conv_w, lru_conv_b, lru_wa, lru_ba, lru_wx, lru_bx, lru_lambda, lru_w_out, ffn_w_in, ffn_w_out):
    batch, seq, d = x.shape
    depth = norm_mix.shape[0]
    n_mixers = 2

    def ffn(l, h, w_in, w_out):
        return _ffn(h, norm_ffn[l], w_in, w_out, norm_final, final_norm=(l == depth - 1))

    def layers_from(l, h):
        if l == depth:
            return h
        j = l // n_mixers
        if l % n_mixers == 1:
            mixed = _lru_mixer(h, batch, norm_mix[l], lru_w_in, lru_conv_w[j], lru_conv_b[j],
                               lru_wa, lru_ba[j], lru_wx, lru_bx[j], lru_lambda[j], lru_w_out, j,
                               ffn_w_in, ffn_w_out, l)
            return layers_from(l + 1, ffn(l, *mixed))
        hgrn = functools.partial(_hgrn_mixer, h, batch, norm_mix[l], hgrn_w_in, hgrn_lb,
                                 hgrn_norm[j], hgrn_w_out, j, ffn_w_in, ffn_w_out, l)
        mixed, worst, w_in, w_out = hgrn(single_ref=True)

        def any_decay():
            mixed, _, w_in, w_out = hgrn(single_ref=False)
            return layers_from(l + 1, ffn(l, mixed, w_in, w_out))

        return lax.cond(jnp.min(worst) >= -HGRN_SAFE_EXP,
                        lambda: layers_from(l + 1, ffn(l, mixed, w_in, w_out)), any_decay)

    return layers_from(0, x.reshape(batch * seq, d)).reshape(batch, seq, d)
```

```python
import functools
import math

import jax
import jax.numpy as jnp
from jax import lax
from jax.experimental import pallas as pl
from jax.experimental.pallas import tpu as pltpu

EPS = 1e-6
LRU_C = 8.0
LOG2E = math.log2(math.e)
MXU_DTYPE = jnp.bfloat16
SUBLANES = 8
LANES = 128
SCAN_RADIX = 4
SCAN_LEVELS = 4

HGRN_CHUNK = 128
HGRN_SAFE_EXP = 80.0
HGRN_LOG_FLOOR = -1e4
HGRN_BLOCK = 512
LRU_BLOCK = 512
FFN_BLOCK = 512
FFN_TILE = 256
VMEM_LIMIT_BYTES = 56 * 1024 * 1024
BF16_ROWS = 16
WEIGHT_CHUNK_BYTES = 2 * 1024 * 1024


def _dot(a, b):
    return jnp.dot(a, b, preferred_element_type=jnp.float32)


def _dot_nt(a, b):
    return lax.dot_general(a, b, (((1,), (1,)), ((), ())), preferred_element_type=jnp.float32)


def _rmsnorm(x, w):
    return x * lax.rsqrt(jnp.mean(x * x, axis=-1, keepdims=True) + EPS) * w


def _sigmoid(x):
    return 0.5 + 0.5 * jnp.tanh(0.5 * x)


def _split(x, terms):
    parts = []
    for i in range(terms):
        p = x.astype(MXU_DTYPE)
        parts.append(p)
        if i + 1 < terms:
            x = x - p.astype(jnp.float32)
    return parts


def _select_sum(sel, parts):
    out = _dot(sel, parts[0])
    for p in parts[1:]:
        out = out + _dot(sel, p)
    return out


def _const_spec(shape):
    return pl.BlockSpec(shape, lambda *_: (0,) * len(shape), pipeline_mode=pl.Buffered(1))


_HBM = pl.BlockSpec(memory_space=pl.ANY)


def _chunk_rows(rows, cols):
    r = max(BF16_ROWS, min(rows, WEIGHT_CHUNK_BYTES // (cols * 4)))
    while rows % r or r % BF16_ROWS:
        r -= 1
    return r


def _weight_scratch(rows, cols):
    return [pltpu.VMEM((rows, cols), MXU_DTYPE),
            pltpu.VMEM((2, _chunk_rows(rows, cols), cols), jnp.float32),
            pltpu.SemaphoreType.DMA((2,))]


def _load_weight(w_hbm, w_ref, stage_ref, sem_ref):
    rows = stage_ref.shape[1]
    n = w_hbm.shape[0] // rows

    def chunk_copy(k):
        return pltpu.make_async_copy(w_hbm.at[pl.ds(k * rows, rows)], stage_ref.at[k % 2], sem_ref.at[k % 2])

    chunk_copy(0).start()
    for k in range(n):
        if k + 1 < n:
            chunk_copy(k + 1).start()
        chunk_copy(k).wait()
        w_ref[k * rows:(k + 1) * rows, :] = stage_ref[k % 2].astype(MXU_DTYPE)


def _ffn_kernel(x_ref, nw_ref, wi_ref, wo_ref, fnw_ref, o_ref, act_ref, *, d_ff, final_norm):
    x = x_ref[...]
    hn = _rmsnorm(x, nw_ref[...]).astype(MXU_DTYPE)
    for j in range(d_ff // FFN_TILE):
        lo, hi = j * FFN_TILE, (j + 1) * FFN_TILE
        g = _dot(hn, wi_ref[:, lo:hi])
        u = _dot(hn, wi_ref[:, d_ff + lo:d_ff + hi])
        act_ref[:, lo:hi] = (g * _sigmoid(g) * u).astype(MXU_DTYPE)
    acc = x + _dot(act_ref[...], wo_ref[...])
    if final_norm:
        acc = _rmsnorm(acc, fnw_ref[...])
    o_ref[...] = acc


def _ffn(x2, norm_w, w_in, w_out, final_norm_w, final_norm):
    t, d = x2.shape
    d_ff = w_out.shape[0]
    assert d_ff % FFN_TILE == 0 and t % FFN_BLOCK == 0
    row = pl.BlockSpec((FFN_BLOCK, d), lambda i: (i, 0))
    return pl.pallas_call(
        functools.partial(_ffn_kernel, d_ff=d_ff, final_norm=final_norm),
        grid=(t // FFN_BLOCK,),
        in_specs=[row, _const_spec((1, d)), _const_spec(w_in.shape), _const_spec(w_out.shape),
                  _const_spec((1, d))],
        out_specs=row,
        out_shape=jax.ShapeDtypeStruct((t, d), jnp.float32),
        scratch_shapes=[pltpu.VMEM((FFN_BLOCK, d_ff), MXU_DTYPE)],
        compiler_params=pltpu.CompilerParams(dimension_semantics=("arbitrary",),
                                             vmem_limit_bytes=VMEM_LIMIT_BYTES),
        name="swiglu_ffn",
    )(x2, norm_w.reshape(1, d), w_in, w_out, final_norm_w.reshape(1, d))


class _FfnWeightCast:
    def __init__(self, w_in_all, w_out_all, layer, n_steps):
        self.arrays = [w_in_all, w_out_all]
        self.layer = layer
        self.plan = []
        for w in self.arrays:
            rows, share = w.shape[1], 1
            while (rows * share) % (n_steps * BF16_ROWS):
                share *= 2
                assert share <= n_steps, "weight rows do not split into packed-row blocks"
            self.plan.append((rows * share // n_steps, share))

    def in_specs(self, step_of):
        return [pl.BlockSpec((None, r, w.shape[2]), lambda *g, _k=k: (self.layer, step_of(*g) // _k, 0))
                for w, (r, k) in zip(self.arrays, self.plan)]

    def out_specs(self, step_of):
        return [pl.BlockSpec((r, w.shape[2]), lambda *g, _k=k: (step_of(*g) // _k, 0))
                for w, (r, k) in zip(self.arrays, self.plan)]

    def out_shapes(self):
        return [jax.ShapeDtypeStruct(w.shape[1:], MXU_DTYPE) for w in self.arrays]


def _cast_slices(in_refs, out_refs):
    for i_ref, o_ref in zip(in_refs, out_refs):
        o_ref[...] = i_ref[...].astype(MXU_DTYPE)


def _hgrn_scores_single_ref(qd, kd, heads, dk, sc_ref):
    c = qd.shape[0]
    row = lax.broadcasted_iota(jnp.int32, (c, c), 0)
    col = lax.broadcasted_iota(jnp.int32, (c, c), 1)
    causal = col <= row
    for h in range(heads):
        sl = slice(h * dk, (h + 1) * dk)
        s = _dot_nt(qd[:, sl], kd[:, sl])
        sc_ref[h] = jnp.where(causal, s, 0.0).astype(MXU_DTYPE)


def _hgrn_scores_halving(q_c, k_c, lf_parts, heads, dk, sc_ref):
    c = q_c.shape[0]
    row = lax.broadcasted_iota(jnp.int32, (c, c), 0)
    col = lax.broadcasted_iota(jnp.int32, (c, c), 1)
    acc = []
    for h in range(heads):
        sl = slice(h * dk, (h + 1) * dk)
        s = _dot_nt(q_c[:, sl].astype(MXU_DTYPE), k_c[:, sl].astype(MXU_DTYPE))
        acc.append(jnp.where(row == col, s, 0.0))
    m = c // 2
    while m >= 1:
        blk = 2 * m
        ref_of_row = (row // blk) * blk + (m - 1)
        ref_of_col = (col // blk) * blk + (m - 1)
        row_upper = (row % blk) >= m
        sel_q = (row_upper & (col > ref_of_row) & (col <= row)).astype(MXU_DTYPE)
        sel_k = ((~row_upper) & (col > row) & (col <= ref_of_row)).astype(MXU_DTYPE)
        qd = (q_c * jnp.exp(_select_sum(sel_q, lf_parts))).astype(MXU_DTYPE)
        kd = (k_c * jnp.exp(_select_sum(sel_k, lf_parts))).astype(MXU_DTYPE)
        pair = (ref_of_row == ref_of_col) & row_upper & ((col % blk) < m)
        for h in range(heads):
            sl = slice(h * dk, (h + 1) * dk)
            acc[h] = acc[h] + jnp.where(pair, _dot_nt(qd[:, sl], kd[:, sl]), 0.0)
        m //= 2
    for h in range(heads):
        sc_ref[h] = acc[h].astype(MXU_DTYPE)


def _hgrn_kernel(x_ref, nw_ref, win_hbm, lbp_ref, gnw_ref, wout_hbm, fwi_f32, fwo_f32,
                 o_ref, worst_ref, fwi_ref, fwo_ref, st_ref, proj_ref, k_ref, sc_ref, y_ref,
                 win_ref, win_stage, win_sem, wout_ref, wout_stage, wout_sem,
                 *, layer, heads, dk, single_ref):
    kd_total = heads * dk
    tb = x_ref.shape[0]
    c = HGRN_CHUNK
    n_chunks = tb // c
    first = (pl.program_id(0) == 0) & (pl.program_id(1) == 0)

    @pl.when(pl.program_id(1) == 0)
    def _():
        st_ref[...] = jnp.zeros_like(st_ref)

    @pl.when(first)
    def _():
        worst_ref[...] = jnp.zeros_like(worst_ref)
        _load_weight(win_hbm.at[layer], win_ref, win_stage, win_sem)
        _load_weight(wout_hbm.at[layer], wout_ref, wout_stage, wout_sem)

    _cast_slices((fwi_f32, fwo_f32), (fwi_ref, fwo_ref))
    x = x_ref[...]
    hn = _rmsnorm(x, nw_ref[...]).astype(MXU_DTYPE)
    proj = _dot(hn, win_ref[...])

    lp = lbp_ref[...]
    e = jnp.exp(lp - jnp.max(lp, axis=0, keepdims=True))
    lb = jnp.sum(e[: layer + 1], axis=0, keepdims=True) / jnp.sum(e, axis=0, keepdims=True)
    one_m_lb = 1.0 - lb

    q = proj[:, :kd_total]
    fl = proj[:, kd_total:2 * kd_total]
    g = proj[:, 3 * kd_total:]
    half_t = 0.5 * jnp.tanh(0.5 * fl)
    kk = one_m_lb * (0.5 - half_t)
    f = lb + one_m_lb * (0.5 + half_t)
    log_f = jnp.maximum(jnp.log(f), HGRN_LOG_FLOOR)

    proj_ref[:, :kd_total] = q * _sigmoid(q)
    proj_ref[:, kd_total:2 * kd_total] = log_f
    proj_ref[:, 2 * kd_total:3 * kd_total] = proj[:, 2 * kd_total:3 * kd_total]
    proj_ref[:, 3 * kd_total:] = g * _sigmoid(g)
    k_ref[...] = kk

    row = lax.broadcasted_iota(jnp.int32, (c, c), 0)
    col = lax.broadcasted_iota(jnp.int32, (c, c), 1)
    tril = (col <= row).astype(MXU_DTYPE)
    gnw = gnw_ref[...]

    for ci in range(n_chunks):
        rows = slice(ci * c, (ci + 1) * c)
        q_c = proj_ref[rows, :kd_total]
        lf_parts = _split(proj_ref[rows, kd_total:2 * kd_total], 2 if single_ref else 3)
        k_c = k_ref[rows, :]
        b = _select_sum(tril, lf_parts)
        b_last = b[c - 1:c, :]
        dec = jnp.exp(b_last)
        eb = jnp.exp(b)
        qd = (q_c * eb).astype(MXU_DTYPE)
        if single_ref:
            kd = (k_c * (1.0 / eb)).astype(MXU_DTYPE)
            _hgrn_scores_single_ref(qd, kd, heads, dk, sc_ref)
            ke = kd * dec.astype(MXU_DTYPE)
            folded = b_last
            width = kd_total
            while width > worst_ref.shape[1]:
                width //= 2
                folded = jnp.minimum(folded[:, :width], folded[:, width:])
            worst_ref[...] = jnp.minimum(worst_ref[...], folded)
        else:
            _hgrn_scores_halving(q_c, k_c, lf_parts, heads, dk, sc_ref)
            ke = (k_c * jnp.exp(b_last - b)).astype(MXU_DTYPE)
        for h in range(heads):
            sl = slice(h * dk, (h + 1) * dk)
            v_h = proj_ref[rows, 2 * kd_total + h * dk:2 * kd_total + (h + 1) * dk]
            st = st_ref[h]
            o = _dot(sc_ref[h], v_h.astype(MXU_DTYPE)) + _dot_nt(qd[:, sl], st.astype(MXU_DTYPE))
            st_ref[h] = st * dec[:, sl] + _dot(v_h.T.astype(MXU_DTYPE), ke[:, sl])
            o = o * lax.rsqrt(jnp.mean(o * o, axis=-1, keepdims=True) + EPS) * gnw
            sg = proj_ref[rows, 3 * kd_total + h * dk:3 * kd_total + (h + 1) * dk]
            y_ref[rows, sl] = (o * sg).astype(MXU_DTYPE)

    o_ref[...] = x + _dot(y_ref[...], wout_ref[...])


def _hgrn_mixer(x2, batch, norm_w, w_in_all, lb_param, gn_w, w_out_all, layer, ffn_w_in_all,
                ffn_w_out_all, ffn_layer, single_ref):
    t, d = x2.shape
    dv = gn_w.shape[-1]
    w_in_shape, w_out_shape = w_in_all.shape[1:], w_out_all.shape[1:]
    heads = w_out_shape[0] // dv
    kd_total = (w_in_shape[1] - 2 * heads * dv) // 2
    dk = kd_total // heads
    assert dk == dv and kd_total == heads * dv, "kernel assumes key dim == value dim"
    tb = HGRN_BLOCK
    spb = t // batch // tb
    assert spb * tb * batch == t and tb % HGRN_CHUNK == 0
    step_of = lambda b, s: b * spb + s
    row = pl.BlockSpec((tb, d), lambda b, s: (step_of(b, s), 0))
    cast = _FfnWeightCast(ffn_w_in_all, ffn_w_out_all, ffn_layer, batch * spb)
    return pl.pallas_call(
        functools.partial(_hgrn_kernel, layer=layer, heads=heads, dk=dk, single_ref=single_ref),
        grid=(batch, spb),
        in_specs=[row, _const_spec((1, d)), _HBM, _const_spec(lb_param.shape),
                  _const_spec((1, dv)), _HBM] + cast.in_specs(step_of),
        out_specs=[row, pl.BlockSpec((1, dk), lambda b, s: (0, 0))] + cast.out_specs(step_of),
        out_shape=[jax.ShapeDtypeStruct((t, d), jnp.float32),
                   jax.ShapeDtypeStruct((1, dk), jnp.float32)] + cast.out_shapes(),
        scratch_shapes=[
            pltpu.VMEM((heads, dv, dk), jnp.float32),
            pltpu.VMEM((tb, w_in_shape[1]), jnp.float32),
            pltpu.VMEM((tb, kd_total), jnp.float32),
            pltpu.VMEM((heads, HGRN_CHUNK, HGRN_CHUNK), MXU_DTYPE),
            pltpu.VMEM((tb, heads * dv), MXU_DTYPE),
        ] + _weight_scratch(*w_in_shape) + _weight_scratch(*w_out_shape),
        compiler_params=pltpu.CompilerParams(dimension_semantics=("arbitrary", "arbitrary"),
                                             vmem_limit_bytes=VMEM_LIMIT_BYTES),
        name="hgrn2_mixer" if single_ref else "hgrn2_mixer_any_decay",
    )(x2, norm_w.reshape(1, d), w_in_all, lb_param, gn_w.reshape(1, dv), w_out_all, *cast.arrays)


def _gelu_tanh(x):
    c = math.sqrt(2.0 / math.pi)
    return (0.5 * x) * (1.0 + jnp.tanh(x * (c + (c * 0.044715) * (x * x))))


def _scan_rows(a, b, n):
    rows = lax.broadcasted_iota(jnp.int32, a.shape, 0)
    d = 1
    while d < n:
        live = rows >= d
        a_sh = jnp.where(live, pltpu.roll(a, d, axis=0), 1.0)
        b_sh = jnp.where(live, pltpu.roll(b, d, axis=0), 0.0)
        b = a * b_sh + b
        a = a * a_sh
        d *= 2
    return a, b


def _scan_level(levels, lvl, j, h0, grouped=False):
    a_ref, b_ref, h_ref = levels[lvl]
    n = a_ref.shape[1]
    if lvl + 1 == len(levels) or n % SCAN_RADIX or n <= SUBLANES:
        a_inc, b_inc = _scan_rows(a_ref[j], b_ref[j], n)
        h_ref[j] = a_inc * h0 + b_inc
        return
    m = n // SCAN_RADIX
    step = lambda r: pl.ds(r, m, stride=SCAN_RADIX)
    read = (lambda r: pl.ds(r * m, m)) if grouped else step
    a_cum = [a_ref[j, read(0), :]]
    h_loc = [b_ref[j, read(0), :]]
    for r in range(1, SCAN_RADIX):
        a_r = a_ref[j, read(r), :]
        h_loc.append(a_r * h_loc[-1] + b_ref[j, read(r), :])
        a_cum.append(a_r * a_cum[-1])
    a_next, b_next, h_next = levels[lvl + 1]
    a_next[j] = a_cum[-1]
    b_next[j] = h_loc[-1]
    _scan_level(levels, lvl + 1, j, h0)
    group = lax.broadcasted_iota(jnp.int32, (m, LANES), 0)
    h_in = jnp.where(group >= 1, pltpu.roll(h_next[j], 1, axis=0), h0)
    for r in range(SCAN_RADIX):
        h_ref[j, step(r), :] = h_loc[r] + a_cum[r] * h_in


def _conv_grouped(u, ext_ref, cw_ref, cb_ref):
    tb, width = u.shape
    taps = cw_ref.shape[0]
    pad = ext_ref.shape[1] - tb
    m = tb // SCAN_RADIX
    cw = cw_ref[...]
    cb = cb_ref[...]
    tiles = []
    for j in range(width // LANES):
        cols = slice(j * LANES, (j + 1) * LANES)
        ext_ref[j, pad:, :] = u[:, cols]
        steps = []
        for r in range(SCAN_RADIX):
            acc = cb[:, cols]
            for t in range(taps):
                shifted = ext_ref[j, pl.ds(pad + r - t, m, stride=SCAN_RADIX), :]
                acc = acc + shifted * cw[taps - 1 - t:taps - t, cols]
            steps.append(acc)
        ext_ref[j, :pad, :] = u[tb - pad:, cols]
        tiles.append(jnp.concatenate(steps, axis=0))
    return jnp.concatenate(tiles, axis=-1)


def _lru_kernel(x_ref, nw_ref, win_hbm, cw_ref, cb_ref, wa_hbm, ba_ref, wx_hbm, bx_ref, lam_ref,
                wout_hbm, fwi_f32, fwo_f32, o_ref, fwi_ref, fwo_ref, h_ref, ext_ref, a_ref, b_ref, hs_ref,
                a1_ref, b1_ref, h1_ref, a2_ref, b2_ref, h2_ref, a3_ref, b3_ref, h3_ref,
                win_ref, win_stage, win_sem, wa_ref, wa_stage, wa_sem, wx_ref, wx_stage, wx_sem,
                wout_ref, wout_stage, wout_sem, *, layer, width, n_blocks):
    tb = x_ref.shape[0]
    bw = width // n_blocks

    @pl.when((pl.program_id(0) == 0) & (pl.program_id(1) == 0))
    def _():
        _load_weight(win_hbm.at[layer], win_ref, win_stage, win_sem)
        _load_weight(wa_hbm.at[layer], wa_ref, wa_stage, wa_sem)
        _load_weight(wx_hbm.at[layer], wx_ref, wx_stage, wx_sem)
        _load_weight(wout_hbm.at[layer], wout_ref, wout_stage, wout_sem)

    @pl.when(pl.program_id(1) == 0)
    def _():
        h_ref[...] = jnp.zeros_like(h_ref)
        ext_ref[:, :SUBLANES, :] = jnp.zeros((ext_ref.shape[0], SUBLANES, LANES), jnp.float32)

    _cast_slices((fwi_f32, fwo_f32), (fwi_ref, fwo_ref))
    x = x_ref[...]
    hn = _rmsnorm(x, nw_ref[...]).astype(MXU_DTYPE)
    proj = _dot(hn, win_ref[...])
    y = _gelu_tanh(proj[:, :width])
    u = proj[:, width:]

    uc = _conv_grouped(u, ext_ref, cw_ref, cb_ref)

    ra, rx = [], []
    for n in range(n_blocks):
        ub = uc[:, n * bw:(n + 1) * bw].astype(MXU_DTYPE)
        ra.append(_dot(ub, wa_ref[n * bw:(n + 1) * bw, :]))
        rx.append(_dot(ub, wx_ref[n * bw:(n + 1) * bw, :]))
    r = _sigmoid(jnp.concatenate(ra, axis=-1) + ba_ref[...])
    ig = _sigmoid(jnp.concatenate(rx, axis=-1) + bx_ref[...])

    nl = -lam_ref[...]
    softplus = jnp.maximum(nl, 0.0) + jnp.log1p(jnp.exp(-jnp.abs(nl)))
    rate = (-LRU_C) * softplus
    log_a = rate * r
    a = jnp.exp2((rate * LOG2E) * r)
    z = -jnp.tanh(log_a) * (a * a + 1.0)
    b_in = jnp.where(z > 0.0, z * lax.rsqrt(z), 0.0) * (ig * uc)

    levels = [(a_ref, b_ref, hs_ref), (a1_ref, b1_ref, h1_ref), (a2_ref, b2_ref, h2_ref),
              (a3_ref, b3_ref, h3_ref)]
    n_tiles = width // LANES
    for j in range(n_tiles):
        cols = slice(j * LANES, (j + 1) * LANES)
        a_ref[j] = a[:, cols]
        b_ref[j] = b_in[:, cols]
    for j in range(n_tiles):
        cols = slice(j * LANES, (j + 1) * LANES)
        _scan_level(levels, 0, j, h_ref[:, cols], grouped=True)
        h_ref[:, cols] = hs_ref[j, tb - 1:tb, :]
    hs = jnp.concatenate([hs_ref[j] for j in range(n_tiles)], axis=-1)
    hy = (hs * y).astype(MXU_DTYPE)

    o_ref[...] = x + _dot(hy, wout_ref[...])


def _lru_mixer(x2, batch, norm_w, w_in_all, conv_w, conv_b, wa_all, ba, wx_all, bx, lam, w_out_all,
               layer, ffn_w_in_all, ffn_w_out_all, ffn_layer):
    t, d = x2.shape
    width = w_out_all.shape[1]
    n_blocks, bw = wa_all.shape[1], wa_all.shape[2]
    taps = conv_w.shape[0]
    wa_all = wa_all.reshape(wa_all.shape[0], n_blocks * bw, bw)
    wx_all = wx_all.reshape(wx_all.shape[0], n_blocks * bw, bw)
    tb = LRU_BLOCK
    spb = t // batch // tb
    assert spb * tb * batch == t and taps - 1 <= SUBLANES and tb % (SUBLANES * SCAN_RADIX) == 0
    step_of = lambda b, s: b * spb + s
    row = pl.BlockSpec((tb, d), lambda b, s: (step_of(b, s), 0))
    vec = lambda v: v.reshape(1, width)
    cast = _FfnWeightCast(ffn_w_in_all, ffn_w_out_all, ffn_layer, batch * spb)
    return pl.pallas_call(
        functools.partial(_lru_kernel, layer=layer, width=width, n_blocks=n_blocks),
        grid=(batch, spb),
        in_specs=[row, _const_spec((1, d)), _HBM, _const_spec(conv_w.shape),
                  _const_spec((1, width)), _HBM, _const_spec((1, width)),
                  _HBM, _const_spec((1, width)), _const_spec((1, width)), _HBM]
        + cast.in_specs(step_of),
        out_specs=[row] + cast.out_specs(step_of),
        out_shape=[jax.ShapeDtypeStruct((t, d), jnp.float32)] + cast.out_shapes(),
        scratch_shapes=[
            pltpu.VMEM((1, width), jnp.float32),
            pltpu.VMEM((width // LANES, SUBLANES + tb, LANES), jnp.float32),
        ] + [pltpu.VMEM((width // LANES, tb // SCAN_RADIX ** lvl, LANES), jnp.float32)
             for lvl in range(SCAN_LEVELS) for _ in "abh"]
        + _weight_scratch(*w_in_all.shape[1:]) + _weight_scratch(*wa_all.shape[1:])
        + _weight_scratch(*wx_all.shape[1:]) + _weight_scratch(*w_out_all.shape[1:]),
        compiler_params=pltpu.CompilerParams(dimension_semantics=("arbitrary", "arbitrary"),
                                             vmem_limit_bytes=VMEM_LIMIT_BYTES),
        name="rglru_mixer",
    )(x2, norm_w.reshape(1, d), w_in_all, conv_w, vec(conv_b), wa_all, vec(ba), wx_all, vec(bx),
      vec(lam), w_out_all, *cast.arrays)


def kernel(x, norm_mix, norm_ffn, norm_final, hgrn_w_in, hgrn_lb, hgrn_norm, hgrn_w_out, lru_w_in, lru_conv_w, lru_conv_b, lru_wa, lru_ba, lru_wx, lru_bx, lru_lambda, lru_w_out, ffn_w_in, ffn_w_out):
    batch, seq, d = x.shape
    depth = norm_mix.shape[0]
    n_mixers = 2

    def ffn(l, h, w_in, w_out):
        return _ffn(h, norm_ffn[l], w_in, w_out, norm_final, final_norm=(l == depth - 1))

    def layers_from(l, h):
        if l == depth:
            return h
        j = l // n_mixers
        if l % n_mixers == 1:
            mixed = _lru_mixer(h, batch, norm_mix[l], lru_w_in, lru_conv_w[j], lru_conv_b[j],
                               lru_wa, lru_ba[j], lru_wx, lru_bx[j], lru_lambda[j], lru_w_out, j,
                               ffn_w_in, ffn_w_out, l)
            return layers_from(l + 1, ffn(l, *mixed))
        hgrn = functools.partial(_hgrn_mixer, h, batch, norm_mix[l], hgrn_w_in, hgrn_lb,
                                 hgrn_norm[j], hgrn_w_out, j, ffn_w_in, ffn_w_out, l)
        mixed, worst, w_in, w_out = hgrn(single_ref=True)

        def any_decay():
            mixed, _, w_in, w_out = hgrn(single_ref=False)
            return layers_from(l + 1, ffn(l, mixed, w_in, w_out))

        return lax.cond(jnp.min(worst) >= -HGRN_SAFE_EXP,
                        lambda: layers_from(l + 1, ffn(l, mixed, w_in, w_out)), any_decay)

    return layers_from(0, x.reshape(batch * seq, d)).reshape(batch, seq, d)
```

```python
import functools
import math

import jax
import jax.numpy as jnp
from jax import lax
from jax.experimental import pallas as pl
from jax.experimental.pallas import tpu as pltpu

EPS = 1e-6
LRU_C = 8.0
LOG2E = math.log2(math.e)
MXU_DTYPE = jnp.bfloat16
SUBLANES = 8
LANES = 128
SCAN_RADIX = 4
SCAN_LEVELS = 4

HGRN_CHUNK = 128
HGRN_SAFE_EXP = 80.0
HGRN_LOG_FLOOR = -1e4
HGRN_BLOCK = 512
LRU_BLOCK = 512
FFN_BLOCK = 512
FFN_TILE = 256
VMEM_LIMIT_BYTES = 56 * 1024 * 1024
BF16_ROWS = 16
WEIGHT_CHUNK_BYTES = 2 * 1024 * 1024


def _dot(a, b):
    return jnp.dot(a, b, preferred_element_type=jnp.float32)


def _dot_nt(a, b):
    return lax.dot_general(a, b, (((1,), (1,)), ((), ())), preferred_element_type=jnp.float32)


def _rmsnorm(x, w):
    return x * lax.rsqrt(jnp.mean(x * x, axis=-1, keepdims=True) + EPS) * w


def _sigmoid(x):
    return 0.5 + 0.5 * jnp.tanh(0.5 * x)


def _silu(x):
    h = 0.5 * x
    return h + h * jnp.tanh(h)


def _split(x, terms):
    parts = []
    for i in range(terms):
        p = x.astype(MXU_DTYPE)
        parts.append(p)
        if i + 1 < terms:
            x = x - p.astype(jnp.float32)
    return parts


def _select_sum(sel, parts):
    out = _dot(sel, parts[0])
    for p in parts[1:]:
        out = out + _dot(sel, p)
    return out


def _const_spec(shape):
    return pl.BlockSpec(shape, lambda *_: (0,) * len(shape), pipeline_mode=pl.Buffered(1))


_HBM = pl.BlockSpec(memory_space=pl.ANY)


def _chunk_rows(rows, cols):
    r = max(BF16_ROWS, min(rows, WEIGHT_CHUNK_BYTES // (cols * 4)))
    while rows % r or r % BF16_ROWS:
        r -= 1
    return r


def _weight_scratch(rows, cols):
    return [pltpu.VMEM((rows, cols), MXU_DTYPE),
            pltpu.VMEM((2, _chunk_rows(rows, cols), cols), jnp.float32),
            pltpu.SemaphoreType.DMA((2,))]


def _load_weight(w_hbm, w_ref, stage_ref, sem_ref):
    rows = stage_ref.shape[1]
    n = w_hbm.shape[0] // rows

    def chunk_copy(k):
        return pltpu.make_async_copy(w_hbm.at[pl.ds(k * rows, rows)], stage_ref.at[k % 2], sem_ref.at[k % 2])

    chunk_copy(0).start()
    for k in range(n):
        if k + 1 < n:
            chunk_copy(k + 1).start()
        chunk_copy(k).wait()
        w_ref[k * rows:(k + 1) * rows, :] = stage_ref[k % 2].astype(MXU_DTYPE)


def _ffn_kernel(x_ref, nw_ref, wi_ref, wo_ref, fnw_ref, o_ref, act_ref, *, d_ff, final_norm):
    x = x_ref[...]
    hn = _rmsnorm(x, nw_ref[...]).astype(MXU_DTYPE)
    for j in range(d_ff // FFN_TILE):
        lo, hi = j * FFN_TILE, (j + 1) * FFN_TILE
        g = _dot(hn, wi_ref[:, lo:hi])
        u = _dot(hn, wi_ref[:, d_ff + lo:d_ff + hi])
        act_ref[:, lo:hi] = (_silu(g) * u).astype(MXU_DTYPE)
    acc = x + _dot(act_ref[...], wo_ref[...])
    if final_norm:
        acc = _rmsnorm(acc, fnw_ref[...])
    o_ref[...] = acc


def _ffn(x2, norm_w, w_in, w_out, final_norm_w, final_norm):
    t, d = x2.shape
    d_ff = w_out.shape[0]
    assert d_ff % FFN_TILE == 0 and t % FFN_BLOCK == 0
    row = pl.BlockSpec((FFN_BLOCK, d), lambda i: (i, 0))
    return pl.pallas_call(
        functools.partial(_ffn_kernel, d_ff=d_ff, final_norm=final_norm),
        grid=(t // FFN_BLOCK,),
        in_specs=[row, _const_spec((1, d)), _const_spec(w_in.shape), _const_spec(w_out.shape),
                  _const_spec((1, d))],
        out_specs=row,
        out_shape=jax.ShapeDtypeStruct((t, d), jnp.float32),
        scratch_shapes=[pltpu.VMEM((FFN_BLOCK, d_ff), MXU_DTYPE)],
        compiler_params=pltpu.CompilerParams(dimension_semantics=("arbitrary",),
                                             vmem_limit_bytes=VMEM_LIMIT_BYTES),
        name="swiglu_ffn",
    )(x2, norm_w.reshape(1, d), w_in, w_out, final_norm_w.reshape(1, d))


class _FfnWeightCast:
    def __init__(self, w_in_all, w_out_all, layer, n_steps):
        self.arrays = [w_in_all, w_out_all]
        self.layer = layer
        self.plan = []
        for w in self.arrays:
            rows, share = w.shape[1], 1
            while (rows * share) % (n_steps * BF16_ROWS):
                share *= 2
                assert share <= n_steps, "weight rows do not split into packed-row blocks"
            self.plan.append((rows * share // n_steps, share))

    def in_specs(self, step_of):
        return [pl.BlockSpec((None, r, w.shape[2]), lambda *g, _k=k: (self.layer, step_of(*g) // _k, 0))
                for w, (r, k) in zip(self.arrays, self.plan)]

    def out_specs(self, step_of):
        return [pl.BlockSpec((r, w.shape[2]), lambda *g, _k=k: (step_of(*g) // _k, 0))
                for w, (r, k) in zip(self.arrays, self.plan)]

    def out_shapes(self):
        return [jax.ShapeDtypeStruct(w.shape[1:], MXU_DTYPE) for w in self.arrays]


def _cast_slices(in_refs, out_refs):
    for i_ref, o_ref in zip(in_refs, out_refs):
        o_ref[...] = i_ref[...].astype(MXU_DTYPE)


def _hgrn_scores_single_ref(qd, kd, heads, dk, sc_ref):
    c = qd.shape[0]
    row = lax.broadcasted_iota(jnp.int32, (c, c), 0)
    col = lax.broadcasted_iota(jnp.int32, (c, c), 1)
    causal = col <= row
    for h in range(heads):
        sl = slice(h * dk, (h + 1) * dk)
        s = _dot_nt(qd[:, sl], kd[:, sl])
        sc_ref[h] = jnp.where(causal, s, 0.0).astype(MXU_DTYPE)


def _hgrn_scores_halving(q_c, k_c, lf_parts, heads, dk, sc_ref):
    c = q_c.shape[0]
    row = lax.broadcasted_iota(jnp.int32, (c, c), 0)
    col = lax.broadcasted_iota(jnp.int32, (c, c), 1)
    acc = []
    for h in range(heads):
        sl = slice(h * dk, (h + 1) * dk)
        s = _dot_nt(q_c[:, sl].astype(MXU_DTYPE), k_c[:, sl].astype(MXU_DTYPE))
        acc.append(jnp.where(row == col, s, 0.0))
    m = c // 2
    while m >= 1:
        blk = 2 * m
        ref_of_row = (row // blk) * blk + (m - 1)
        ref_of_col = (col // blk) * blk + (m - 1)
        row_upper = (row % blk) >= m
        sel_q = (row_upper & (col > ref_of_row) & (col <= row)).astype(MXU_DTYPE)
        sel_k = ((~row_upper) & (col > row) & (col <= ref_of_row)).astype(MXU_DTYPE)
        qd = (q_c * jnp.exp(_select_sum(sel_q, lf_parts))).astype(MXU_DTYPE)
        kd = (k_c * jnp.exp(_select_sum(sel_k, lf_parts))).astype(MXU_DTYPE)
        pair = (ref_of_row == ref_of_col) & row_upper & ((col % blk) < m)
        for h in range(heads):
            sl = slice(h * dk, (h + 1) * dk)
            acc[h] = acc[h] + jnp.where(pair, _dot_nt(qd[:, sl], kd[:, sl]), 0.0)
        m //= 2
    for h in range(heads):
        sc_ref[h] = acc[h].astype(MXU_DTYPE)


def _hgrn_kernel(x_ref, nw_ref, win_hbm, lbp_ref, gnw_ref, wout_hbm, fwi_f32, fwo_f32,
                 o_ref, worst_ref, fwi_ref, fwo_ref, st_ref, proj_ref, k_ref, sc_ref, y_ref,
                 win_ref, win_stage, win_sem, wout_ref, wout_stage, wout_sem,
                 *, layer, heads, dk, single_ref):
    kd_total = heads * dk
    tb = x_ref.shape[0]
    c = HGRN_CHUNK
    n_chunks = tb // c
    first = (pl.program_id(0) == 0) & (pl.program_id(1) == 0)

    @pl.when(pl.program_id(1) == 0)
    def _():
        st_ref[...] = jnp.zeros_like(st_ref)

    @pl.when(first)
    def _():
        worst_ref[...] = jnp.zeros_like(worst_ref)
        _load_weight(win_hbm.at[layer], win_ref, win_stage, win_sem)
        _load_weight(wout_hbm.at[layer], wout_ref, wout_stage, wout_sem)

    _cast_slices((fwi_f32, fwo_f32), (fwi_ref, fwo_ref))
    x = x_ref[...]
    hn = _rmsnorm(x, nw_ref[...]).astype(MXU_DTYPE)
    proj = _dot(hn, win_ref[...])

    lp = lbp_ref[...]
    e = jnp.exp(lp - jnp.max(lp, axis=0, keepdims=True))
    lb = jnp.sum(e[: layer + 1], axis=0, keepdims=True) / jnp.sum(e, axis=0, keepdims=True)
    one_m_lb = 1.0 - lb

    q = proj[:, :kd_total]
    fl = proj[:, kd_total:2 * kd_total]
    g = proj[:, 3 * kd_total:]
    swing = (0.5 * one_m_lb) * jnp.tanh(0.5 * fl)
    kk = 0.5 * one_m_lb - swing
    f = (lb + 0.5 * one_m_lb) + swing
    log_f = jnp.maximum(jnp.log(f), HGRN_LOG_FLOOR)

    proj_ref[:, :kd_total] = _silu(q)
    proj_ref[:, kd_total:2 * kd_total] = log_f
    proj_ref[:, 2 * kd_total:3 * kd_total] = proj[:, 2 * kd_total:3 * kd_total]
    proj_ref[:, 3 * kd_total:] = _silu(g)
    k_ref[...] = kk

    row = lax.broadcasted_iota(jnp.int32, (c, c), 0)
    col = lax.broadcasted_iota(jnp.int32, (c, c), 1)
    tril = (col <= row).astype(MXU_DTYPE)
    gnw = gnw_ref[...]

    for ci in range(n_chunks):
        rows = slice(ci * c, (ci + 1) * c)
        q_c = proj_ref[rows, :kd_total]
        lf_parts = _split(proj_ref[rows, kd_total:2 * kd_total], 2 if single_ref else 3)
        k_c = k_ref[rows, :]
        b = _select_sum(tril, lf_parts)
        b_last = b[c - 1:c, :]
        dec = jnp.exp(b_last)
        eb = jnp.exp(b)
        qd = (q_c * eb).astype(MXU_DTYPE)
        if single_ref:
            kd = (k_c * (1.0 / eb)).astype(MXU_DTYPE)
            _hgrn_scores_single_ref(qd, kd, heads, dk, sc_ref)
            ke = kd * dec.astype(MXU_DTYPE)
            folded = b_last
            width = kd_total
            while width > worst_ref.shape[1]:
                width //= 2
                folded = jnp.minimum(folded[:, :width], folded[:, width:])
            worst_ref[...] = jnp.minimum(worst_ref[...], folded)
        else:
            _hgrn_scores_halving(q_c, k_c, lf_parts, heads, dk, sc_ref)
            ke = (k_c * jnp.exp(b_last - b)).astype(MXU_DTYPE)
        for h in range(heads):
            sl = slice(h * dk, (h + 1) * dk)
            v_h = proj_ref[rows, 2 * kd_total + h * dk:2 * kd_total + (h + 1) * dk]
            st = st_ref[h]
            o = _dot(sc_ref[h], v_h.astype(MXU_DTYPE)) + _dot_nt(qd[:, sl], st.astype(MXU_DTYPE))
            st_ref[h] = st * dec[:, sl] + _dot(v_h.T.astype(MXU_DTYPE), ke[:, sl])
            o = o * lax.rsqrt(jnp.mean(o * o, axis=-1, keepdims=True) + EPS) * gnw
            sg = proj_ref[rows, 3 * kd_total + h * dk:3 * kd_total + (h + 1) * dk]
            y_ref[rows, sl] = (o * sg).astype(MXU_DTYPE)

    o_ref[...] = x + _dot(y_ref[...], wout_ref[...])


def _hgrn_mixer(x2, batch, norm_w, w_in_all, lb_param, gn_w, w_out_all, layer, ffn_w_in_all,
                ffn_w_out_all, ffn_layer, single_ref):
    t, d = x2.shape
    dv = gn_w.shape[-1]
    w_in_shape, w_out_shape = w_in_all.shape[1:], w_out_all.shape[1:]
    heads = w_out_shape[0] // dv
    kd_total = (w_in_shape[1] - 2 * heads * dv) // 2
    dk = kd_total // heads
    assert dk == dv and kd_total == heads * dv, "kernel assumes key dim == value dim"
    tb = HGRN_BLOCK
    spb = t // batch // tb
    assert spb * tb * batch == t and tb % HGRN_CHUNK == 0
    step_of = lambda b, s: b * spb + s
    row = pl.BlockSpec((tb, d), lambda b, s: (step_of(b, s), 0))
    cast = _FfnWeightCast(ffn_w_in_all, ffn_w_out_all, ffn_layer, batch * spb)
    return pl.pallas_call(
        functools.partial(_hgrn_kernel, layer=layer, heads=heads, dk=dk, single_ref=single_ref),
        grid=(batch, spb),
        in_specs=[row, _const_spec((1, d)), _HBM, _const_spec(lb_param.shape),
                  _const_spec((1, dv)), _HBM] + cast.in_specs(step_of),
        out_specs=[row, pl.BlockSpec((1, dk), lambda b, s: (0, 0))] + cast.out_specs(step_of),
        out_shape=[jax.ShapeDtypeStruct((t, d), jnp.float32),
                   jax.ShapeDtypeStruct((1, dk), jnp.float32)] + cast.out_shapes(),
        scratch_shapes=[
            pltpu.VMEM((heads, dv, dk), jnp.float32),
            pltpu.VMEM((tb, w_in_shape[1]), jnp.float32),
            pltpu.VMEM((tb, kd_total), jnp.float32),
            pltpu.VMEM((heads, HGRN_CHUNK, HGRN_CHUNK), MXU_DTYPE),
            pltpu.VMEM((tb, heads * dv), MXU_DTYPE),
        ] + _weight_scratch(*w_in_shape) + _weight_scratch(*w_out_shape),
        compiler_params=pltpu.CompilerParams(dimension_semantics=("arbitrary", "arbitrary"),
                                             vmem_limit_bytes=VMEM_LIMIT_BYTES),
        name="hgrn2_mixer" if single_ref else "hgrn2_mixer_any_decay",
    )(x2, norm_w.reshape(1, d), w_in_all, lb_param, gn_w.reshape(1, dv), w_out_all, *cast.arrays)


def _gelu_tanh(x):
    c = math.sqrt(2.0 / math.pi)
    return (0.5 * x) * (1.0 + jnp.tanh(x * (c + (c * 0.044715) * (x * x))))


def _scan_rows(a, b, n):
    rows = lax.broadcasted_iota(jnp.int32, a.shape, 0)
    d = 1
    while d < n:
        live = rows >= d
        a_sh = jnp.where(live, pltpu.roll(a, d, axis=0), 1.0)
        b_sh = jnp.where(live, pltpu.roll(b, d, axis=0), 0.0)
        b = a * b_sh + b
        a = a * a_sh
        d *= 2
    return a, b


def _scan_level(levels, lvl, j, h0, grouped=False):
    a_ref, b_ref, h_ref = levels[lvl]
    n = a_ref.shape[1]
    if lvl + 1 == len(levels) or n % SCAN_RADIX or n <= SUBLANES:
        a_inc, b_inc = _scan_rows(a_ref[j], b_ref[j], n)
        h_ref[j] = a_inc * h0 + b_inc
        return
    m = n // SCAN_RADIX
    step = lambda r: pl.ds(r, m, stride=SCAN_RADIX)
    read = (lambda r: pl.ds(r * m, m)) if grouped else step
    a_cum = [a_ref[j, read(0), :]]
    h_loc = [b_ref[j, read(0), :]]
    for r in range(1, SCAN_RADIX):
        a_r = a_ref[j, read(r), :]
        h_loc.append(a_r * h_loc[-1] + b_ref[j, read(r), :])
        a_cum.append(a_r * a_cum[-1])
    a_next, b_next, h_next = levels[lvl + 1]
    a_next[j] = a_cum[-1]
    b_next[j] = h_loc[-1]
    _scan_level(levels, lvl + 1, j, h0)
    group = lax.broadcasted_iota(jnp.int32, (m, LANES), 0)
    h_in = jnp.where(group >= 1, pltpu.roll(h_next[j], 1, axis=0), h0)
    for r in range(SCAN_RADIX):
        h_ref[j, step(r), :] = h_loc[r] + a_cum[r] * h_in


def _conv_grouped(u, ext_ref, cw_ref, cb_ref):
    tb, width = u.shape
    taps = cw_ref.shape[0]
    pad = ext_ref.shape[1] - tb
    m = tb // SCAN_RADIX
    cw = cw_ref[...]
    cb = cb_ref[...]
    tiles = []
    for j in range(width // LANES):
        cols = slice(j * LANES, (j + 1) * LANES)
        ext_ref[j, pad:, :] = u[:, cols]
        steps = []
        for r in range(SCAN_RADIX):
            acc = cb[:, cols]
            for t in range(taps):
                shifted = ext_ref[j, pl.ds(pad + r - t, m, stride=SCAN_RADIX), :]
                acc = acc + shifted * cw[taps - 1 - t:taps - t, cols]
            steps.append(acc)
        ext_ref[j, :pad, :] = u[tb - pad:, cols]
        tiles.append(jnp.concatenate(steps, axis=0))
    return jnp.concatenate(tiles, axis=-1)


def _lru_kernel(x_ref, nw_ref, win_hbm, cw_ref, cb_ref, wa_hbm, ba_ref, wx_hbm, bx_ref, lam_ref,
                wout_hbm, fwi_f32, fwo_f32, o_ref, fwi_ref, fwo_ref, h_ref, ext_ref, a_ref, b_ref, hs_ref,
                a1_ref, b1_ref, h1_ref, a2_ref, b2_ref, h2_ref, a3_ref, b3_ref, h3_ref,
                win_ref, win_stage, win_sem, wa_ref, wa_stage, wa_sem, wx_ref, wx_stage, wx_sem,
                wout_ref, wout_stage, wout_sem, *, layer, width, n_blocks):
    tb = x_ref.shape[0]
    bw = width // n_blocks

    @pl.when((pl.program_id(0) == 0) & (pl.program_id(1) == 0))
    def _():
        _load_weight(win_hbm.at[layer], win_ref, win_stage, win_sem)
        _load_weight(wa_hbm.at[layer], wa_ref, wa_stage, wa_sem)
        _load_weight(wx_hbm.at[layer], wx_ref, wx_stage, wx_sem)
        _load_weight(wout_hbm.at[layer], wout_ref, wout_stage, wout_sem)

    @pl.when(pl.program_id(1) == 0)
    def _():
        h_ref[...] = jnp.zeros_like(h_ref)
        ext_ref[:, :SUBLANES, :] = jnp.zeros((ext_ref.shape[0], SUBLANES, LANES), jnp.float32)

    _cast_slices((fwi_f32, fwo_f32), (fwi_ref, fwo_ref))
    x = x_ref[...]
    hn = _rmsnorm(x, nw_ref[...]).astype(MXU_DTYPE)
    proj = _dot(hn, win_ref[...])
    y = _gelu_tanh(proj[:, :width])
    u = proj[:, width:]

    uc = _conv_grouped(u, ext_ref, cw_ref, cb_ref)

    ra, rx = [], []
    for n in range(n_blocks):
        ub = uc[:, n * bw:(n + 1) * bw].astype(MXU_DTYPE)
        ra.append(_dot(ub, wa_ref[n * bw:(n + 1) * bw, :]))
        rx.append(_dot(ub, wx_ref[n * bw:(n + 1) * bw, :]))
    r = _sigmoid(jnp.concatenate(ra, axis=-1) + ba_ref[...])
    ig = _sigmoid(jnp.concatenate(rx, axis=-1) + bx_ref[...])

    nl = -lam_ref[...]
    softplus = jnp.maximum(nl, 0.0) + jnp.log1p(jnp.exp(-jnp.abs(nl)))
    rate = (-LRU_C) * softplus
    log_a = rate * r
    a = jnp.exp2((rate * LOG2E) * r)
    z = -jnp.tanh(log_a) * (a * a + 1.0)
    b_in = jnp.where(z > 0.0, z * lax.rsqrt(z), 0.0) * (ig * uc)

    levels = [(a_ref, b_ref, hs_ref), (a1_ref, b1_ref, h1_ref), (a2_ref, b2_ref, h2_ref),
              (a3_ref, b3_ref, h3_ref)]
    n_tiles = width // LANES
    for j in range(n_tiles):
        cols = slice(j * LANES, (j + 1) * LANES)
        a_ref[j] = a[:, cols]
        b_ref[j] = b_in[:, cols]
    for j in range(n_tiles):
        cols = slice(j * LANES, (j + 1) * LANES)
        _scan_level(levels, 0, j, h_ref[:, cols], grouped=True)
        h_ref[:, cols] = hs_ref[j, tb - 1:tb, :]
    hs = jnp.concatenate([hs_ref[j] for j in range(n_tiles)], axis=-1)
    hy = (hs * y).astype(MXU_DTYPE)

    o_ref[...] = x + _dot(hy, wout_ref[...])


def _lru_mixer(x2, batch, norm_w, w_in_all, conv_w, conv_b, wa_all, ba, wx_all, bx, lam, w_out_all,
               layer, ffn_w_in_all, ffn_w_out_all, ffn_layer):
    t, d = x2.shape
    width = w_out_all.shape[1]
    n_blocks, bw = wa_all.shape[1], wa_all.shape[2]
    taps = conv_w.shape[0]
    wa_all = wa_all.reshape(wa_all.shape[0], n_blocks * bw, bw)
    wx_all = wx_all.reshape(wx_all.shape[0], n_blocks * bw, bw)
    tb = LRU_BLOCK
    spb = t // batch // tb
    assert spb * tb * batch == t and taps - 1 <= SUBLANES and tb % (SUBLANES * SCAN_RADIX) == 0
    step_of = lambda b, s: b * spb + s
    row = pl.BlockSpec((tb, d), lambda b, s: (step_of(b, s), 0))
    vec = lambda v: v.reshape(1, width)
    cast = _FfnWeightCast(ffn_w_in_all, ffn_w_out_all, ffn_layer, batch * spb)
    return pl.pallas_call(
        functools.partial(_lru_kernel, layer=layer, width=width, n_blocks=n_blocks),
        grid=(batch, spb),
        in_specs=[row, _const_spec((1, d)), _HBM, _const_spec(conv_w.shape),
                  _const_spec((1, width)), _HBM, _const_spec((1, width)),
                  _HBM, _const_spec((1, width)), _const_spec((1, width)), _HBM]
        + cast.in_specs(step_of),
        out_specs=[row] + cast.out_specs(step_of),
        out_shape=[jax.ShapeDtypeStruct((t, d), jnp.float32)] + cast.out_shapes(),
        scratch_shapes=[
            pltpu.VMEM((1, width), jnp.float32),
            pltpu.VMEM((width // LANES, SUBLANES + tb, LANES), jnp.float32),
        ] + [pltpu.VMEM((width // LANES, tb // SCAN_RADIX ** lvl, LANES), jnp.float32)
             for lvl in range(SCAN_LEVELS) for _ in "abh"]
        + _weight_scratch(*w_in_all.shape[1:]) + _weight_scratch(*wa_all.shape[1:])
        + _weight_scratch(*wx_all.shape[1:]) + _weight_scratch(*w_out_all.shape[1:]),
        compiler_params=pltpu.CompilerParams(dimension_semantics=("arbitrary", "arbitrary"),
                                             vmem_limit_bytes=VMEM_LIMIT_BYTES),
        name="rglru_mixer",
    )(x2, norm_w.reshape(1, d), w_in_all, conv_w, vec(conv_b), wa_all, vec(ba), wx_all, vec(bx),
      vec(lam), w_out_all, *cast.arrays)


def kernel(x, norm_mix, norm_ffn, norm_final, hgrn_w_in, hgrn_lb, hgrn_norm, hgrn_w_out, lru_w_in, lru_conv_w, lru_conv_b, lru_wa, lru_ba, lru_wx, lru_bx, lru_lambda, lru_w_out, ffn_w_in, ffn_w_out):
    batch, seq, d = x.shape
    depth = norm_mix.shape[0]
    n_mixers = 2

    def ffn(l, h, w_in, w_out):
        return _ffn(h, norm_ffn[l], w_in, w_out, norm_final, final_norm=(l == depth - 1))

    def layers_from(l, h):
        if l == depth:
            return h
        j = l // n_mixers
        if l % n_mixers == 1:
            mixed = _lru_mixer(h, batch, norm_mix[l], lru_w_in, lru_conv_w[j], lru_conv_b[j],
                               lru_wa, lru_ba[j], lru_wx, lru_bx[j], lru_lambda[j], lru_w_out, j,
                               ffn_w_in, ffn_w_out, l)
            return layers_from(l + 1, ffn(l, *mixed))
        hgrn = functools.partial(_hgrn_mixer, h, batch, norm_mix[l], hgrn_w_in, hgrn_lb,
                                 hgrn_norm[j], hgrn_w_out, j, ffn_w_in, ffn_w_out, l)
        mixed, worst, w_in, w_out = hgrn(single_ref=True)

        def any_decay():
            mixed, _, w_in, w_out = hgrn(single_ref=False)
            return layers_from(l + 1, ffn(l, mixed, w_in, w_out))

        return lax.cond(jnp.min(worst) >= -HGRN_SAFE_EXP,
                        lambda: layers_from(l + 1, ffn(l, mixed, w_in, w_out)), any_decay)

    return layers_from(0, x.reshape(batch * seq, d)).reshape(batch, seq, d)
```

```python
import functools
import math

import jax
import jax.numpy as jnp
from jax import lax
from jax.experimental import pallas as pl
from jax.experimental.pallas import tpu as pltpu

EPS = 1e-6
LRU_C = 8.0
LOG2E = math.log2(math.e)
MXU_DTYPE = jnp.bfloat16
SUBLANES = 8
LANES = 128
SCAN_RADIX = 4
SCAN_LEVELS = 4

HGRN_CHUNK = 128
HGRN_SCAN_LEVELS = 3
HGRN_SAFE_EXP = 80.0
HGRN_LOG_FLOOR = -1e4
HGRN_BLOCK = 512
LRU_BLOCK = 512
FFN_BLOCK = 512
FFN_TILE = 256
VMEM_LIMIT_BYTES = 56 * 1024 * 1024
BF16_ROWS = 16
WEIGHT_CHUNK_BYTES = 2 * 1024 * 1024


def _dot(a, b):
    return jnp.dot(a, b, preferred_element_type=jnp.float32)


def _dot_nt(a, b):
    return lax.dot_general(a, b, (((1,), (1,)), ((), ())), preferred_element_type=jnp.float32)


def _rmsnorm(x, w):
    return x * lax.rsqrt(jnp.mean(x * x, axis=-1, keepdims=True) + EPS) * w


def _sigmoid(x):
    return 0.5 + 0.5 * jnp.tanh(0.5 * x)


def _split(x, terms):
    parts = []
    for i in range(terms):
        p = x.astype(MXU_DTYPE)
        parts.append(p)
        if i + 1 < terms:
            x = x - p.astype(jnp.float32)
    return parts


def _select_sum(sel, parts):
    out = _dot(sel, parts[0])
    for p in parts[1:]:
        out = out + _dot(sel, p)
    return out


def _const_spec(shape):
    return pl.BlockSpec(shape, lambda *_: (0,) * len(shape), pipeline_mode=pl.Buffered(1))


_HBM = pl.BlockSpec(memory_space=pl.ANY)


def _chunk_rows(rows, cols):
    r = max(BF16_ROWS, min(rows, WEIGHT_CHUNK_BYTES // (cols * 4)))
    while rows % r or r % BF16_ROWS:
        r -= 1
    return r


def _weight_scratch(rows, cols):
    return [pltpu.VMEM((rows, cols), MXU_DTYPE),
            pltpu.VMEM((2, _chunk_rows(rows, cols), cols), jnp.float32),
            pltpu.SemaphoreType.DMA((2,))]


def _load_weight(w_hbm, w_ref, stage_ref, sem_ref):
    rows = stage_ref.shape[1]
    n = w_hbm.shape[0] // rows

    def chunk_copy(k):
        return pltpu.make_async_copy(w_hbm.at[pl.ds(k * rows, rows)], stage_ref.at[k % 2], sem_ref.at[k % 2])

    chunk_copy(0).start()
    for k in range(n):
        if k + 1 < n:
            chunk_copy(k + 1).start()
        chunk_copy(k).wait()
        w_ref[k * rows:(k + 1) * rows, :] = stage_ref[k % 2].astype(MXU_DTYPE)


def _ffn_kernel(x_ref, nw_ref, wi_ref, wo_ref, fnw_ref, o_ref, act_ref, *, d_ff, final_norm):
    x = x_ref[...]
    hn = _rmsnorm(x, nw_ref[...]).astype(MXU_DTYPE)
    for j in range(d_ff // FFN_TILE):
        lo, hi = j * FFN_TILE, (j + 1) * FFN_TILE
        g = _dot(hn, wi_ref[:, lo:hi])
        u = _dot(hn, wi_ref[:, d_ff + lo:d_ff + hi])
        act_ref[:, lo:hi] = (g * _sigmoid(g) * u).astype(MXU_DTYPE)
    acc = x + _dot(act_ref[...], wo_ref[...])
    if final_norm:
        acc = _rmsnorm(acc, fnw_ref[...])
    o_ref[...] = acc


def _ffn(x2, norm_w, w_in, w_out, final_norm_w, final_norm):
    t, d = x2.shape
    d_ff = w_out.shape[0]
    assert d_ff % FFN_TILE == 0 and t % FFN_BLOCK == 0
    row = pl.BlockSpec((FFN_BLOCK, d), lambda i: (i, 0))
    return pl.pallas_call(
        functools.partial(_ffn_kernel, d_ff=d_ff, final_norm=final_norm),
        grid=(t // FFN_BLOCK,),
        in_specs=[row, _const_spec((1, d)), _const_spec(w_in.shape), _const_spec(w_out.shape),
                  _const_spec((1, d))],
        out_specs=row,
        out_shape=jax.ShapeDtypeStruct((t, d), jnp.float32),
        scratch_shapes=[pltpu.VMEM((FFN_BLOCK, d_ff), MXU_DTYPE)],
        compiler_params=pltpu.CompilerParams(dimension_semantics=("arbitrary",),
                                             vmem_limit_bytes=VMEM_LIMIT_BYTES),
        name="swiglu_ffn",
    )(x2, norm_w.reshape(1, d), w_in, w_out, final_norm_w.reshape(1, d))


class _FfnWeightCast:
    def __init__(self, w_in_all, w_out_all, layer, n_steps):
        self.arrays = [w_in_all, w_out_all]
        self.layer = layer
        self.plan = []
        for w in self.arrays:
            rows, share = w.shape[1], 1
            while (rows * share) % (n_steps * BF16_ROWS):
                share *= 2
                assert share <= n_steps, "weight rows do not split into packed-row blocks"
            self.plan.append((rows * share // n_steps, share))

    def in_specs(self, step_of):
        return [pl.BlockSpec((None, r, w.shape[2]), lambda *g, _k=k: (self.layer, step_of(*g) // _k, 0))
                for w, (r, k) in zip(self.arrays, self.plan)]

    def out_specs(self, step_of):
        return [pl.BlockSpec((r, w.shape[2]), lambda *g, _k=k: (step_of(*g) // _k, 0))
                for w, (r, k) in zip(self.arrays, self.plan)]

    def out_shapes(self):
        return [jax.ShapeDtypeStruct(w.shape[1:], MXU_DTYPE) for w in self.arrays]


def _cast_slices(in_refs, out_refs):
    for i_ref, o_ref in zip(in_refs, out_refs):
        o_ref[...] = i_ref[...].astype(MXU_DTYPE)


def _hgrn_scores_single_ref(qd, kd, heads, dk, sc_ref):
    c = qd.shape[0]
    row = lax.broadcasted_iota(jnp.int32, (c, c), 0)
    col = lax.broadcasted_iota(jnp.int32, (c, c), 1)
    causal = col <= row
    for h in range(heads):
        sl = slice(h * dk, (h + 1) * dk)
        s = _dot_nt(qd[:, sl], kd[:, sl])
        sc_ref[h] = jnp.where(causal, s, 0.0).astype(MXU_DTYPE)


def _hgrn_scores_halving(q_c, k_c, lf_parts, heads, dk, sc_ref):
    c = q_c.shape[0]
    row = lax.broadcasted_iota(jnp.int32, (c, c), 0)
    col = lax.broadcasted_iota(jnp.int32, (c, c), 1)
    acc = []
    for h in range(heads):
        sl = slice(h * dk, (h + 1) * dk)
        s = _dot_nt(q_c[:, sl].astype(MXU_DTYPE), k_c[:, sl].astype(MXU_DTYPE))
        acc.append(jnp.where(row == col, s, 0.0))
    m = c // 2
    while m >= 1:
        blk = 2 * m
        ref_of_row = (row // blk) * blk + (m - 1)
        ref_of_col = (col // blk) * blk + (m - 1)
        row_upper = (row % blk) >= m
        sel_q = (row_upper & (col > ref_of_row) & (col <= row)).astype(MXU_DTYPE)
        sel_k = ((~row_upper) & (col > row) & (col <= ref_of_row)).astype(MXU_DTYPE)
        qd = (q_c * jnp.exp(_select_sum(sel_q, lf_parts))).astype(MXU_DTYPE)
        kd = (k_c * jnp.exp(_select_sum(sel_k, lf_parts))).astype(MXU_DTYPE)
        pair = (ref_of_row == ref_of_col) & row_upper & ((col % blk) < m)
        for h in range(heads):
            sl = slice(h * dk, (h + 1) * dk)
            acc[h] = acc[h] + jnp.where(pair, _dot_nt(qd[:, sl], kd[:, sl]), 0.0)
        m //= 2
    for h in range(heads):
        sc_ref[h] = acc[h].astype(MXU_DTYPE)


def _hgrn_kernel(x_ref, nw_ref, win_hbm, lbp_ref, gnw_ref, wout_hbm, fwi_f32, fwo_f32,
                 o_ref, worst_ref, fwi_ref, fwo_ref, st_ref, proj_ref, k_ref, sc_ref, y_ref,
                 win_ref, win_stage, win_sem, wout_ref, wout_stage, wout_sem, *decay_refs,
                 layer, heads, dk, single_ref):
    kd_total = heads * dk
    tb = x_ref.shape[0]
    c = HGRN_CHUNK
    n_chunks = tb // c
    first = (pl.program_id(0) == 0) & (pl.program_id(1) == 0)

    @pl.when(pl.program_id(1) == 0)
    def _():
        st_ref[...] = jnp.zeros_like(st_ref)

    decay_levels = [tuple(decay_refs[3 * lvl:3 * lvl + 3]) for lvl in range(len(decay_refs) // 3)]

    @pl.when(first)
    def _():
        worst_ref[...] = jnp.zeros_like(worst_ref)
        decay_levels[0][1][...] = jnp.zeros_like(decay_levels[0][1])
        _load_weight(win_hbm.at[layer], win_ref, win_stage, win_sem)
        _load_weight(wout_hbm.at[layer], wout_ref, wout_stage, wout_sem)

    _cast_slices((fwi_f32, fwo_f32), (fwi_ref, fwo_ref))
    x = x_ref[...]
    hn = _rmsnorm(x, nw_ref[...]).astype(MXU_DTYPE)
    proj = _dot(hn, win_ref[...])

    lp = lbp_ref[...]
    e = jnp.exp(lp - jnp.max(lp, axis=0, keepdims=True))
    lb = jnp.sum(e[: layer + 1], axis=0, keepdims=True) / jnp.sum(e, axis=0, keepdims=True)
    one_m_lb = 1.0 - lb

    q = proj[:, :kd_total]
    fl = proj[:, kd_total:2 * kd_total]
    g = proj[:, 3 * kd_total:]
    half_t = 0.5 * jnp.tanh(0.5 * fl)
    kk = one_m_lb * (0.5 - half_t)
    f = lb + one_m_lb * (0.5 + half_t)

    proj_ref[:, :kd_total] = q * _sigmoid(q)
    proj_ref[:, kd_total:2 * kd_total] = f if single_ref else jnp.maximum(jnp.log(f), HGRN_LOG_FLOOR)
    proj_ref[:, 2 * kd_total:3 * kd_total] = proj[:, 2 * kd_total:3 * kd_total]
    proj_ref[:, 3 * kd_total:] = g * _sigmoid(g)
    k_ref[...] = kk

    row = lax.broadcasted_iota(jnp.int32, (c, c), 0)
    col = lax.broadcasted_iota(jnp.int32, (c, c), 1)
    tril = (col <= row).astype(MXU_DTYPE)
    gnw = gnw_ref[...]

    n_tiles = kd_total // LANES
    ones = jnp.ones((1, LANES), jnp.float32)
    for ci in range(n_chunks):
        rows = slice(ci * c, (ci + 1) * c)
        q_c = proj_ref[rows, :kd_total]
        k_c = k_ref[rows, :]
        if single_ref:
            f_c = proj_ref[rows, kd_total:2 * kd_total]
            fa_ref = decay_levels[0][0]
            for j in range(n_tiles):
                fa_ref[j] = f_c[:, j * LANES:(j + 1) * LANES]
            for j in range(n_tiles):
                _scan_level(decay_levels, 0, j, ones)
            eb = jnp.concatenate([decay_levels[0][2][j] for j in range(n_tiles)], axis=-1)
            dec = eb[c - 1:c, :]
            b_last = jnp.log(dec)
        else:
            lf_parts = _split(proj_ref[rows, kd_total:2 * kd_total], 3)
            b = _select_sum(tril, lf_parts)
            b_last = b[c - 1:c, :]
            dec = jnp.exp(b_last)
            eb = jnp.exp(b)
        qd = (q_c * eb).astype(MXU_DTYPE)
        if single_ref:
            kd = (k_c * (1.0 / eb)).astype(MXU_DTYPE)
            _hgrn_scores_single_ref(qd, kd, heads, dk, sc_ref)
            ke = kd * dec.astype(MXU_DTYPE)
            folded = b_last
            width = kd_total
            while width > worst_ref.shape[1]:
                width //= 2
                folded = jnp.minimum(folded[:, :width], folded[:, width:])
            worst_ref[...] = jnp.minimum(worst_ref[...], folded)
        else:
            _hgrn_scores_halving(q_c, k_c, lf_parts, heads, dk, sc_ref)
            ke = (k_c * jnp.exp(b_last - b)).astype(MXU_DTYPE)
        for h in range(heads):
            sl = slice(h * dk, (h + 1) * dk)
            v_h = proj_ref[rows, 2 * kd_total + h * dk:2 * kd_total + (h + 1) * dk]
            st = st_ref[h]
            o = _dot(sc_ref[h], v_h.astype(MXU_DTYPE)) + _dot_nt(qd[:, sl], st.astype(MXU_DTYPE))
            st_ref[h] = st * dec[:, sl] + _dot(v_h.T.astype(MXU_DTYPE), ke[:, sl])
            o = o * lax.rsqrt(jnp.mean(o * o, axis=-1, keepdims=True) + EPS) * gnw
            sg = proj_ref[rows, 3 * kd_total + h * dk:3 * kd_total + (h + 1) * dk]
            y_ref[rows, sl] = (o * sg).astype(MXU_DTYPE)

    o_ref[...] = x + _dot(y_ref[...], wout_ref[...])


def _hgrn_mixer(x2, batch, norm_w, w_in_all, lb_param, gn_w, w_out_all, layer, ffn_w_in_all,
                ffn_w_out_all, ffn_layer, single_ref):
    t, d = x2.shape
    dv = gn_w.shape[-1]
    w_in_shape, w_out_shape = w_in_all.shape[1:], w_out_all.shape[1:]
    heads = w_out_shape[0] // dv
    kd_total = (w_in_shape[1] - 2 * heads * dv) // 2
    dk = kd_total // heads
    assert dk == dv and kd_total == heads * dv, "kernel assumes key dim == value dim"
    tb = HGRN_BLOCK
    spb = t // batch // tb
    assert spb * tb * batch == t and tb % HGRN_CHUNK == 0
    step_of = lambda b, s: b * spb + s
    row = pl.BlockSpec((tb, d), lambda b, s: (step_of(b, s), 0))
    cast = _FfnWeightCast(ffn_w_in_all, ffn_w_out_all, ffn_layer, batch * spb)
    return pl.pallas_call(
        functools.partial(_hgrn_kernel, layer=layer, heads=heads, dk=dk, single_ref=single_ref),
        grid=(batch, spb),
        in_specs=[row, _const_spec((1, d)), _HBM, _const_spec(lb_param.shape),
                  _const_spec((1, dv)), _HBM] + cast.in_specs(step_of),
        out_specs=[row, pl.BlockSpec((1, dk), lambda b, s: (0, 0))] + cast.out_specs(step_of),
        out_shape=[jax.ShapeDtypeStruct((t, d), jnp.float32),
                   jax.ShapeDtypeStruct((1, dk), jnp.float32)] + cast.out_shapes(),
        scratch_shapes=[
            pltpu.VMEM((heads, dv, dk), jnp.float32),
            pltpu.VMEM((tb, w_in_shape[1]), jnp.float32),
            pltpu.VMEM((tb, kd_total), jnp.float32),
            pltpu.VMEM((heads, HGRN_CHUNK, HGRN_CHUNK), MXU_DTYPE),
            pltpu.VMEM((tb, heads * dv), MXU_DTYPE),
        ] + _weight_scratch(*w_in_shape) + _weight_scratch(*w_out_shape)
        + [pltpu.VMEM((kd_total // LANES, HGRN_CHUNK // SCAN_RADIX ** lvl, LANES), jnp.float32)
           for lvl in range(HGRN_SCAN_LEVELS) for _ in "abh"],
        compiler_params=pltpu.CompilerParams(dimension_semantics=("arbitrary", "arbitrary"),
                                             vmem_limit_bytes=VMEM_LIMIT_BYTES),
        name="hgrn2_mixer" if single_ref else "hgrn2_mixer_any_decay",
    )(x2, norm_w.reshape(1, d), w_in_all, lb_param, gn_w.reshape(1, dv), w_out_all, *cast.arrays)


def _gelu_tanh(x):
    c = math.sqrt(2.0 / math.pi)
    return (0.5 * x) * (1.0 + jnp.tanh(x * (c + (c * 0.044715) * (x * x))))


def _scan_rows(a, b, n):
    rows = lax.broadcasted_iota(jnp.int32, a.shape, 0)
    d = 1
    while d < n:
        live = rows >= d
        a_sh = jnp.where(live, pltpu.roll(a, d, axis=0), 1.0)
        b_sh = jnp.where(live, pltpu.roll(b, d, axis=0), 0.0)
        b = a * b_sh + b
        a = a * a_sh
        d *= 2
    return a, b


def _scan_level(levels, lvl, j, h0, grouped=False):
    a_ref, b_ref, h_ref = levels[lvl]
    n = a_ref.shape[1]
    if lvl + 1 == len(levels) or n % SCAN_RADIX or n <= SUBLANES:
        a_inc, b_inc = _scan_rows(a_ref[j], b_ref[j], n)
        h_ref[j] = a_inc * h0 + b_inc
        return
    m = n // SCAN_RADIX
    step = lambda r: pl.ds(r, m, stride=SCAN_RADIX)
    read = (lambda r: pl.ds(r * m, m)) if grouped else step
    a_cum = [a_ref[j, read(0), :]]
    h_loc = [b_ref[j, read(0), :]]
    for r in range(1, SCAN_RADIX):
        a_r = a_ref[j, read(r), :]
        h_loc.append(a_r * h_loc[-1] + b_ref[j, read(r), :])
        a_cum.append(a_r * a_cum[-1])
    a_next, b_next, h_next = levels[lvl + 1]
    a_next[j] = a_cum[-1]
    b_next[j] = h_loc[-1]
    _scan_level(levels, lvl + 1, j, h0)
    group = lax.broadcasted_iota(jnp.int32, (m, LANES), 0)
    h_in = jnp.where(group >= 1, pltpu.roll(h_next[j], 1, axis=0), h0)
    for r in range(SCAN_RADIX):
        h_ref[j, step(r), :] = h_loc[r] + a_cum[r] * h_in


def _conv_grouped(u, ext_ref, cw_ref, cb_ref):
    tb, width = u.shape
    taps = cw_ref.shape[0]
    pad = ext_ref.shape[1] - tb
    m = tb // SCAN_RADIX
    cw = cw_ref[...]
    cb = cb_ref[...]
    tiles = []
    for j in range(width // LANES):
        cols = slice(j * LANES, (j + 1) * LANES)
        ext_ref[j, pad:, :] = u[:, cols]
        steps = []
        for r in range(SCAN_RADIX):
            acc = cb[:, cols]
            for t in range(taps):
                shifted = ext_ref[j, pl.ds(pad + r - t, m, stride=SCAN_RADIX), :]
                acc = acc + shifted * cw[taps - 1 - t:taps - t, cols]
            steps.append(acc)
        ext_ref[j, :pad, :] = u[tb - pad:, cols]
        tiles.append(jnp.concatenate(steps, axis=0))
    return jnp.concatenate(tiles, axis=-1)


def _lru_kernel(x_ref, nw_ref, win_hbm, cw_ref, cb_ref, wa_hbm, ba_ref, wx_hbm, bx_ref, lam_ref,
                wout_hbm, fwi_f32, fwo_f32, o_ref, fwi_ref, fwo_ref, h_ref, ext_ref, a_ref, b_ref, hs_ref,
                a1_ref, b1_ref, h1_ref, a2_ref, b2_ref, h2_ref, a3_ref, b3_ref, h3_ref,
                win_ref, win_stage, win_sem, wa_ref, wa_stage, wa_sem, wx_ref, wx_stage, wx_sem,
                wout_ref, wout_stage, wout_sem, *, layer, width, n_blocks):
    tb = x_ref.shape[0]
    bw = width // n_blocks

    @pl.when((pl.program_id(0) == 0) & (pl.program_id(1) == 0))
    def _():
        _load_weight(win_hbm.at[layer], win_ref, win_stage, win_sem)
        _load_weight(wa_hbm.at[layer], wa_ref, wa_stage, wa_sem)
        _load_weight(wx_hbm.at[layer], wx_ref, wx_stage, wx_sem)
        _load_weight(wout_hbm.at[layer], wout_ref, wout_stage, wout_sem)

    @pl.when(pl.program_id(1) == 0)
    def _():
        h_ref[...] = jnp.zeros_like(h_ref)
        ext_ref[:, :SUBLANES, :] = jnp.zeros((ext_ref.shape[0], SUBLANES, LANES), jnp.float32)

    _cast_slices((fwi_f32, fwo_f32), (fwi_ref, fwo_ref))
    x = x_ref[...]
    hn = _rmsnorm(x, nw_ref[...]).astype(MXU_DTYPE)
    proj = _dot(hn, win_ref[...])
    y = _gelu_tanh(proj[:, :width])
    u = proj[:, width:]

    uc = _conv_grouped(u, ext_ref, cw_ref, cb_ref)

    ra, rx = [], []
    for n in range(n_blocks):
        ub = uc[:, n * bw:(n + 1) * bw].astype(MXU_DTYPE)
        ra.append(_dot(ub, wa_ref[n * bw:(n + 1) * bw, :]))
        rx.append(_dot(ub, wx_ref[n * bw:(n + 1) * bw, :]))
    r = _sigmoid(jnp.concatenate(ra, axis=-1) + ba_ref[...])
    ig = _sigmoid(jnp.concatenate(rx, axis=-1) + bx_ref[...])

    nl = -lam_ref[...]
    softplus = jnp.maximum(nl, 0.0) + jnp.log1p(jnp.exp(-jnp.abs(nl)))
    rate = (-LRU_C) * softplus
    log_a = rate * r
    a = jnp.exp2((rate * LOG2E) * r)
    z = -jnp.tanh(log_a) * (a * a + 1.0)
    b_in = jnp.where(z > 0.0, z * lax.rsqrt(z), 0.0) * (ig * uc)

    levels = [(a_ref, b_ref, hs_ref), (a1_ref, b1_ref, h1_ref), (a2_ref, b2_ref, h2_ref),
              (a3_ref, b3_ref, h3_ref)]
    n_tiles = width // LANES
    for j in range(n_tiles):
        cols = slice(j * LANES, (j + 1) * LANES)
        a_ref[j] = a[:, cols]
        b_ref[j] = b_in[:, cols]
    for j in range(n_tiles):
        cols = slice(j * LANES, (j + 1) * LANES)
        _scan_level(levels, 0, j, h_ref[:, cols], grouped=True)
        h_ref[:, cols] = hs_ref[j, tb - 1:tb, :]
    hs = jnp.concatenate([hs_ref[j] for j in range(n_tiles)], axis=-1)
    hy = (hs * y).astype(MXU_DTYPE)

    o_ref[...] = x + _dot(hy, wout_ref[...])


def _lru_mixer(x2, batch, norm_w, w_in_all, conv_w, conv_b, wa_all, ba, wx_all, bx, lam, w_out_all,
               layer, ffn_w_in_all, ffn_w_out_all, ffn_layer):
    t, d = x2.shape
    width = w_out_all.shape[1]
    n_blocks, bw = wa_all.shape[1], wa_all.shape[2]
    taps = conv_w.shape[0]
    wa_all = wa_all.reshape(wa_all.shape[0], n_blocks * bw, bw)
    wx_all = wx_all.reshape(wx_all.shape[0], n_blocks * bw, bw)
    tb = LRU_BLOCK
    spb = t // batch // tb
    assert spb * tb * batch == t and taps - 1 <= SUBLANES and tb % (SUBLANES * SCAN_RADIX) == 0
    step_of = lambda b, s: b * spb + s
    row = pl.BlockSpec((tb, d), lambda b, s: (step_of(b, s), 0))
    vec = lambda v: v.reshape(1, width)
    cast = _FfnWeightCast(ffn_w_in_all, ffn_w_out_all, ffn_layer, batch * spb)
    return pl.pallas_call(
        functools.partial(_lru_kernel, layer=layer, width=width, n_blocks=n_blocks),
        grid=(batch, spb),
        in_specs=[row, _const_spec((1, d)), _HBM, _const_spec(conv_w.shape),
                  _const_spec((1, width)), _HBM, _const_spec((1, width)),
                  _HBM, _const_spec((1, width)), _const_spec((1, width)), _HBM]
        + cast.in_specs(step_of),
        out_specs=[row] + cast.out_specs(step_of),
        out_shape=[jax.ShapeDtypeStruct((t, d), jnp.float32)] + cast.out_shapes(),
        scratch_shapes=[
            pltpu.VMEM((1, width), jnp.float32),
            pltpu.VMEM((width // LANES, SUBLANES + tb, LANES), jnp.float32),
        ] + [pltpu.VMEM((width // LANES, tb // SCAN_RADIX ** lvl, LANES), jnp.float32)
             for lvl in range(SCAN_LEVELS) for _ in "abh"]
        + _weight_scratch(*w_in_all.shape[1:]) + _weight_scratch(*wa_all.shape[1:])
        + _weight_scratch(*wx_all.shape[1:]) + _weight_scratch(*w_out_all.shape[1:]),
        compiler_params=pltpu.CompilerParams(dimension_semantics=("arbitrary", "arbitrary"),
                                             vmem_limit_bytes=VMEM_LIMIT_BYTES),
        name="rglru_mixer",
    )(x2, norm_w.reshape(1, d), w_in_all, conv_w, vec(conv_b), wa_all, vec(ba), wx_all, vec(bx),
      vec(lam), w_out_all, *cast.arrays)


def kernel(x, norm_mix, norm_ffn, norm_final, hgrn_w_in, hgrn_lb, hgrn_norm, hgrn_w_out, lru_w_in, lru_conv_w, lru_conv_b, lru_wa, lru_ba, lru_wx, lru_bx, lru_lambda, lru_w_out, ffn_w_in, ffn_w_out):
    batch, seq, d = x.shape
    depth = norm_mix.shape[0]
    n_mixers = 2

    def ffn(l, h, w_in, w_out):
        return _ffn(h, norm_ffn[l], w_in, w_out, norm_final, final_norm=(l == depth - 1))

    def layers_from(l, h):
        if l == depth:
            return h
        j = l // n_mixers
        if l % n_mixers == 1:
            mixed = _lru_mixer(h, batch, norm_mix[l], lru_w_in, lru_conv_w[j], lru_conv_b[j],
                               lru_wa, lru_ba[j], lru_wx, lru_bx[j], lru_lambda[j], lru_w_out, j,
                               ffn_w_in, ffn_w_out, l)
            return layers_from(l + 1, ffn(l, *mixed))
        hgrn = functools.partial(_hgrn_mixer, h, batch, norm_mix[l], hgrn_w_in, hgrn_lb,
                                 hgrn_norm[j], hgrn_w_out, j, ffn_w_in, ffn_w_out, l)
        mixed, worst, w_in, w_out = hgrn(single_ref=True)

        def any_decay():
            mixed, _, w_in, w_out = hgrn(single_ref=False)
            return layers_from(l + 1, ffn(l, mixed, w_in, w_out))

        return lax.cond(jnp.min(worst) >= -HGRN_SAFE_EXP,
                        lambda: layers_from(l + 1, ffn(l, mixed, w_in, w_out)), any_decay)

    return layers_from(0, x.reshape(batch * seq, d)).reshape(batch, seq, d)
```

```python
import functools
import math

import jax
import jax.numpy as jnp
from jax import lax
from jax.experimental import pallas as pl
from jax.experimental.pallas import tpu as pltpu

EPS = 1e-6
LRU_C = 8.0
LOG2E = math.log2(math.e)
MXU_DTYPE = jnp.bfloat16
SUBLANES = 8
LANES = 128
SCAN_RADIX = 4
SCAN_LEVELS = 4

HGRN_CHUNK = 128
HGRN_SCAN_LEVELS = 3
HGRN_SAFE_EXP = 80.0
HGRN_LOG_FLOOR = -1e4
HGRN_BLOCK = 512
LRU_BLOCK = 512
FFN_BLOCK = 512
FFN_TILE = 256
VMEM_LIMIT_BYTES = 56 * 1024 * 1024
BF16_ROWS = 16
WEIGHT_CHUNK_BYTES = 2 * 1024 * 1024


def _dot(a, b):
    return jnp.dot(a, b, preferred_element_type=jnp.float32)


def _dot_nt(a, b):
    return lax.dot_general(a, b, (((1,), (1,)), ((), ())), preferred_element_type=jnp.float32)


def _rmsnorm(x, w):
    return x * lax.rsqrt(jnp.mean(x * x, axis=-1, keepdims=True) + EPS) * w


def _sigmoid(x):
    return 0.5 + 0.5 * jnp.tanh(0.5 * x)


def _split(x, terms):
    parts = []
    for i in range(terms):
        p = x.astype(MXU_DTYPE)
        parts.append(p)
        if i + 1 < terms:
            x = x - p.astype(jnp.float32)
    return parts


def _select_sum(sel, parts):
    out = _dot(sel, parts[0])
    for p in parts[1:]:
        out = out + _dot(sel, p)
    return out


def _const_spec(shape):
    return pl.BlockSpec(shape, lambda *_: (0,) * len(shape), pipeline_mode=pl.Buffered(1))


_HBM = pl.BlockSpec(memory_space=pl.ANY)


def _chunk_rows(rows, cols):
    r = max(BF16_ROWS, min(rows, WEIGHT_CHUNK_BYTES // (cols * 4)))
    while rows % r or r % BF16_ROWS:
        r -= 1
    return r


def _weight_scratch(rows, cols):
    return [pltpu.VMEM((rows, cols), MXU_DTYPE),
            pltpu.VMEM((2, _chunk_rows(rows, cols), cols), jnp.float32),
            pltpu.SemaphoreType.DMA((2,))]


def _load_weight(w_hbm, w_ref, stage_ref, sem_ref):
    rows = stage_ref.shape[1]
    n = w_hbm.shape[0] // rows

    def chunk_copy(k):
        return pltpu.make_async_copy(w_hbm.at[pl.ds(k * rows, rows)], stage_ref.at[k % 2], sem_ref.at[k % 2])

    chunk_copy(0).start()
    for k in range(n):
        if k + 1 < n:
            chunk_copy(k + 1).start()
        chunk_copy(k).wait()
        w_ref[k * rows:(k + 1) * rows, :] = stage_ref[k % 2].astype(MXU_DTYPE)


def _ffn_kernel(x_ref, nw_ref, wi_ref, wo_ref, fnw_ref, o_ref, act_ref, *, d_ff, final_norm):
    x = x_ref[...]
    hn = _rmsnorm(x, nw_ref[...]).astype(MXU_DTYPE)
    for j in range(d_ff // FFN_TILE):
        lo, hi = j * FFN_TILE, (j + 1) * FFN_TILE
        g = _dot(hn, wi_ref[:, lo:hi])
        u = _dot(hn, wi_ref[:, d_ff + lo:d_ff + hi])
        act_ref[:, lo:hi] = (g * _sigmoid(g) * u).astype(MXU_DTYPE)
    acc = x + _dot(act_ref[...], wo_ref[...])
    if final_norm:
        acc = _rmsnorm(acc, fnw_ref[...])
    o_ref[...] = acc


def _ffn(x2, norm_w, w_in, w_out, final_norm_w, final_norm):
    t, d = x2.shape
    d_ff = w_out.shape[0]
    assert d_ff % FFN_TILE == 0 and t % FFN_BLOCK == 0
    row = pl.BlockSpec((FFN_BLOCK, d), lambda i: (i, 0))
    return pl.pallas_call(
        functools.partial(_ffn_kernel, d_ff=d_ff, final_norm=final_norm),
        grid=(t // FFN_BLOCK,),
        in_specs=[row, _const_spec((1, d)), _const_spec(w_in.shape), _const_spec(w_out.shape),
                  _const_spec((1, d))],
        out_specs=row,
        out_shape=jax.ShapeDtypeStruct((t, d), jnp.float32),
        scratch_shapes=[pltpu.VMEM((FFN_BLOCK, d_ff), MXU_DTYPE)],
        compiler_params=pltpu.CompilerParams(dimension_semantics=("arbitrary",),
                                             vmem_limit_bytes=VMEM_LIMIT_BYTES),
        name="swiglu_ffn",
    )(x2, norm_w.reshape(1, d), w_in, w_out, final_norm_w.reshape(1, d))


class _FfnWeightCast:
    def __init__(self, w_in_all, w_out_all, layer, n_steps):
        self.arrays = [w_in_all, w_out_all]
        self.layer = layer
        self.plan = []
        for w in self.arrays:
            rows, share = w.shape[1], 1
            while (rows * share) % (n_steps * BF16_ROWS):
                share *= 2
                assert share <= n_steps, "weight rows do not split into packed-row blocks"
            self.plan.append((rows * share // n_steps, share))

    def in_specs(self, step_of):
        return [pl.BlockSpec((None, r, w.shape[2]), lambda *g, _k=k: (self.layer, step_of(*g) // _k, 0))
                for w, (r, k) in zip(self.arrays, self.plan)]

    def out_specs(self, step_of):
        return [pl.BlockSpec((r, w.shape[2]), lambda *g, _k=k: (step_of(*g) // _k, 0))
                for w, (r, k) in zip(self.arrays, self.plan)]

    def out_shapes(self):
        return [jax.ShapeDtypeStruct(w.shape[1:], MXU_DTYPE) for w in self.arrays]


def _cast_slices(in_refs, out_refs):
    for i_ref, o_ref in zip(in_refs, out_refs):
        o_ref[...] = i_ref[...].astype(MXU_DTYPE)


def _hgrn_scores_single_ref(qd, kd, heads, dk, sc_ref):
    c = qd.shape[0]
    row = lax.broadcasted_iota(jnp.int32, (c, c), 0)
    col = lax.broadcasted_iota(jnp.int32, (c, c), 1)
    causal = col <= row
    for h in range(heads):
        sl = slice(h * dk, (h + 1) * dk)
        s = _dot_nt(qd[:, sl], kd[:, sl])
        sc_ref[h] = jnp.where(causal, s, 0.0).astype(MXU_DTYPE)


def _hgrn_scores_halving(q_c, k_c, lf_parts, heads, dk, sc_ref):
    c = q_c.shape[0]
    row = lax.broadcasted_iota(jnp.int32, (c, c), 0)
    col = lax.broadcasted_iota(jnp.int32, (c, c), 1)
    acc = []
    for h in range(heads):
        sl = slice(h * dk, (h + 1) * dk)
        s = _dot_nt(q_c[:, sl].astype(MXU_DTYPE), k_c[:, sl].astype(MXU_DTYPE))
        acc.append(jnp.where(row == col, s, 0.0))
    m = c // 2
    while m >= 1:
        blk = 2 * m
        ref_of_row = (row // blk) * blk + (m - 1)
        ref_of_col = (col // blk) * blk + (m - 1)
        row_upper = (row % blk) >= m
        sel_q = (row_upper & (col > ref_of_row) & (col <= row)).astype(MXU_DTYPE)
        sel_k = ((~row_upper) & (col > row) & (col <= ref_of_row)).astype(MXU_DTYPE)
        qd = (q_c * jnp.exp(_select_sum(sel_q, lf_parts))).astype(MXU_DTYPE)
        kd = (k_c * jnp.exp(_select_sum(sel_k, lf_parts))).astype(MXU_DTYPE)
        pair = (ref_of_row == ref_of_col) & row_upper & ((col % blk) < m)
        for h in range(heads):
            sl = slice(h * dk, (h + 1) * dk)
            acc[h] = acc[h] + jnp.where(pair, _dot_nt(qd[:, sl], kd[:, sl]), 0.0)
        m //= 2
    for h in range(heads):
        sc_ref[h] = acc[h].astype(MXU_DTYPE)


def _hgrn_kernel(x_ref, nw_ref, win_hbm, lbp_ref, gnw_ref, wout_hbm, fwi_f32, fwo_f32,
                 o_ref, worst_ref, fwi_ref, fwo_ref, st_ref, proj_ref, k_ref, sc_ref, y_ref,
                 win_ref, win_stage, win_sem, wout_ref, wout_stage, wout_sem, *decay_refs,
                 layer, heads, dk, single_ref):
    kd_total = heads * dk
    tb = x_ref.shape[0]
    c = HGRN_CHUNK
    n_chunks = tb // c
    first = (pl.program_id(0) == 0) & (pl.program_id(1) == 0)

    @pl.when(pl.program_id(1) == 0)
    def _():
        st_ref[...] = jnp.zeros_like(st_ref)

    decay_levels = [tuple(decay_refs[2 * lvl:2 * lvl + 2]) for lvl in range(len(decay_refs) // 2)]

    @pl.when(first)
    def _():
        worst_ref[...] = jnp.zeros_like(worst_ref)
        _load_weight(win_hbm.at[layer], win_ref, win_stage, win_sem)
        _load_weight(wout_hbm.at[layer], wout_ref, wout_stage, wout_sem)

    _cast_slices((fwi_f32, fwo_f32), (fwi_ref, fwo_ref))
    x = x_ref[...]
    hn = _rmsnorm(x, nw_ref[...]).astype(MXU_DTYPE)
    proj = _dot(hn, win_ref[...])

    lp = lbp_ref[...]
    e = jnp.exp(lp - jnp.max(lp, axis=0, keepdims=True))
    lb = jnp.sum(e[: layer + 1], axis=0, keepdims=True) / jnp.sum(e, axis=0, keepdims=True)
    one_m_lb = 1.0 - lb

    q = proj[:, :kd_total]
    fl = proj[:, kd_total:2 * kd_total]
    g = proj[:, 3 * kd_total:]
    half_t = 0.5 * jnp.tanh(0.5 * fl)
    kk = one_m_lb * (0.5 - half_t)
    f = lb + one_m_lb * (0.5 + half_t)

    proj_ref[:, :kd_total] = q * _sigmoid(q)
    proj_ref[:, kd_total:2 * kd_total] = f if single_ref else jnp.maximum(jnp.log(f), HGRN_LOG_FLOOR)
    proj_ref[:, 2 * kd_total:3 * kd_total] = proj[:, 2 * kd_total:3 * kd_total]
    proj_ref[:, 3 * kd_total:] = g * _sigmoid(g)
    k_ref[...] = kk

    row = lax.broadcasted_iota(jnp.int32, (c, c), 0)
    col = lax.broadcasted_iota(jnp.int32, (c, c), 1)
    tril = (col <= row).astype(MXU_DTYPE)
    gnw = gnw_ref[...]

    n_tiles = kd_total // LANES
    for ci in range(n_chunks):
        rows = slice(ci * c, (ci + 1) * c)
        q_c = proj_ref[rows, :kd_total]
        k_c = k_ref[rows, :]
        if single_ref:
            f_c = proj_ref[rows, kd_total:2 * kd_total]
            fa_ref, eb_ref = decay_levels[0]
            for j in range(n_tiles):
                fa_ref[j] = f_c[:, j * LANES:(j + 1) * LANES]
            for j in range(n_tiles):
                _cumprod_level(decay_levels, 0, j)
            eb = jnp.concatenate([eb_ref[j] for j in range(n_tiles)], axis=-1)
            dec = eb[c - 1:c, :]
            b_last = jnp.log(dec)
        else:
            lf_parts = _split(proj_ref[rows, kd_total:2 * kd_total], 3)
            b = _select_sum(tril, lf_parts)
            b_last = b[c - 1:c, :]
            dec = jnp.exp(b_last)
            eb = jnp.exp(b)
        qd = (q_c * eb).astype(MXU_DTYPE)
        if single_ref:
            kd = (k_c * (1.0 / eb)).astype(MXU_DTYPE)
            _hgrn_scores_single_ref(qd, kd, heads, dk, sc_ref)
            ke = kd * dec.astype(MXU_DTYPE)
            folded = b_last
            width = kd_total
            while width > worst_ref.shape[1]:
                width //= 2
                folded = jnp.minimum(folded[:, :width], folded[:, width:])
            worst_ref[...] = jnp.minimum(worst_ref[...], folded)
        else:
            _hgrn_scores_halving(q_c, k_c, lf_parts, heads, dk, sc_ref)
            ke = (k_c * jnp.exp(b_last - b)).astype(MXU_DTYPE)
        for h in range(heads):
            sl = slice(h * dk, (h + 1) * dk)
            v_h = proj_ref[rows, 2 * kd_total + h * dk:2 * kd_total + (h + 1) * dk]
            st = st_ref[h]
            o = _dot(sc_ref[h], v_h.astype(MXU_DTYPE)) + _dot_nt(qd[:, sl], st.astype(MXU_DTYPE))
            st_ref[h] = st * dec[:, sl] + _dot(v_h.T.astype(MXU_DTYPE), ke[:, sl])
            o = o * lax.rsqrt(jnp.mean(o * o, axis=-1, keepdims=True) + EPS) * gnw
            sg = proj_ref[rows, 3 * kd_total + h * dk:3 * kd_total + (h + 1) * dk]
            y_ref[rows, sl] = (o * sg).astype(MXU_DTYPE)

    o_ref[...] = x + _dot(y_ref[...], wout_ref[...])


def _hgrn_mixer(x2, batch, norm_w, w_in_all, lb_param, gn_w, w_out_all, layer, ffn_w_in_all,
                ffn_w_out_all, ffn_layer, single_ref):
    t, d = x2.shape
    dv = gn_w.shape[-1]
    w_in_shape, w_out_shape = w_in_all.shape[1:], w_out_all.shape[1:]
    heads = w_out_shape[0] // dv
    kd_total = (w_in_shape[1] - 2 * heads * dv) // 2
    dk = kd_total // heads
    assert dk == dv and kd_total == heads * dv, "kernel assumes key dim == value dim"
    tb = HGRN_BLOCK
    spb = t // batch // tb
    assert spb * tb * batch == t and tb % HGRN_CHUNK == 0
    step_of = lambda b, s: b * spb + s
    row = pl.BlockSpec((tb, d), lambda b, s: (step_of(b, s), 0))
    cast = _FfnWeightCast(ffn_w_in_all, ffn_w_out_all, ffn_layer, batch * spb)
    return pl.pallas_call(
        functools.partial(_hgrn_kernel, layer=layer, heads=heads, dk=dk, single_ref=single_ref),
        grid=(batch, spb),
        in_specs=[row, _const_spec((1, d)), _HBM, _const_spec(lb_param.shape),
                  _const_spec((1, dv)), _HBM] + cast.in_specs(step_of),
        out_specs=[row, pl.BlockSpec((1, dk), lambda b, s: (0, 0))] + cast.out_specs(step_of),
        out_shape=[jax.ShapeDtypeStruct((t, d), jnp.float32),
                   jax.ShapeDtypeStruct((1, dk), jnp.float32)] + cast.out_shapes(),
        scratch_shapes=[
            pltpu.VMEM((heads, dv, dk), jnp.float32),
            pltpu.VMEM((tb, w_in_shape[1]), jnp.float32),
            pltpu.VMEM((tb, kd_total), jnp.float32),
            pltpu.VMEM((heads, HGRN_CHUNK, HGRN_CHUNK), MXU_DTYPE),
            pltpu.VMEM((tb, heads * dv), MXU_DTYPE),
        ] + _weight_scratch(*w_in_shape) + _weight_scratch(*w_out_shape)
        + [pltpu.VMEM((kd_total // LANES, HGRN_CHUNK // SCAN_RADIX ** lvl, LANES), jnp.float32)
           for lvl in range(HGRN_SCAN_LEVELS) for _ in "ap"],
        compiler_params=pltpu.CompilerParams(dimension_semantics=("arbitrary", "arbitrary"),
                                             vmem_limit_bytes=VMEM_LIMIT_BYTES),
        name="hgrn2_mixer" if single_ref else "hgrn2_mixer_any_decay",
    )(x2, norm_w.reshape(1, d), w_in_all, lb_param, gn_w.reshape(1, dv), w_out_all, *cast.arrays)


def _gelu_tanh(x):
    c = math.sqrt(2.0 / math.pi)
    return (0.5 * x) * (1.0 + jnp.tanh(x * (c + (c * 0.044715) * (x * x))))


def _scan_rows(a, b, n):
    rows = lax.broadcasted_iota(jnp.int32, a.shape, 0)
    d = 1
    while d < n:
        live = rows >= d
        a_sh = jnp.where(live, pltpu.roll(a, d, axis=0), 1.0)
        b_sh = jnp.where(live, pltpu.roll(b, d, axis=0), 0.0)
        b = a * b_sh + b
        a = a * a_sh
        d *= 2
    return a, b


def _scan_level(levels, lvl, j, h0, grouped=False):
    a_ref, b_ref, h_ref = levels[lvl]
    n = a_ref.shape[1]
    if lvl + 1 == len(levels) or n % SCAN_RADIX or n <= SUBLANES:
        a_inc, b_inc = _scan_rows(a_ref[j], b_ref[j], n)
        h_ref[j] = a_inc * h0 + b_inc
        return
    m = n // SCAN_RADIX
    step = lambda r: pl.ds(r, m, stride=SCAN_RADIX)
    read = (lambda r: pl.ds(r * m, m)) if grouped else step
    a_cum = [a_ref[j, read(0), :]]
    h_loc = [b_ref[j, read(0), :]]
    for r in range(1, SCAN_RADIX):
        a_r = a_ref[j, read(r), :]
        h_loc.append(a_r * h_loc[-1] + b_ref[j, read(r), :])
        a_cum.append(a_r * a_cum[-1])
    a_next, b_next, h_next = levels[lvl + 1]
    a_next[j] = a_cum[-1]
    b_next[j] = h_loc[-1]
    _scan_level(levels, lvl + 1, j, h0)
    group = lax.broadcasted_iota(jnp.int32, (m, LANES), 0)
    h_in = jnp.where(group >= 1, pltpu.roll(h_next[j], 1, axis=0), h0)
    for r in range(SCAN_RADIX):
        h_ref[j, step(r), :] = h_loc[r] + a_cum[r] * h_in


def _cumprod_level(levels, lvl, j):
    a_ref, p_ref = levels[lvl]
    n = a_ref.shape[1]
    if lvl + 1 == len(levels) or n % SCAN_RADIX or n <= SUBLANES:
        a = a_ref[j]
        rows = lax.broadcasted_iota(jnp.int32, a.shape, 0)
        d = 1
        while d < n:
            a = a * jnp.where(rows >= d, pltpu.roll(a, d, axis=0), 1.0)
            d *= 2
        p_ref[j] = a
        return
    m = n // SCAN_RADIX
    step = lambda r: pl.ds(r, m, stride=SCAN_RADIX)
    a_cum = [a_ref[j, step(0), :]]
    for r in range(1, SCAN_RADIX):
        a_cum.append(a_ref[j, step(r), :] * a_cum[-1])
    a_next, p_next = levels[lvl + 1]
    a_next[j] = a_cum[-1]
    _cumprod_level(levels, lvl + 1, j)
    group = lax.broadcasted_iota(jnp.int32, (m, LANES), 0)
    p_in = jnp.where(group >= 1, pltpu.roll(p_next[j], 1, axis=0), 1.0)
    for r in range(SCAN_RADIX):
        p_ref[j, step(r), :] = a_cum[r] * p_in


def _conv_grouped(u, ext_ref, cw_ref, cb_ref):
    tb, width = u.shape
    taps = cw_ref.shape[0]
    pad = ext_ref.shape[1] - tb
    m = tb // SCAN_RADIX
    cw = cw_ref[...]
    cb = cb_ref[...]
    tiles = []
    for j in range(width // LANES):
        cols = slice(j * LANES, (j + 1) * LANES)
        ext_ref[j, pad:, :] = u[:, cols]
        steps = []
        for r in range(SCAN_RADIX):
            acc = cb[:, cols]
            for t in range(taps):
                shifted = ext_ref[j, pl.ds(pad + r - t, m, stride=SCAN_RADIX), :]
                acc = acc + shifted * cw[taps - 1 - t:taps - t, cols]
            steps.append(acc)
        ext_ref[j, :pad, :] = u[tb - pad:, cols]
        tiles.append(jnp.concatenate(steps, axis=0))
    return jnp.concatenate(tiles, axis=-1)


def _lru_kernel(x_ref, nw_ref, win_hbm, cw_ref, cb_ref, wa_hbm, ba_ref, wx_hbm, bx_ref, lam_ref,
                wout_hbm, fwi_f32, fwo_f32, o_ref, fwi_ref, fwo_ref, h_ref, ext_ref, a_ref, b_ref, hs_ref,
                a1_ref, b1_ref, h1_ref, a2_ref, b2_ref, h2_ref, a3_ref, b3_ref, h3_ref,
                win_ref, win_stage, win_sem, wa_ref, wa_stage, wa_sem, wx_ref, wx_stage, wx_sem,
                wout_ref, wout_stage, wout_sem, *, layer, width, n_blocks):
    tb = x_ref.shape[0]
    bw = width // n_blocks

    @pl.when((pl.program_id(0) == 0) & (pl.program_id(1) == 0))
    def _():
        _load_weight(win_hbm.at[layer], win_ref, win_stage, win_sem)
        _load_weight(wa_hbm.at[layer], wa_ref, wa_stage, wa_sem)
        _load_weight(wx_hbm.at[layer], wx_ref, wx_stage, wx_sem)
        _load_weight(wout_hbm.at[layer], wout_ref, wout_stage, wout_sem)

    @pl.when(pl.program_id(1) == 0)
    def _():
        h_ref[...] = jnp.zeros_like(h_ref)
        ext_ref[:, :SUBLANES, :] = jnp.zeros((ext_ref.shape[0], SUBLANES, LANES), jnp.float32)

    _cast_slices((fwi_f32, fwo_f32), (fwi_ref, fwo_ref))
    x = x_ref[...]
    hn = _rmsnorm(x, nw_ref[...]).astype(MXU_DTYPE)
    proj = _dot(hn, win_ref[...])
    y = _gelu_tanh(proj[:, :width])
    u = proj[:, width:]

    uc = _conv_grouped(u, ext_ref, cw_ref, cb_ref)

    ra, rx = [], []
    for n in range(n_blocks):
        ub = uc[:, n * bw:(n + 1) * bw].astype(MXU_DTYPE)
        ra.append(_dot(ub, wa_ref[n * bw:(n + 1) * bw, :]))
        rx.append(_dot(ub, wx_ref[n * bw:(n + 1) * bw, :]))
    r = _sigmoid(jnp.concatenate(ra, axis=-1) + ba_ref[...])
    ig = _sigmoid(jnp.concatenate(rx, axis=-1) + bx_ref[...])

    nl = -lam_ref[...]
    softplus = jnp.maximum(nl, 0.0) + jnp.log1p(jnp.exp(-jnp.abs(nl)))
    rate = (-LRU_C) * softplus
    log_a = rate * r
    a = jnp.exp2((rate * LOG2E) * r)
    z = -jnp.tanh(log_a) * (a * a + 1.0)
    b_in = jnp.where(z > 0.0, z * lax.rsqrt(z), 0.0) * (ig * uc)

    levels = [(a_ref, b_ref, hs_ref), (a1_ref, b1_ref, h1_ref), (a2_ref, b2_ref, h2_ref),
              (a3_ref, b3_ref, h3_ref)]
    n_tiles = width // LANES
    for j in range(n_tiles):
        cols = slice(j * LANES, (j + 1) * LANES)
        a_ref[j] = a[:, cols]
        b_ref[j] = b_in[:, cols]
    for j in range(n_tiles):
        cols = slice(j * LANES, (j + 1) * LANES)
        _scan_level(levels, 0, j, h_ref[:, cols], grouped=True)
        h_ref[:, cols] = hs_ref[j, tb - 1:tb, :]
    hs = jnp.concatenate([hs_ref[j] for j in range(n_tiles)], axis=-1)
    hy = (hs * y).astype(MXU_DTYPE)

    o_ref[...] = x + _dot(hy, wout_ref[...])


def _lru_mixer(x2, batch, norm_w, w_in_all, conv_w, conv_b, wa_all, ba, wx_all, bx, lam, w_out_all,
               layer, ffn_w_in_all, ffn_w_out_all, ffn_layer):
    t, d = x2.shape
    width = w_out_all.shape[1]
    n_blocks, bw = wa_all.shape[1], wa_all.shape[2]
    taps = conv_w.shape[0]
    wa_all = wa_all.reshape(wa_all.shape[0], n_blocks * bw, bw)
    wx_all = wx_all.reshape(wx_all.shape[0], n_blocks * bw, bw)
    tb = LRU_BLOCK
    spb = t // batch // tb
    assert spb * tb * batch == t and taps - 1 <= SUBLANES and tb % (SUBLANES * SCAN_RADIX) == 0
    step_of = lambda b, s: b * spb + s
    row = pl.BlockSpec((tb, d), lambda b, s: (step_of(b, s), 0))
    vec = lambda v: v.reshape(1, width)
    cast = _FfnWeightCast(ffn_w_in_all, ffn_w_out_all, ffn_layer, batch * spb)
    return pl.pallas_call(
        functools.partial(_lru_kernel, layer=layer, width=width, n_blocks=n_blocks),
        grid=(batch, spb),
        in_specs=[row, _const_spec((1, d)), _HBM, _const_spec(conv_w.shape),
                  _const_spec((1, width)), _HBM, _const_spec((1, width)),
                  _HBM, _const_spec((1, width)), _const_spec((1, width)), _HBM]
        + cast.in_specs(step_of),
        out_specs=[row] + cast.out_specs(step_of),
        out_shape=[jax.ShapeDtypeStruct((t, d), jnp.float32)] + cast.out_shapes(),
        scratch_shapes=[
            pltpu.VMEM((1, width), jnp.float32),
            pltpu.VMEM((width // LANES, SUBLANES + tb, LANES), jnp.float32),
        ] + [pltpu.VMEM((width // LANES, tb // SCAN_RADIX ** lvl, LANES), jnp.float32)
             for lvl in range(SCAN_LEVELS) for _ in "abh"]
        + _weight_scratch(*w_in_all.shape[1:]) + _weight_scratch(*wa_all.shape[1:])
        + _weight_scratch(*wx_all.shape[1:]) + _weight_scratch(*w_out_all.shape[1:]),
        compiler_params=pltpu.CompilerParams(dimension_semantics=("arbitrary", "arbitrary"),
                                             vmem_limit_bytes=VMEM_LIMIT_BYTES),
        name="rglru_mixer",
    )(x2, norm_w.reshape(1, d), w_in_all, conv_w, vec(conv_b), wa_all, vec(ba), wx_all, vec(bx),
      vec(lam), w_out_all, *cast.arrays)


def kernel(x, norm_mix, norm_ffn, norm_final, hgrn_w_in, hgrn_lb, hgrn_norm, hgrn_w_out, lru_w_in, lru_conv_w, lru_conv_b, lru_wa, lru_ba, lru_wx, lru_bx, lru_lambda, lru_w_out, ffn_w_in, ffn_w_out):
    batch, seq, d = x.shape
    depth = norm_mix.shape[0]
    n_mixers = 2

    def ffn(l, h, w_in, w_out):
        return _ffn(h, norm_ffn[l], w_in, w_out, norm_final, final_norm=(l == depth - 1))

    def layers_from(l, h):
        if l == depth:
            return h
        j = l // n_mixers
        if l % n_mixers == 1:
            mixed = _lru_mixer(h, batch, norm_mix[l], lru_w_in, lru_conv_w[j], lru_conv_b[j],
                               lru_wa, lru_ba[j], lru_wx, lru_bx[j], lru_lambda[j], lru_w_out, j,
                               ffn_w_in, ffn_w_out, l)
            return layers_from(l + 1, ffn(l, *mixed))
        hgrn = functools.partial(_hgrn_mixer, h, batch, norm_mix[l], hgrn_w_in, hgrn_lb,
                                 hgrn_norm[j], hgrn_w_out, j, ffn_w_in, ffn_w_out, l)
        mixed, worst, w_in, w_out = hgrn(single_ref=True)

        def any_decay():
            mixed, _, w_in, w_out = hgrn(single_ref=False)
            return layers_from(l + 1, ffn(l, mixed, w_in, w_out))

        return lax.cond(jnp.min(worst) >= -HGRN_SAFE_EXP,
                        lambda: layers_from(l + 1, ffn(l, mixed, w_in, w_out)), any_decay)

    return layers_from(0, x.reshape(batch * seq, d)).reshape(batch, seq, d)
```

```python
import functools
import math

import jax
import jax.numpy as jnp
from jax import lax
from jax.experimental import pallas as pl
from jax.experimental.pallas import tpu as pltpu

EPS = 1e-6
LRU_C = 8.0
LOG2E = math.log2(math.e)
MXU_DTYPE = jnp.bfloat16
SUBLANES = 8
LANES = 128
SCAN_RADIX = 4
SCAN_LEVELS = 4

HGRN_CHUNK = 128
HGRN_SCAN_LEVELS = 3
HGRN_SAFE_EXP = 80.0
HGRN_LOG_FLOOR = -1e4
HGRN_BLOCK = 512
LRU_BLOCK = 512
FFN_BLOCK = 512
FFN_TILE = 256
VMEM_LIMIT_BYTES = 56 * 1024 * 1024
BF16_ROWS = 16
WEIGHT_CHUNK_BYTES = 2 * 1024 * 1024


def _dot(a, b):
    return jnp.dot(a, b, preferred_element_type=jnp.float32)


def _dot_nt(a, b):
    return lax.dot_general(a, b, (((1,), (1,)), ((), ())), preferred_element_type=jnp.float32)


def _dot_tn(a, b):
    return lax.dot_general(a, b, (((0,), (0,)), ((), ())), preferred_element_type=jnp.float32)


def _rmsnorm(x, w):
    return x * lax.rsqrt(jnp.mean(x * x, axis=-1, keepdims=True) + EPS) * w


def _sigmoid(x):
    return 0.5 + 0.5 * jnp.tanh(0.5 * x)


def _split(x, terms):
    parts = []
    for i in range(terms):
        p = x.astype(MXU_DTYPE)
        parts.append(p)
        if i + 1 < terms:
            x = x - p.astype(jnp.float32)
    return parts


def _select_sum(sel, parts):
    out = _dot(sel, parts[0])
    for p in parts[1:]:
        out = out + _dot(sel, p)
    return out


def _const_spec(shape):
    return pl.BlockSpec(shape, lambda *_: (0,) * len(shape), pipeline_mode=pl.Buffered(1))


_HBM = pl.BlockSpec(memory_space=pl.ANY)


def _chunk_rows(rows, cols):
    r = max(BF16_ROWS, min(rows, WEIGHT_CHUNK_BYTES // (cols * 4)))
    while rows % r or r % BF16_ROWS:
        r -= 1
    return r


def _weight_scratch(rows, cols):
    return [pltpu.VMEM((rows, cols), MXU_DTYPE),
            pltpu.VMEM((2, _chunk_rows(rows, cols), cols), jnp.float32),
            pltpu.SemaphoreType.DMA((2,))]


def _load_weight(w_hbm, w_ref, stage_ref, sem_ref):
    rows = stage_ref.shape[1]
    n = w_hbm.shape[0] // rows

    def chunk_copy(k):
        return pltpu.make_async_copy(w_hbm.at[pl.ds(k * rows, rows)], stage_ref.at[k % 2], sem_ref.at[k % 2])

    chunk_copy(0).start()
    for k in range(n):
        if k + 1 < n:
            chunk_copy(k + 1).start()
        chunk_copy(k).wait()
        w_ref[k * rows:(k + 1) * rows, :] = stage_ref[k % 2].astype(MXU_DTYPE)


def _ffn_kernel(x_ref, nw_ref, wi_ref, wo_ref, fnw_ref, o_ref, act_ref, *, d_ff, final_norm):
    x = x_ref[...]
    hn = _rmsnorm(x, nw_ref[...]).astype(MXU_DTYPE)
    for j in range(d_ff // FFN_TILE):
        lo, hi = j * FFN_TILE, (j + 1) * FFN_TILE
        g = _dot(hn, wi_ref[:, lo:hi])
        u = _dot(hn, wi_ref[:, d_ff + lo:d_ff + hi])
        act_ref[:, lo:hi] = (g * _sigmoid(g) * u).astype(MXU_DTYPE)
    acc = x + _dot(act_ref[...], wo_ref[...])
    if final_norm:
        acc = _rmsnorm(acc, fnw_ref[...])
    o_ref[...] = acc


def _ffn(x2, norm_w, w_in, w_out, final_norm_w, final_norm):
    t, d = x2.shape
    d_ff = w_out.shape[0]
    assert d_ff % FFN_TILE == 0 and t % FFN_BLOCK == 0
    row = pl.BlockSpec((FFN_BLOCK, d), lambda i: (i, 0))
    return pl.pallas_call(
        functools.partial(_ffn_kernel, d_ff=d_ff, final_norm=final_norm),
        grid=(t // FFN_BLOCK,),
        in_specs=[row, _const_spec((1, d)), _const_spec(w_in.shape), _const_spec(w_out.shape),
                  _const_spec((1, d))],
        out_specs=row,
        out_shape=jax.ShapeDtypeStruct((t, d), jnp.float32),
        scratch_shapes=[pltpu.VMEM((FFN_BLOCK, d_ff), MXU_DTYPE)],
        compiler_params=pltpu.CompilerParams(dimension_semantics=("arbitrary",),
                                             vmem_limit_bytes=VMEM_LIMIT_BYTES),
        name="swiglu_ffn",
    )(x2, norm_w.reshape(1, d), w_in, w_out, final_norm_w.reshape(1, d))


class _FfnWeightCast:
    def __init__(self, w_in_all, w_out_all, layer, n_steps):
        self.arrays = [w_in_all, w_out_all]
        self.layer = layer
        self.plan = []
        for w in self.arrays:
            rows, share = w.shape[1], 1
            while (rows * share) % (n_steps * BF16_ROWS):
                share *= 2
                assert share <= n_steps, "weight rows do not split into packed-row blocks"
            self.plan.append((rows * share // n_steps, share))

    def in_specs(self, step_of):
        return [pl.BlockSpec((None, r, w.shape[2]), lambda *g, _k=k: (self.layer, step_of(*g) // _k, 0))
                for w, (r, k) in zip(self.arrays, self.plan)]

    def out_specs(self, step_of):
        return [pl.BlockSpec((r, w.shape[2]), lambda *g, _k=k: (step_of(*g) // _k, 0))
                for w, (r, k) in zip(self.arrays, self.plan)]

    def out_shapes(self):
        return [jax.ShapeDtypeStruct(w.shape[1:], MXU_DTYPE) for w in self.arrays]


def _cast_slices(in_refs, out_refs):
    for i_ref, o_ref in zip(in_refs, out_refs):
        o_ref[...] = i_ref[...].astype(MXU_DTYPE)


def _hgrn_scores_single_ref(qd, kd, heads, dk, sc_ref):
    c = qd.shape[0]
    row = lax.broadcasted_iota(jnp.int32, (c, c), 0)
    col = lax.broadcasted_iota(jnp.int32, (c, c), 1)
    causal = col <= row
    for h in range(heads):
        sl = slice(h * dk, (h + 1) * dk)
        s = _dot_nt(qd[:, sl], kd[:, sl])
        sc_ref[h] = jnp.where(causal, s, 0.0).astype(MXU_DTYPE)


def _hgrn_scores_halving(q_c, k_c, lf_parts, heads, dk, sc_ref):
    c = q_c.shape[0]
    row = lax.broadcasted_iota(jnp.int32, (c, c), 0)
    col = lax.broadcasted_iota(jnp.int32, (c, c), 1)
    acc = []
    for h in range(heads):
        sl = slice(h * dk, (h + 1) * dk)
        s = _dot_nt(q_c[:, sl].astype(MXU_DTYPE), k_c[:, sl].astype(MXU_DTYPE))
        acc.append(jnp.where(row == col, s, 0.0))
    m = c // 2
    while m >= 1:
        blk = 2 * m
        ref_of_row = (row // blk) * blk + (m - 1)
        ref_of_col = (col // blk) * blk + (m - 1)
        row_upper = (row % blk) >= m
        sel_q = (row_upper & (col > ref_of_row) & (col <= row)).astype(MXU_DTYPE)
        sel_k = ((~row_upper) & (col > row) & (col <= ref_of_row)).astype(MXU_DTYPE)
        qd = (q_c * jnp.exp(_select_sum(sel_q, lf_parts))).astype(MXU_DTYPE)
        kd = (k_c * jnp.exp(_select_sum(sel_k, lf_parts))).astype(MXU_DTYPE)
        pair = (ref_of_row == ref_of_col) & row_upper & ((col % blk) < m)
        for h in range(heads):
            sl = slice(h * dk, (h + 1) * dk)
            acc[h] = acc[h] + jnp.where(pair, _dot_nt(qd[:, sl], kd[:, sl]), 0.0)
        m //= 2
    for h in range(heads):
        sc_ref[h] = acc[h].astype(MXU_DTYPE)


def _hgrn_kernel(x_ref, nw_ref, win_hbm, lbp_ref, gnw_ref, wout_hbm, fwi_f32, fwo_f32,
                 o_ref, worst_ref, fwi_ref, fwo_ref, st_ref, proj_ref, k_ref, sc_ref, y_ref,
                 win_ref, win_stage, win_sem, wout_ref, wout_stage, wout_sem, *decay_refs,
                 layer, heads, dk, single_ref):
    kd_total = heads * dk
    tb = x_ref.shape[0]
    c = HGRN_CHUNK
    n_chunks = tb // c
    first = (pl.program_id(0) == 0) & (pl.program_id(1) == 0)

    @pl.when(pl.program_id(1) == 0)
    def _():
        st_ref[...] = jnp.zeros_like(st_ref)

    decay_levels = [tuple(decay_refs[2 * lvl:2 * lvl + 2]) for lvl in range(len(decay_refs) // 2)]

    @pl.when(first)
    def _():
        worst_ref[...] = jnp.zeros_like(worst_ref)
        _load_weight(win_hbm.at[layer], win_ref, win_stage, win_sem)
        _load_weight(wout_hbm.at[layer], wout_ref, wout_stage, wout_sem)

    _cast_slices((fwi_f32, fwo_f32), (fwi_ref, fwo_ref))
    x = x_ref[...]
    hn = _rmsnorm(x, nw_ref[...]).astype(MXU_DTYPE)
    proj = _dot(hn, win_ref[...])

    lp = lbp_ref[...]
    e = jnp.exp(lp - jnp.max(lp, axis=0, keepdims=True))
    lb = jnp.sum(e[: layer + 1], axis=0, keepdims=True) / jnp.sum(e, axis=0, keepdims=True)
    one_m_lb = 1.0 - lb

    q = proj[:, :kd_total]
    fl = proj[:, kd_total:2 * kd_total]
    g = proj[:, 3 * kd_total:]
    half_t = 0.5 * jnp.tanh(0.5 * fl)
    kk = one_m_lb * (0.5 - half_t)
    f = lb + one_m_lb * (0.5 + half_t)

    proj_ref[:, :kd_total] = q * _sigmoid(q)
    proj_ref[:, kd_total:2 * kd_total] = f if single_ref else jnp.maximum(jnp.log(f), HGRN_LOG_FLOOR)
    proj_ref[:, 2 * kd_total:3 * kd_total] = proj[:, 2 * kd_total:3 * kd_total]
    proj_ref[:, 3 * kd_total:] = g * _sigmoid(g)
    k_ref[...] = kk

    row = lax.broadcasted_iota(jnp.int32, (c, c), 0)
    col = lax.broadcasted_iota(jnp.int32, (c, c), 1)
    tril = (col <= row).astype(MXU_DTYPE)
    gnw = gnw_ref[...]

    n_tiles = kd_total // LANES
    for ci in range(n_chunks):
        rows = slice(ci * c, (ci + 1) * c)
        q_c = proj_ref[rows, :kd_total]
        k_c = k_ref[rows, :]
        if single_ref:
            f_c = proj_ref[rows, kd_total:2 * kd_total]
            fa_ref, eb_ref = decay_levels[0]
            for j in range(n_tiles):
                fa_ref[j] = f_c[:, j * LANES:(j + 1) * LANES]
            for j in range(n_tiles):
                _cumprod_level(decay_levels, 0, j)
            eb = jnp.concatenate([eb_ref[j] for j in range(n_tiles)], axis=-1)
            dec = eb[c - 1:c, :]
            b_last = jnp.log(dec)
        else:
            lf_parts = _split(proj_ref[rows, kd_total:2 * kd_total], 3)
            b = _select_sum(tril, lf_parts)
            b_last = b[c - 1:c, :]
            dec = jnp.exp(b_last)
            eb = jnp.exp(b)
        qd = (q_c * eb).astype(MXU_DTYPE)
        if single_ref:
            kd = (k_c * (1.0 / eb)).astype(MXU_DTYPE)
            _hgrn_scores_single_ref(qd, kd, heads, dk, sc_ref)
            ke = kd * dec.astype(MXU_DTYPE)
            folded = b_last
            width = kd_total
            while width > worst_ref.shape[1]:
                width //= 2
                folded = jnp.minimum(folded[:, :width], folded[:, width:])
            worst_ref[...] = jnp.minimum(worst_ref[...], folded)
        else:
            _hgrn_scores_halving(q_c, k_c, lf_parts, heads, dk, sc_ref)
            ke = (k_c * jnp.exp(b_last - b)).astype(MXU_DTYPE)
        for h in range(heads):
            sl = slice(h * dk, (h + 1) * dk)
            v_h = proj_ref[rows, 2 * kd_total + h * dk:2 * kd_total + (h + 1) * dk]
            st = st_ref[h]
            o = _dot(sc_ref[h], v_h.astype(MXU_DTYPE)) + _dot_nt(qd[:, sl], st.astype(MXU_DTYPE))
            st_ref[h] = st * dec[:, sl] + _dot_tn(v_h.astype(MXU_DTYPE), ke[:, sl])
            o = o * lax.rsqrt(jnp.mean(o * o, axis=-1, keepdims=True) + EPS) * gnw
            sg = proj_ref[rows, 3 * kd_total + h * dk:3 * kd_total + (h + 1) * dk]
            y_ref[rows, sl] = (o * sg).astype(MXU_DTYPE)

    o_ref[...] = x + _dot(y_ref[...], wout_ref[...])


def _hgrn_mixer(x2, batch, norm_w, w_in_all, lb_param, gn_w, w_out_all, layer, ffn_w_in_all,
                ffn_w_out_all, ffn_layer, single_ref):
    t, d = x2.shape
    dv = gn_w.shape[-1]
    w_in_shape, w_out_shape = w_in_all.shape[1:], w_out_all.shape[1:]
    heads = w_out_shape[0] // dv
    kd_total = (w_in_shape[1] - 2 * heads * dv) // 2
    dk = kd_total // heads
    assert dk == dv and kd_total == heads * dv, "kernel assumes key dim == value dim"
    tb = HGRN_BLOCK
    spb = t // batch // tb
    assert spb * tb * batch == t and tb % HGRN_CHUNK == 0
    step_of = lambda b, s: b * spb + s
    row = pl.BlockSpec((tb, d), lambda b, s: (step_of(b, s), 0))
    cast = _FfnWeightCast(ffn_w_in_all, ffn_w_out_all, ffn_layer, batch * spb)
    return pl.pallas_call(
        functools.partial(_hgrn_kernel, layer=layer, heads=heads, dk=dk, single_ref=single_ref),
        grid=(batch, spb),
        in_specs=[row, _const_spec((1, d)), _HBM, _const_spec(lb_param.shape),
                  _const_spec((1, dv)), _HBM] + cast.in_specs(step_of),
        out_specs=[row, pl.BlockSpec((1, dk), lambda b, s: (0, 0))] + cast.out_specs(step_of),
        out_shape=[jax.ShapeDtypeStruct((t, d), jnp.float32),
                   jax.ShapeDtypeStruct((1, dk), jnp.float32)] + cast.out_shapes(),
        scratch_shapes=[
            pltpu.VMEM((heads, dv, dk), jnp.float32),
            pltpu.VMEM((tb, w_in_shape[1]), jnp.float32),
            pltpu.VMEM((tb, kd_total), jnp.float32),
            pltpu.VMEM((heads, HGRN_CHUNK, HGRN_CHUNK), MXU_DTYPE),
            pltpu.VMEM((tb, heads * dv), MXU_DTYPE),
        ] + _weight_scratch(*w_in_shape) + _weight_scratch(*w_out_shape)
        + [pltpu.VMEM((kd_total // LANES, HGRN_CHUNK // SCAN_RADIX ** lvl, LANES), jnp.float32)
           for lvl in range(HGRN_SCAN_LEVELS) for _ in "ap"],
        compiler_params=pltpu.CompilerParams(dimension_semantics=("arbitrary", "arbitrary"),
                                             vmem_limit_bytes=VMEM_LIMIT_BYTES),
        name="hgrn2_mixer" if single_ref else "hgrn2_mixer_any_decay",
    )(x2, norm_w.reshape(1, d), w_in_all, lb_param, gn_w.reshape(1, dv), w_out_all, *cast.arrays)


def _gelu_tanh(x):
    c = math.sqrt(2.0 / math.pi)
    return (0.5 * x) * (1.0 + jnp.tanh(x * (c + (c * 0.044715) * (x * x))))


def _scan_rows(a, b, n):
    rows = lax.broadcasted_iota(jnp.int32, a.shape, 0)
    d = 1
    while d < n:
        live = rows >= d
        a_sh = jnp.where(live, pltpu.roll(a, d, axis=0), 1.0)
        b_sh = jnp.where(live, pltpu.roll(b, d, axis=0), 0.0)
        b = a * b_sh + b
        a = a * a_sh
        d *= 2
    return a, b


def _scan_level(levels, lvl, j, h0, grouped=False):
    a_ref, b_ref, h_ref = levels[lvl]
    n = a_ref.shape[1]
    if lvl + 1 == len(levels) or n % SCAN_RADIX or n <= SUBLANES:
        a_inc, b_inc = _scan_rows(a_ref[j], b_ref[j], n)
        h_ref[j] = a_inc * h0 + b_inc
        return
    m = n // SCAN_RADIX
    step = lambda r: pl.ds(r, m, stride=SCAN_RADIX)
    read = (lambda r: pl.ds(r * m, m)) if grouped else step
    a_cum = [a_ref[j, read(0), :]]
    h_loc = [b_ref[j, read(0), :]]
    for r in range(1, SCAN_RADIX):
        a_r = a_ref[j, read(r), :]
        h_loc.append(a_r * h_loc[-1] + b_ref[j, read(r), :])
        a_cum.append(a_r * a_cum[-1])
    a_next, b_next, h_next = levels[lvl + 1]
    a_next[j] = a_cum[-1]
    b_next[j] = h_loc[-1]
    _scan_level(levels, lvl + 1, j, h0)
    group = lax.broadcasted_iota(jnp.int32, (m, LANES), 0)
    h_in = jnp.where(group >= 1, pltpu.roll(h_next[j], 1, axis=0), h0)
    for r in range(SCAN_RADIX):
        h_ref[j, step(r), :] = h_loc[r] + a_cum[r] * h_in


def _cumprod_level(levels, lvl, j):
    a_ref, p_ref = levels[lvl]
    n = a_ref.shape[1]
    if lvl + 1 == len(levels) or n % SCAN_RADIX or n <= SUBLANES:
        a = a_ref[j]
        rows = lax.broadcasted_iota(jnp.int32, a.shape, 0)
        d = 1
        while d < n:
            a = a * jnp.where(rows >= d, pltpu.roll(a, d, axis=0), 1.0)
            d *= 2
        p_ref[j] = a
        return
    m = n // SCAN_RADIX
    step = lambda r: pl.ds(r, m, stride=SCAN_RADIX)
    a_cum = [a_ref[j, step(0), :]]
    for r in range(1, SCAN_RADIX):
        a_cum.append(a_ref[j, step(r), :] * a_cum[-1])
    a_next, p_next = levels[lvl + 1]
    a_next[j] = a_cum[-1]
    _cumprod_level(levels, lvl + 1, j)
    group = lax.broadcasted_iota(jnp.int32, (m, LANES), 0)
    p_in = jnp.where(group >= 1, pltpu.roll(p_next[j], 1, axis=0), 1.0)
    for r in range(SCAN_RADIX):
        p_ref[j, step(r), :] = a_cum[r] * p_in


def _conv_grouped(u, ext_ref, cw_ref, cb_ref):
    tb, width = u.shape
    taps = cw_ref.shape[0]
    pad = ext_ref.shape[1] - tb
    m = tb // SCAN_RADIX
    cw = cw_ref[...]
    cb = cb_ref[...]
    tiles = []
    for j in range(width // LANES):
        cols = slice(j * LANES, (j + 1) * LANES)
        ext_ref[j, pad:, :] = u[:, cols]
        steps = []
        for r in range(SCAN_RADIX):
            acc = cb[:, cols]
            for t in range(taps):
                shifted = ext_ref[j, pl.ds(pad + r - t, m, stride=SCAN_RADIX), :]
                acc = acc + shifted * cw[taps - 1 - t:taps - t, cols]
            steps.append(acc)
        ext_ref[j, :pad, :] = u[tb - pad:, cols]
        tiles.append(jnp.concatenate(steps, axis=0))
    return jnp.concatenate(tiles, axis=-1)


def _lru_kernel(x_ref, nw_ref, win_hbm, cw_ref, cb_ref, wa_hbm, ba_ref, wx_hbm, bx_ref, lam_ref,
                wout_hbm, fwi_f32, fwo_f32, o_ref, fwi_ref, fwo_ref, h_ref, ext_ref, a_ref, b_ref, hs_ref,
                a1_ref, b1_ref, h1_ref, a2_ref, b2_ref, h2_ref, a3_ref, b3_ref, h3_ref,
                win_ref, win_stage, win_sem, wa_ref, wa_stage, wa_sem, wx_ref, wx_stage, wx_sem,
                wout_ref, wout_stage, wout_sem, *, layer, width, n_blocks):
    tb = x_ref.shape[0]
    bw = width // n_blocks

    @pl.when((pl.program_id(0) == 0) & (pl.program_id(1) == 0))
    def _():
        _load_weight(win_hbm.at[layer], win_ref, win_stage, win_sem)
        _load_weight(wa_hbm.at[layer], wa_ref, wa_stage, wa_sem)
        _load_weight(wx_hbm.at[layer], wx_ref, wx_stage, wx_sem)
        _load_weight(wout_hbm.at[layer], wout_ref, wout_stage, wout_sem)

    @pl.when(pl.program_id(1) == 0)
    def _():
        h_ref[...] = jnp.zeros_like(h_ref)
        ext_ref[:, :SUBLANES, :] = jnp.zeros((ext_ref.shape[0], SUBLANES, LANES), jnp.float32)

    _cast_slices((fwi_f32, fwo_f32), (fwi_ref, fwo_ref))
    x = x_ref[...]
    hn = _rmsnorm(x, nw_ref[...]).astype(MXU_DTYPE)
    proj = _dot(hn, win_ref[...])
    y = _gelu_tanh(proj[:, :width])
    u = proj[:, width:]

    uc = _conv_grouped(u, ext_ref, cw_ref, cb_ref)

    ra, rx = [], []
    for n in range(n_blocks):
        ub = uc[:, n * bw:(n + 1) * bw].astype(MXU_DTYPE)
        ra.append(_dot(ub, wa_ref[n * bw:(n + 1) * bw, :]))
        rx.append(_dot(ub, wx_ref[n * bw:(n + 1) * bw, :]))
    r = _sigmoid(jnp.concatenate(ra, axis=-1) + ba_ref[...])
    ig = _sigmoid(jnp.concatenate(rx, axis=-1) + bx_ref[...])

    nl = -lam_ref[...]
    softplus = jnp.maximum(nl, 0.0) + jnp.log1p(jnp.exp(-jnp.abs(nl)))
    rate = (-LRU_C) * softplus
    log_a = rate * r
    a = jnp.exp2((rate * LOG2E) * r)
    z = -jnp.tanh(log_a) * (a * a + 1.0)
    b_in = jnp.where(z > 0.0, z * lax.rsqrt(z), 0.0) * (ig * uc)

    levels = [(a_ref, b_ref, hs_ref), (a1_ref, b1_ref, h1_ref), (a2_ref, b2_ref, h2_ref),
              (a3_ref, b3_ref, h3_ref)]
    n_tiles = width // LANES
    for j in range(n_tiles):
        cols = slice(j * LANES, (j + 1) * LANES)
        a_ref[j] = a[:, cols]
        b_ref[j] = b_in[:, cols]
    for j in range(n_tiles):
        cols = slice(j * LANES, (j + 1) * LANES)
        _scan_level(levels, 0, j, h_ref[:, cols], grouped=True)
        h_ref[:, cols] = hs_ref[j, tb - 1:tb, :]
    hs = jnp.concatenate([hs_ref[j] for j in range(n_tiles)], axis=-1)
    hy = (hs * y).astype(MXU_DTYPE)

    o_ref[...] = x + _dot(hy, wout_ref[...])


def _lru_mixer(x2, batch, norm_w, w_in_all, conv_w, conv_b, wa_all, ba, wx_all, bx, lam, w_out_all,
               layer, ffn_w_in_all, ffn_w_out_all, ffn_layer):
    t, d = x2.shape
    width = w_out_all.shape[1]
    n_blocks, bw = wa_all.shape[1], wa_all.shape[2]
    taps = conv_w.shape[0]
    wa_all = wa_all.reshape(wa_all.shape[0], n_blocks * bw, bw)
    wx_all = wx_all.reshape(wx_all.shape[0], n_blocks * bw, bw)
    tb = LRU_BLOCK
    spb = t // batch // tb
    assert spb * tb * batch == t and taps - 1 <= SUBLANES and tb % (SUBLANES * SCAN_RADIX) == 0
    step_of = lambda b, s: b * spb + s
    row = pl.BlockSpec((tb, d), lambda b, s: (step_of(b, s), 0))
    vec = lambda v: v.reshape(1, width)
    cast = _FfnWeightCast(ffn_w_in_all, ffn_w_out_all, ffn_layer, batch * spb)
    return pl.pallas_call(
        functools.partial(_lru_kernel, layer=layer, width=width, n_blocks=n_blocks),
        grid=(batch, spb),
        in_specs=[row, _const_spec((1, d)), _HBM, _const_spec(conv_w.shape),
                  _const_spec((1, width)), _HBM, _const_spec((1, width)),
                  _HBM, _const_spec((1, width)), _const_spec((1, width)), _HBM]
        + cast.in_specs(step_of),
        out_specs=[row] + cast.out_specs(step_of),
        out_shape=[jax.ShapeDtypeStruct((t, d), jnp.float32)] + cast.out_shapes(),
        scratch_shapes=[
            pltpu.VMEM((1, width), jnp.float32),
            pltpu.VMEM((width // LANES, SUBLANES + tb, LANES), jnp.float32),
        ] + [pltpu.VMEM((width // LANES, tb // SCAN_RADIX ** lvl, LANES), jnp.float32)
             for lvl in range(SCAN_LEVELS) for _ in "abh"]
        + _weight_scratch(*w_in_all.shape[1:]) + _weight_scratch(*wa_all.shape[1:])
        + _weight_scratch(*wx_all.shape[1:]) + _weight_scratch(*w_out_all.shape[1:]),
        compiler_params=pltpu.CompilerParams(dimension_semantics=("arbitrary", "arbitrary"),
                                             vmem_limit_bytes=VMEM_LIMIT_BYTES),
        name="rglru_mixer",
    )(x2, norm_w.reshape(1, d), w_in_all, conv_w, vec(conv_b), wa_all, vec(ba), wx_all, vec(bx),
      vec(lam), w_out_all, *cast.arrays)


def kernel(x, norm_mix, norm_ffn, norm_final, hgrn_w_in, hgrn_lb, hgrn_norm, hgrn_w_out, lru_w_in, lru_conv_w, lru_conv_b, lru_wa, lru_ba, lru_wx, lru_bx, lru_lambda, lru_w_out, ffn_w_in, ffn_w_out):
    batch, seq, d = x.shape
    depth = norm_mix.shape[0]
    n_mixers = 2

    def ffn(l, h, w_in, w_out):
        return _ffn(h, norm_ffn[l], w_in, w_out, norm_final, final_norm=(l == depth - 1))

    def layers_from(l, h):
        if l == depth:
            return h
        j = l // n_mixers
        if l % n_mixers == 1:
            mixed = _lru_mixer(h, batch, norm_mix[l], lru_w_in, lru_conv_w[j], lru_conv_b[j],
                               lru_wa, lru_ba[j], lru_wx, lru_bx[j], lru_lambda[j], lru_w_out, j,
                               ffn_w_in, ffn_w_out, l)
            return layers_from(l + 1, ffn(l, *mixed))
        hgrn = functools.partial(_hgrn_mixer, h, batch, norm_mix[l], hgrn_w_in, hgrn_lb,
                                 hgrn_norm[j], hgrn_w_out, j, ffn_w_in, ffn_w_out, l)
        mixed, worst, w_in, w_out = hgrn(single_ref=True)

        def any_decay():
            mixed, _, w_in, w_out = hgrn(single_ref=False)
            return layers_from(l + 1, ffn(l, mixed, w_in, w_out))

        return lax.cond(jnp.min(worst) >= -HGRN_SAFE_EXP,
                        lambda: layers_from(l + 1, ffn(l, mixed, w_in, w_out)), any_decay)

    return layers_from(0, x.reshape(batch * seq, d)).reshape(batch, seq, d)
```
